```python
import math
import jax, jax.numpy as jnp
from jax import lax
import numpy as np

D_MODEL = 1024
BATCH = 8
SEQ = 2048
DEPTH = 4

N_MIXERS = 3
BLOCK = 128
EPS = 1e-6
NEG = -1e30

SB_HEADS = 16
SB_HEAD_DIM = 64

DIL_GROUPS = ((128, 1), (512, 4), (2048, 16))
DIL_HEADS = 8
DIL_HEAD_DIM = 64
N_BUCKETS = 32
BUCKET_MAX_DIST = 2048

MLA_HEADS = 16
MLA_Q_RANK = 384
MLA_KV_RANK = 256
MLA_NOPE = 64
MLA_ROPE = 32
MLA_V = 64
ROPE_THETA = 10000.0

D_FF = 2816
CONV_W = 3

kernel_name = 'hybrid_sb_dilated_mla_convffn'


def rms_norm(x, g):
    xf = x.astype(jnp.float32)
    y = xf * lax.rsqrt(jnp.mean(xf * xf, axis=-1, keepdims=True) + EPS)
    return (y * g.astype(jnp.float32)).astype(x.dtype)


def t5_bucket(dist):
    dist = jnp.maximum(dist, 0)
    max_exact = N_BUCKETS // 2
    d = jnp.maximum(dist.astype(jnp.float32), 1.0)
    large = max_exact + (jnp.log(d / max_exact) / math.log(BUCKET_MAX_DIST / max_exact)
                         * (N_BUCKETS - max_exact)).astype(jnp.int32)
    large = jnp.minimum(large, N_BUCKETS - 1)
    return jnp.where(dist < max_exact, dist, large)


def stick_breaking_attention(xn, w_qkv, w_o):
    B, S, _ = xn.shape
    H, dh = SB_HEADS, SB_HEAD_DIM
    qkv = (xn @ w_qkv).reshape(B, S, 3, H, dh)
    q, k, v = qkv[:, :, 0], qkv[:, :, 1], qkv[:, :, 2]
    nb = S // BLOCK
    qb = q.reshape(B, nb, BLOCK, H, dh).transpose(1, 0, 3, 2, 4)
    kt = k.transpose(0, 2, 1, 3)
    vt = v.transpose(0, 2, 1, 3)
    key_pos = jnp.arange(S)
    scale = dh ** -0.5

    def one_block(args):
        q_blk, n = args
        z = jnp.einsum('bhqd,bhkd->bhqk', q_blk, kt).astype(jnp.float32) * scale
        q_pos = n * BLOCK + jnp.arange(BLOCK)
        mask = key_pos[None, :] < q_pos[:, None]
        log_not = jnp.where(mask, jax.nn.log_sigmoid(-z), 0.0)
        between = lax.cumsum(log_not, axis=3, reverse=True) - log_not
        a = jnp.where(mask, jnp.exp(jax.nn.log_sigmoid(z) + between), 0.0)
        return jnp.einsum('bhqk,bhkd->bhqd', a.astype(vt.dtype), vt)

    o = lax.map(one_block, (qb, jnp.arange(nb)))
    o = o.transpose(1, 0, 3, 2, 4).reshape(B, S, H * dh)
    return o @ w_o


def strided_window_attention(q, k, v, dilation, n_back, bias_g):
    B, S, Hg, dh = q.shape
    r, W = dilation, n_back
    L = S // r

    def to_sub(a):
        return a.reshape(B, L, r, Hg, dh).transpose(0, 2, 3, 1, 4)

    qs, ks, vs = to_sub(q), to_sub(k), to_sub(v)
    Lp = -(-L // W) * W
    nb = Lp // W
    pad = Lp - L
    qs = jnp.pad(qs, ((0, 0), (0, 0), (0, 0), (0, pad), (0, 0)))
    kp = jnp.pad(ks, ((0, 0), (0, 0), (0, 0), (W, pad), (0, 0)))
    vp = jnp.pad(vs, ((0, 0), (0, 0), (0, 0), (W, pad), (0, 0)))
    q_blk = qs.reshape(B, r, Hg, nb, W, dh)
    k_blk = jnp.concatenate([kp[:, :, :, :Lp].reshape(B, r, Hg, nb, W, dh),
                             kp[:, :, :, W:].reshape(B, r, Hg, nb, W, dh)], axis=4)
    v_blk = jnp.concatenate([vp[:, :, :, :Lp].reshape(B, r, Hg, nb, W, dh),
                             vp[:, :, :, W:].reshape(B, r, Hg, nb, W, dh)], axis=4)
    i = jnp.arange(W)[:, None]
    j = jnp.arange(2 * W)[None, :]
    m = i + W - j
    key_abs = jnp.arange(nb)[:, None, None] * W - W + j[None]
    mask = (m >= 0)[None] & (m <= n_back)[None] & (key_abs >= 0)
    bias = bias_g[t5_bucket(m * r)].astype(jnp.float32).transpose(2, 0, 1)
    s = jnp.einsum('brhnqd,brhnkd->brhnqk', q_blk, k_blk).astype(jnp.float32) * (dh ** -0.5)
    s = jnp.where(mask[None, None, None], s + bias[None, None, :, None], NEG)
    lse = jax.nn.logsumexp(s, axis=-1)
    p = jnp.exp(s - lse[..., None])
    o = jnp.einsum('brhnqk,brhnkd->brhnqd', p.astype(v_blk.dtype), v_blk)
    o = o.reshape(B, r, Hg, Lp, dh)[:, :, :, :L].transpose(0, 3, 1, 2, 4).reshape(B, S, Hg, dh)
    lse = lse.reshape(B, r, Hg, Lp)[:, :, :, :L].transpose(0, 3, 1, 2).reshape(B, S, Hg)
    return o, lse


def dilated_attention(xn, w_qkv, w_o, rel_bias):
    B, S, _ = xn.shape
    G, Hg, dh = len(DIL_GROUPS), DIL_HEADS, DIL_HEAD_DIM
    qkv = (xn @ w_qkv).reshape(B, S, 3, G, Hg, dh)
    outs, lses = [], []
    for g, (window, dilation) in enumerate(DIL_GROUPS):
        o, lse = strided_window_attention(qkv[:, :, 0, g], qkv[:, :, 1, g], qkv[:, :, 2, g],
                                          dilation, window // dilation,
                                          rel_bias[:, g * Hg:(g + 1) * Hg])
        outs.append(o)
        lses.append(lse)
    outs = jnp.stack(outs)
    wts = jax.nn.softmax(jnp.stack(lses), axis=0)
    o = jnp.einsum('gbshd,gbsh->bshd', outs, wts.astype(outs.dtype)).reshape(B, S, Hg * dh)
    return o @ w_o


def apply_rope(x, positions):
    half = x.shape[-1] // 2
    freqs = ROPE_THETA ** (-jnp.arange(half, dtype=jnp.float32) / half)
    ang = positions.astype(jnp.float32)[:, :, None, None] * freqs
    cos, sin = jnp.cos(ang).astype(x.dtype), jnp.sin(ang).astype(x.dtype)
    x1, x2 = x[..., :half], x[..., half:]
    return jnp.concatenate([x1 * cos - x2 * sin, x2 * cos + x1 * sin], axis=-1)


def causal_softmax_blocks(q, k, v, scale):
    B, S, H, dq = q.shape
    nb = S // BLOCK
    qb = q.reshape(B, nb, BLOCK, H, dq).transpose(1, 0, 3, 2, 4)
    kt = k.transpose(0, 2, 1, 3)
    vt = v.transpose(0, 2, 1, 3)
    key_pos = jnp.arange(S)

    def one_block(args):
        q_blk, n = args
        s = jnp.einsum('bhqd,bhkd->bhqk', q_blk, kt).astype(jnp.float32) * scale
        q_pos = n * BLOCK + jnp.arange(BLOCK)
        s = jnp.where(key_pos[None, :] <= q_pos[:, None], s, NEG)
        p = jax.nn.softmax(s, axis=-1)
        return jnp.einsum('bhqk,bhkd->bhqd', p.astype(vt.dtype), vt)

    o = lax.map(one_block, (qb, jnp.arange(nb)))
    return o.transpose(1, 0, 3, 2, 4).reshape(B, S, H, v.shape[-1])


def latent_attention(xn, positions, w_in, q_norm, w_qb, kv_norm, w_kvb, w_o):
    B, S, _ = xn.shape
    H = MLA_HEADS
    h = xn @ w_in
    c_q = h[..., :MLA_Q_RANK]
    c_kv = h[..., MLA_Q_RANK:MLA_Q_RANK + MLA_KV_RANK]
    k_rope = h[..., MLA_Q_RANK + MLA_KV_RANK:][:, :, None, :]
    q = (rms_norm(c_q, q_norm) @ w_qb).reshape(B, S, H, MLA_NOPE + MLA_ROPE)
    kv = (rms_norm(c_kv, kv_norm) @ w_kvb).reshape(B, S, H, MLA_NOPE + MLA_V)
    q_rope = apply_rope(q[..., MLA_NOPE:], positions)
    k_rope = apply_rope(k_rope, positions)
    q_full = jnp.concatenate([q[..., :MLA_NOPE], q_rope], axis=-1)
    k_full = jnp.concatenate([kv[..., :MLA_NOPE],
                              jnp.broadcast_to(k_rope, (B, S, H, MLA_ROPE))], axis=-1)
    o = causal_softmax_blocks(q_full, k_full, kv[..., MLA_NOPE:], (MLA_NOPE + MLA_ROPE) ** -0.5)
    return o.reshape(B, S, H * MLA_V) @ w_o


def conv_ffn(xn, w_up, conv_w, conv_b, w_down):
    h = xn @ w_up
    C = h.shape[-1]
    h = lax.conv_general_dilated(h, conv_w[:, None, :], window_strides=(1,),
                                 padding=[(CONV_W - 1, 0)],
                                 dimension_numbers=('NWC', 'WIO', 'NWC'),
                                 feature_group_count=C) + conv_b
    g, u = h[..., :D_FF], h[..., D_FF:]
    return (jax.nn.silu(g) * u) @ w_down


def setup_inputs(seed: int = 0) -> dict:
    key = jax.random.key(seed)
    ks = iter(jax.random.split(key, 32))
    n_a = len(range(0, DEPTH, N_MIXERS))
    n_b = len(range(1, DEPTH, N_MIXERS))
    n_c = len(range(2, DEPTH, N_MIXERS))
    G = len(DIL_GROUPS)

    def w(shape, fan_in):
        return jax.random.normal(next(ks), shape, jnp.float32) * fan_in ** -0.5

    def gain(shape):
        return 1.0 + 0.05 * jax.random.normal(next(ks), shape, jnp.float32)

    return {
        'x': jax.random.normal(next(ks), (BATCH, SEQ, D_MODEL), jnp.float32),
        'positions': jnp.broadcast_to(jnp.arange(SEQ, dtype=jnp.int32), (BATCH, SEQ)),
        'rel_bias': 0.5 * jax.random.normal(next(ks), (N_BUCKETS, G * DIL_HEADS), jnp.float32),
        'norm_gains': gain((DEPTH, 4, D_MODEL)),
        'sb_w_qkv': w((n_a, D_MODEL, 3 * SB_HEADS * SB_HEAD_DIM), D_MODEL),
        'sb_w_o': w((n_a, SB_HEADS * SB_HEAD_DIM, D_MODEL), SB_HEADS * SB_HEAD_DIM),
        'dil_w_qkv': w((n_b, D_MODEL, 3 * G * DIL_HEADS * DIL_HEAD_DIM), D_MODEL),
        'dil_w_o': w((n_b, DIL_HEADS * DIL_HEAD_DIM, D_MODEL), DIL_HEADS * DIL_HEAD_DIM),
        'mla_w_in': w((n_c, D_MODEL, MLA_Q_RANK + MLA_KV_RANK + MLA_ROPE), D_MODEL),
        'mla_q_norm': gain((n_c, MLA_Q_RANK)),
        'mla_w_qb': w((n_c, MLA_Q_RANK, MLA_HEADS * (MLA_NOPE + MLA_ROPE)), MLA_Q_RANK),
        'mla_kv_norm': gain((n_c, MLA_KV_RANK)),
        'mla_w_kvb': w((n_c, MLA_KV_RANK, MLA_HEADS * (MLA_NOPE + MLA_V)), MLA_KV_RANK),
        'mla_w_o': w((n_c, MLA_HEADS * MLA_V, D_MODEL), MLA_HEADS * MLA_V),
        'ffn_w_up': w((DEPTH, D_MODEL, 2 * D_FF), D_MODEL),
        'ffn_conv_w': w((DEPTH, CONV_W, 2 * D_FF), CONV_W),
        'ffn_conv_b': 0.02 * jax.random.normal(next(ks), (DEPTH, 2 * D_FF), jnp.float32),
        'ffn_w_down': w((DEPTH, D_FF, D_MODEL), D_FF),
    }


def reference(x, positions, rel_bias, norm_gains, sb_w_qkv, sb_w_o, dil_w_qkv, dil_w_o,
              mla_w_in, mla_q_norm, mla_w_qb, mla_kv_norm, mla_w_kvb, mla_w_o,
              ffn_w_up, ffn_conv_w, ffn_conv_b, ffn_w_down):
    for i in range(DEPTH):
        kind, j = i % N_MIXERS, i // N_MIXERS
        hn = rms_norm(x, norm_gains[i, 0])
        if kind == 0:
            m = stick_breaking_attention(hn, sb_w_qkv[j], sb_w_o[j])
        elif kind == 1:
            m = dilated_attention(hn, dil_w_qkv[j], dil_w_o[j], rel_bias)
        else:
            m = latent_attention(hn, positions, mla_w_in[j], mla_q_norm[j], mla_w_qb[j],
                                 mla_kv_norm[j], mla_w_kvb[j], mla_w_o[j])
        x = x + rms_norm(m, norm_gains[i, 1])
        f = conv_ffn(rms_norm(x, norm_gains[i, 2]), ffn_w_up[i], ffn_conv_w[i],
                     ffn_conv_b[i], ffn_w_down[i])
        x = x + rms_norm(f, norm_gains[i, 3])
    return x
```

```python
import functools
import math

import jax
import jax.numpy as jnp
from jax import lax
from jax.experimental import pallas as pl
from jax.experimental.pallas import tpu as pltpu

F32 = jnp.float32
BF16 = jnp.bfloat16

D_MODEL = 1024
DEPTH = 4
N_MIXERS = 3
EPS = 1e-6
NEG = -1e30

SB_HEADS = 16
HEAD_DIM = 64

DIL_GROUPS = ((128, 1), (512, 4), (2048, 16))
DIL_HEADS = 8
N_BUCKETS = 32
BUCKET_MAX_DIST = 2048

MLA_HEADS = 16
MLA_Q_RANK = 384
MLA_KV_RANK = 256
MLA_NOPE = 64
MLA_ROPE = 32
MLA_V = 64
ROPE_THETA = 10000.0

D_FF = 2816

LANES = 128
V7X_VMEM_CAP_BYTES = 56 * 1024 * 1024

ROW_TILE = 512
ATT_TILE = 256
DIL_TILE = 128
FF_CHUNK = 256


def _vmem_limit(pipelined_bytes, resident_bytes):
    return int(min(V7X_VMEM_CAP_BYTES, 2 * pipelined_bytes + resident_bytes))


def _nbytes(shape, dtype):
    return math.prod(shape) * jnp.dtype(dtype).itemsize


def _rms_scale(x):
    return x * lax.rsqrt(jnp.mean(x * x, axis=-1, keepdims=True) + EPS)


def _dot(a, b):
    return jnp.dot(a, b, preferred_element_type=F32)


def _dot_nt(a, b):
    return lax.dot_general(a, b, (((1,), (1,)), ((), ())), preferred_element_type=F32)


def _norm_proj_kernel(x_ref, g_ref, w_ref, o_ref, *, n_chunk):
    xn = (_rms_scale(x_ref[...]) * g_ref[...]).astype(BF16)
    for c in range(o_ref.shape[1] // n_chunk):
        sl = slice(c * n_chunk, (c + 1) * n_chunk)
        o_ref[:, sl] = _dot(xn, w_ref[:, sl]).astype(o_ref.dtype)


def _norm_proj(x, g, w, name):
    t, d = x.shape
    n = w.shape[1]
    n_chunk = 512
    assert t % ROW_TILE == 0 and n % n_chunk == 0
    limit = _vmem_limit(
        _nbytes((ROW_TILE, d), F32) + _nbytes((d, n), BF16) + _nbytes((ROW_TILE, n), BF16),
        _nbytes((ROW_TILE, d), F32) + 2 * _nbytes((ROW_TILE, n_chunk), F32))
    return pl.pallas_call(
        functools.partial(_norm_proj_kernel, n_chunk=n_chunk),
        out_shape=jax.ShapeDtypeStruct((t, n), BF16),
        grid=(t // ROW_TILE,),
        in_specs=[pl.BlockSpec((ROW_TILE, d), lambda i: (i, 0)),
                  pl.BlockSpec((1, d), lambda i: (0, 0)),
                  pl.BlockSpec((d, n), lambda i: (0, 0))],
        out_specs=pl.BlockSpec((ROW_TILE, n), lambda i: (i, 0)),
        compiler_params=pltpu.CompilerParams(dimension_semantics=("parallel",),
                                             vmem_limit_bytes=limit),
        name=name,
    )(x, g, w)


def _proj_post_kernel(a_ref, w_ref, x_ref, g_ref, o_ref):
    m = _dot(a_ref[...], w_ref[...])
    o_ref[...] = x_ref[...] + _rms_scale(m) * g_ref[...]


def _proj_post(a, w, x, g, name):
    t, k = a.shape
    d = w.shape[1]
    limit = _vmem_limit(
        _nbytes((ROW_TILE, k), BF16) + _nbytes((k, d), BF16) + 2 * _nbytes((ROW_TILE, d), F32),
        2 * _nbytes((ROW_TILE, d), F32))
    return pl.pallas_call(
        _proj_post_kernel,
        out_shape=jax.ShapeDtypeStruct((t, d), F32),
        grid=(t // ROW_TILE,),
        in_specs=[pl.BlockSpec((ROW_TILE, k), lambda i: (i, 0)),
                  pl.BlockSpec((k, d), lambda i: (0, 0)),
                  pl.BlockSpec((ROW_TILE, d), lambda i: (i, 0)),
                  pl.BlockSpec((1, d), lambda i: (0, 0))],
        out_specs=pl.BlockSpec((ROW_TILE, d), lambda i: (i, 0)),
        compiler_params=pltpu.CompilerParams(dimension_semantics=("parallel",),
                                             vmem_limit_bytes=limit),
        name=name,
    )(a, w, x, g)


def _sb_kernel(q_ref, k_ref, v_ref, o_ref):
    tq = q_ref.shape[1]
    qi = pl.program_id(2)
    lane = lax.broadcasted_iota(jnp.int32, (tq, LANES), 1)
    first = lane < HEAD_DIM
    q2 = q_ref[0] * jnp.asarray(HEAD_DIM ** -0.5, BF16)
    zero = jnp.zeros_like(q2)
    q_heads = (jnp.where(first, q2, zero), jnp.where(first, zero, q2))

    row = lax.broadcasted_iota(jnp.int32, (tq, tq), 0)
    col = lax.broadcasted_iota(jnp.int32, (tq, tq), 1)
    later = jnp.where(row > col, 1.0, 0.0).astype(BF16)
    causal = col < row

    def block(qh, kblk, vblk, carry, acc, mask):
        z = _dot_nt(qh, kblk)
        softplus = jnp.maximum(z, 0.0) + jnp.log1p(jnp.exp(-jnp.abs(z)))
        log_not = -softplus
        log_beta = z - softplus
        if mask is not None:
            log_not = jnp.where(mask, log_not, 0.0)
        hi = log_not.astype(BF16)
        lo = (log_not - hi.astype(F32)).astype(BF16)
        between = _dot(hi, later) + _dot(lo, later) + carry
        a = jnp.exp(log_beta + between)
        if mask is not None:
            a = jnp.where(mask, a, 0.0)
        acc = acc + _dot(a.astype(BF16), vblk)
        carry = carry + jnp.sum(log_not, axis=-1, keepdims=True)
        return carry, acc

    def kv_block(kb):
        start = pl.multiple_of(kb * tq, tq)
        return k_ref[0, pl.ds(start, tq), :], v_ref[0, pl.ds(start, tq), :]

    kd, vd = kv_block(qi)
    state = []
    for qh in q_heads:
        state.extend(block(qh, kd, vd, jnp.zeros((tq, 1), F32),
                           jnp.zeros((tq, LANES), F32), causal))

    def body(n, st):
        kblk, vblk = kv_block(qi - 1 - n)
        out = []
        for h, qh in enumerate(q_heads):
            out.extend(block(qh, kblk, vblk, st[2 * h], st[2 * h + 1], None))
        return tuple(out)

    st = lax.fori_loop(0, qi, body, tuple(state))
    o_ref[0] = jnp.where(first, st[1], st[3]).astype(o_ref.dtype)


def _sb_attention(qkv, batch, seq):
    pairs = SB_HEADS * HEAD_DIM // LANES
    tq = ATT_TILE
    limit = _vmem_limit(
        2 * _nbytes((tq, LANES), BF16) + 2 * _nbytes((seq, LANES), BF16),
        24 * _nbytes((tq, tq), F32))
    return pl.pallas_call(
        _sb_kernel,
        out_shape=jax.ShapeDtypeStruct((batch, seq, SB_HEADS * HEAD_DIM), BF16),
        grid=(batch, pairs, seq // tq),
        in_specs=[pl.BlockSpec((1, tq, LANES), lambda b, p, i: (b, i, p)),
                  pl.BlockSpec((1, seq, LANES), lambda b, p, i: (b, 0, pairs + p)),
                  pl.BlockSpec((1, seq, LANES), lambda b, p, i: (b, 0, 2 * pairs + p))],
        out_specs=pl.BlockSpec((1, tq, LANES), lambda b, p, i: (b, i, p)),
        compiler_params=pltpu.CompilerParams(
            dimension_semantics=("parallel", "parallel", "parallel"), vmem_limit_bytes=limit),
        name="sb_attention",
    )(qkv, qkv, qkv)


def _dil_kernel(q_ref, kp_ref, kc_ref, vp_ref, vc_ref, bias_ref, o_ref, lse_ref):
    w = q_ref.shape[1]
    ut = pl.program_id(2)
    lane = lax.broadcasted_iota(jnp.int32, (w, LANES), 1)
    first = lane < HEAD_DIM
    col = lax.broadcasted_iota(jnp.int32, (w, 2 * w), 1)
    key_ok = jnp.logical_or(col >= w, ut > 0)
    for p in range(q_ref.shape[2] // LANES):
        sl = slice(p * LANES, (p + 1) * LANES)
        q2 = q_ref[0, :, sl] * jnp.asarray(HEAD_DIM ** -0.5, BF16)
        zero = jnp.zeros_like(q2)
        k2 = jnp.concatenate([kp_ref[0, :, sl], kc_ref[0, :, sl]], axis=0)
        v2 = jnp.concatenate([vp_ref[0, :, sl], vc_ref[0, :, sl]], axis=0)
        outs, lses = [], []
        for h, qh in enumerate((jnp.where(first, q2, zero), jnp.where(first, zero, q2))):
            s = _dot_nt(qh, k2) + bias_ref[2 * p + h]
            s = jnp.where(key_ok, s, NEG)
            m = jnp.max(s, axis=-1, keepdims=True)
            e = jnp.exp(s - m)
            den = jnp.sum(e, axis=-1, keepdims=True)
            outs.append(_dot(e.astype(BF16), v2) / den)
            lses.append(m + jnp.log(den))
        o_ref[0, :, sl] = jnp.where(first, outs[0], outs[1]).astype(o_ref.dtype)
        lse_ref[0, :, sl] = jnp.where(first, lses[0], lses[1])


def _dil_bias(rel_bias, group, dilation, n_back):
    w = n_back
    m = jnp.arange(w)[:, None] + w - jnp.arange(2 * w)[None, :]
    dist = jnp.maximum(m * dilation, 0)
    max_exact = N_BUCKETS // 2
    d = jnp.maximum(dist.astype(F32), 1.0)
    large = max_exact + (jnp.log(d / max_exact) / math.log(BUCKET_MAX_DIST / max_exact)
                         * (N_BUCKETS - max_exact)).astype(jnp.int32)
    bucket = jnp.where(dist < max_exact, dist, jnp.minimum(large, N_BUCKETS - 1))
    bias = rel_bias[:, group * DIL_HEADS:(group + 1) * DIL_HEADS][bucket].astype(F32)
    bias = bias.transpose(2, 0, 1)
    return jnp.where(((m >= 0) & (m <= n_back))[None], bias, NEG)


def _dil_group_attention(qkv, bias, group, dilation, batch, seq):
    n_groups = len(DIL_GROUPS)
    width = DIL_HEADS * HEAD_DIM
    length = seq // dilation
    w = DIL_TILE
    per_token = 3 * n_groups
    qv = qkv.reshape(batch, length, dilation * per_token * width)

    def col(which):
        return lambda b, c, u: c * per_token + which * n_groups + group

    def cur(which):
        return pl.BlockSpec((1, w, width), lambda b, c, u: (b, u, col(which)(b, c, u)))

    def prev(which):
        return pl.BlockSpec((1, w, width),
                            lambda b, c, u: (b, jnp.maximum(u - 1, 0), col(which)(b, c, u)))

    out_spec = pl.BlockSpec((1, w, width), lambda b, c, u: (b, u, c))
    limit = _vmem_limit(
        5 * _nbytes((w, width), BF16) + _nbytes((DIL_HEADS, w, 2 * w), F32)
        + _nbytes((w, width), BF16) + _nbytes((w, width), F32),
        32 * _nbytes((w, 2 * w), F32))
    o, lse = pl.pallas_call(
        _dil_kernel,
        out_shape=(jax.ShapeDtypeStruct((batch, length, dilation * width), BF16),
                   jax.ShapeDtypeStruct((batch, length, dilation * width), F32)),
        grid=(batch, dilation, length // w),
        in_specs=[cur(0), prev(1), cur(1), prev(2), cur(2),
                  pl.BlockSpec((DIL_HEADS, w, 2 * w), lambda b, c, u: (0, 0, 0))],
        out_specs=(out_spec, out_spec),
        compiler_params=pltpu.CompilerParams(
            dimension_semantics=("parallel", "parallel", "parallel"), vmem_limit_bytes=limit),
        name=f"dil_attention_g{group}",
    )(qv, qv, qv, qv, qv, bias)
    return o.reshape(batch * seq, width), lse.reshape(batch * seq, width)


def _dil_merge_kernel(o0_ref, o1_ref, o2_ref, l0_ref, l1_ref, l2_ref, w_ref, x_ref, g_ref, out_ref):
    l0, l1, l2 = l0_ref[...], l1_ref[...], l2_ref[...]
    m = jnp.maximum(jnp.maximum(l0, l1), l2)
    e0, e1, e2 = jnp.exp(l0 - m), jnp.exp(l1 - m), jnp.exp(l2 - m)
    inv = 1.0 / (e0 + e1 + e2)
    o = (o0_ref[...].astype(F32) * (e0 * inv) + o1_ref[...].astype(F32) * (e1 * inv)
         + o2_ref[...].astype(F32) * (e2 * inv))
    mo = _dot(o.astype(BF16), w_ref[...])
    out_ref[...] = x_ref[...] + _rms_scale(mo) * g_ref[...]


def _dil_merge_proj(outs, lses, w, x, g):
    t, k = outs[0].shape
    d = w.shape[1]
    row = lambda i: (i, 0)
    limit = _vmem_limit(
        3 * _nbytes((ROW_TILE, k), BF16) + 3 * _nbytes((ROW_TILE, k), F32)
        + _nbytes((k, d), BF16) + 2 * _nbytes((ROW_TILE, d), F32),
        8 * _nbytes((ROW_TILE, k), F32) + 2 * _nbytes((ROW_TILE, d), F32))
    return pl.pallas_call(
        _dil_merge_kernel,
        out_shape=jax.ShapeDtypeStruct((t, d), F32),
        grid=(t // ROW_TILE,),
        in_specs=[pl.BlockSpec((ROW_TILE, k), row)] * 6
        + [pl.BlockSpec((k, d), lambda i: (0, 0)),
           pl.BlockSpec((ROW_TILE, d), row),
           pl.BlockSpec((1, d), lambda i: (0, 0))],
        out_specs=pl.BlockSpec((ROW_TILE, d), row),
        compiler_params=pltpu.CompilerParams(dimension_semantics=("parallel",),
                                             vmem_limit_bytes=limit),
        name="dil_merge_proj",
    )(*outs, *lses, w, x, g)


def _mla_pre_kernel(x_ref, g_ref, wcq_ref, wckv_ref, wkr_ref, wkr_rot_ref, qn_ref, kvn_ref,
                    wq_ref, wq_rot_ref, wk_ref, wv_ref, cos_ref, sin_ref,
                    q_ref, k_ref, v_ref, *, q_scale):
    xn = (_rms_scale(x_ref[...]) * g_ref[...]).astype(BF16)
    cq = (_rms_scale(_dot(xn, wcq_ref[...])) * qn_ref[...]).astype(BF16)
    ckv = (_rms_scale(_dot(xn, wckv_ref[...])) * kvn_ref[...]).astype(BF16)
    cos, sin = cos_ref[...], sin_ref[...]
    k_rope = _dot(xn, wkr_ref[...]) * cos + _dot(xn, wkr_rot_ref[...]) * sin
    for h in range(MLA_HEADS):
        sl = slice(h * LANES, (h + 1) * LANES)
        q = _dot(cq, wq_ref[:, sl]) * cos + _dot(cq, wq_rot_ref[:, sl]) * sin
        q_ref[:, sl] = (q * q_scale).astype(q_ref.dtype)
        k_ref[:, sl] = (_dot(ckv, wk_ref[:, sl]) + k_rope).astype(k_ref.dtype)
    v_ref[...] = _dot(ckv, wv_ref[...]).astype(v_ref.dtype)


def _rope_rotation(w):
    half = MLA_ROPE // 2
    return jnp.concatenate([-w[..., half:], w[..., :half]], axis=-1)


def _pad_head_slabs(nope, rope):
    k, h = nope.shape[0], nope.shape[1]
    pad = jnp.zeros((k, h, LANES - MLA_NOPE - MLA_ROPE), nope.dtype)
    return jnp.concatenate([nope, rope, pad], axis=-1).reshape(k, h * LANES)


def _mla_pre(x, g, positions, w_in, q_norm, w_qb, kv_norm, w_kvb):
    t, d = x.shape
    h = MLA_HEADS
    w_cq = w_in[:, :MLA_Q_RANK].astype(BF16)
    w_ckv = w_in[:, MLA_Q_RANK:MLA_Q_RANK + MLA_KV_RANK].astype(BF16)
    w_kr = w_in[:, MLA_Q_RANK + MLA_KV_RANK:]
    zeros_nope = jnp.zeros((d, 1, MLA_NOPE), F32)
    w_kr_pad = _pad_head_slabs(zeros_nope, w_kr[:, None, :]).astype(BF16)
    w_kr_rot = _pad_head_slabs(zeros_nope, _rope_rotation(w_kr)[:, None, :]).astype(BF16)

    wq = w_qb.reshape(MLA_Q_RANK, h, MLA_NOPE + MLA_ROPE)
    wq_pad = _pad_head_slabs(wq[..., :MLA_NOPE], wq[..., MLA_NOPE:]).astype(BF16)
    wq_rot = _pad_head_slabs(jnp.zeros_like(wq[..., :MLA_NOPE]),
                             _rope_rotation(wq[..., MLA_NOPE:])).astype(BF16)
    wkv = w_kvb.reshape(MLA_KV_RANK, h, MLA_NOPE + MLA_V)
    wk_pad = _pad_head_slabs(wkv[..., :MLA_NOPE],
                             jnp.zeros((MLA_KV_RANK, h, MLA_ROPE), F32)).astype(BF16)
    wv = wkv[..., MLA_NOPE:].reshape(MLA_KV_RANK, h * MLA_V).astype(BF16)

    half = MLA_ROPE // 2
    freqs = ROPE_THETA ** (-jnp.arange(half, dtype=F32) / half)
    ang = positions.astype(F32).reshape(t, 1) * freqs
    cos, sin = jnp.cos(ang), jnp.sin(ang)
    tail = jnp.zeros((t, LANES - MLA_NOPE - MLA_ROPE), F32)
    cos_tab = jnp.concatenate([jnp.ones((t, MLA_NOPE), F32), cos, cos, tail], axis=-1)
    sin_tab = jnp.concatenate([jnp.zeros((t, MLA_NOPE), F32), sin, sin, tail], axis=-1)

    tm = ROW_TILE // 2
    full = lambda a: pl.BlockSpec(a.shape, lambda i: (0,) * a.ndim)
    row = lambda n: pl.BlockSpec((tm, n), lambda i: (i, 0))
    weights = (w_cq, w_ckv, w_kr_pad, w_kr_rot, q_norm.reshape(1, -1), kv_norm.reshape(1, -1),
               wq_pad, wq_rot, wk_pad, wv)
    limit = _vmem_limit(
        _nbytes((tm, d), F32) + sum(_nbytes(a.shape, a.dtype) for a in weights)
        + 2 * _nbytes((tm, LANES), F32) + 2 * _nbytes((tm, h * LANES), BF16)
        + _nbytes((tm, h * MLA_V), BF16),
        4 * _nbytes((tm, d), F32))
    return pl.pallas_call(
        functools.partial(_mla_pre_kernel, q_scale=(MLA_NOPE + MLA_ROPE) ** -0.5),
        out_shape=(jax.ShapeDtypeStruct((t, h * LANES), BF16),
                   jax.ShapeDtypeStruct((t, h * LANES), BF16),
                   jax.ShapeDtypeStruct((t, h * MLA_V), BF16)),
        grid=(t // tm,),
        in_specs=[row(d), full(g)] + [full(a) for a in weights] + [row(LANES), row(LANES)],
        out_specs=(row(h * LANES), row(h * LANES), row(h * MLA_V)),
        compiler_params=pltpu.CompilerParams(dimension_semantics=("parallel",),
                                             vmem_limit_bytes=limit),
        name="mla_pre",
    )(x, g, *weights, cos_tab, sin_tab)


def _mla_kernel(q_ref, k_ref, v_ref, o_ref):
    tq = q_ref.shape[1]
    qi = pl.program_id(2)
    lane = lax.broadcasted_iota(jnp.int32, (tq, LANES), 1)
    first = lane < MLA_V
    row = lax.broadcasted_iota(jnp.int32, (tq, tq), 0)
    col = lax.broadcasted_iota(jnp.int32, (tq, tq), 1)
    causal = col <= row
    q_heads = (q_ref[0, :, :LANES], q_ref[0, :, LANES:])

    def block(qh, kblk, vblk, m, den, acc, mask):
        s = _dot_nt(qh, kblk)
        if mask is not None:
            s = jnp.where(mask, s, NEG)
        m_new = jnp.maximum(m, jnp.max(s, axis=-1, keepdims=True))
        alpha = jnp.exp(m - m_new)
        e = jnp.exp(s - m_new)
        den = alpha * den + jnp.sum(e, axis=-1, keepdims=True)
        acc = alpha * acc + _dot(e.astype(BF16), vblk)
        return m_new, den, acc

    def kv_block(kb):
        start = pl.multiple_of(kb * tq, tq)
        return k_ref[0, pl.ds(start, tq), :], v_ref[0, pl.ds(start, tq), :]

    kd, vd = kv_block(qi)
    state = []
    for h, qh in enumerate(q_heads):
        state.extend(block(qh, kd[:, h * LANES:(h + 1) * LANES], vd,
                           jnp.full((tq, 1), NEG, F32), jnp.zeros((tq, 1), F32),
                           jnp.zeros((tq, LANES), F32), causal))

    def body(n, st):
        kblk, vblk = kv_block(qi - 1 - n)
        out = []
        for h, qh in enumerate(q_heads):
            out.extend(block(qh, kblk[:, h * LANES:(h + 1) * LANES], vblk,
                             st[3 * h], st[3 * h + 1], st[3 * h + 2], None))
        return tuple(out)

    st = lax.fori_loop(0, qi, body, tuple(state))
    o_ref[0] = jnp.where(first, st[2] / st[1], st[5] / st[4]).astype(o_ref.dtype)


def _mla_attention(q, k, v, batch, seq):
    pairs = MLA_HEADS * MLA_V // LANES
    tq = ATT_TILE
    limit = _vmem_limit(
        _nbytes((tq, 2 * LANES), BF16) + _nbytes((seq, 2 * LANES), BF16)
        + _nbytes((seq, LANES), BF16) + _nbytes((tq, LANES), BF16),
        16 * _nbytes((tq, tq), F32))
    return pl.pallas_call(
        _mla_kernel,
        out_shape=jax.ShapeDtypeStruct((batch, seq, MLA_HEADS * MLA_V), BF16),
        grid=(batch, pairs, seq // tq),
        in_specs=[pl.BlockSpec((1, tq, 2 * LANES), lambda b, p, i: (b, i, p)),
                  pl.BlockSpec((1, seq, 2 * LANES), lambda b, p, i: (b, 0, p)),
                  pl.BlockSpec((1, seq, LANES), lambda b, p, i: (b, 0, p))],
        out_specs=pl.BlockSpec((1, tq, LANES), lambda b, p, i: (b, i, p)),
        compiler_params=pltpu.CompilerParams(
            dimension_semantics=("parallel", "parallel", "parallel"), vmem_limit_bytes=limit),
        name="mla_attention",
    )(q, k, v)


def _ffn_up_kernel(x_ref, g_ref, wg_ref, wu_ref, cwg_ref, cwu_ref, cbg_ref, cbu_ref, o_ref,
                   tail_g_ref, tail_u_ref, *, tiles_per_seq):
    tm = x_ref.shape[0]

    @pl.when(pl.program_id(0) % tiles_per_seq == 0)
    def _():
        tail_g_ref[...] = jnp.zeros_like(tail_g_ref)
        tail_u_ref[...] = jnp.zeros_like(tail_u_ref)

    xn = (_rms_scale(x_ref[...]) * g_ref[...]).astype(BF16)
    row = lax.broadcasted_iota(jnp.int32, (tm, FF_CHUNK), 0)

    def conv(h, tail_ref, cw_ref, cb_ref, sl):
        p1 = tail_ref[7:8, sl]
        p2 = tail_ref[6:7, sl]
        h1 = jnp.where(row == 0, p1, pltpu.roll(h, 1, 0))
        h2 = jnp.where(row == 0, p2, jnp.where(row == 1, p1, pltpu.roll(h, 2, 0)))
        tail_ref[:, sl] = h[tm - 8:, :]
        return cw_ref[0:1, sl] * h2 + cw_ref[1:2, sl] * h1 + cw_ref[2:3, sl] * h + cb_ref[:, sl]

    for c in range(o_ref.shape[1] // FF_CHUNK):
        sl = slice(c * FF_CHUNK, (c + 1) * FF_CHUNK)
        gate = conv(_dot(xn, wg_ref[:, sl]), tail_g_ref, cwg_ref, cbg_ref, sl)
        val = conv(_dot(xn, wu_ref[:, sl]), tail_u_ref, cwu_ref, cbu_ref, sl)
        act = gate * (1.0 / (1.0 + jnp.exp(-gate))) * val
        o_ref[:, sl] = act.astype(o_ref.dtype)


def _ffn_up(x, g, w_up, conv_w, conv_b, seq):
    t, d = x.shape
    wg, wu = w_up[:, :D_FF].astype(BF16), w_up[:, D_FF:].astype(BF16)
    cwg, cwu = conv_w[:, :D_FF], conv_w[:, D_FF:]
    cbg, cbu = conv_b[:D_FF].reshape(1, D_FF), conv_b[D_FF:].reshape(1, D_FF)
    assert seq % ROW_TILE == 0 and D_FF % FF_CHUNK == 0
    full = lambda a: pl.BlockSpec(a.shape, lambda i: (0,) * a.ndim)
    limit = _vmem_limit(
        _nbytes((ROW_TILE, d), F32) + 2 * _nbytes((d, D_FF), BF16)
        + _nbytes((ROW_TILE, D_FF), BF16) + 8 * _nbytes((1, D_FF), F32),
        2 * _nbytes((8, D_FF), F32) + _nbytes((ROW_TILE, d), F32)
        + 12 * _nbytes((ROW_TILE, FF_CHUNK), F32))
    return pl.pallas_call(
        functools.partial(_ffn_up_kernel, tiles_per_seq=seq // ROW_TILE),
        out_shape=jax.ShapeDtypeStruct((t, D_FF), BF16),
        grid=(t // ROW_TILE,),
        in_specs=[pl.BlockSpec((ROW_TILE, d), lambda i: (i, 0)), full(g), full(wg), full(wu),
                  full(cwg), full(cwu), full(cbg), full(cbu)],
        out_specs=pl.BlockSpec((ROW_TILE, D_FF), lambda i: (i, 0)),
        scratch_shapes=[pltpu.VMEM((8, D_FF), F32), pltpu.VMEM((8, D_FF), F32)],
        compiler_params=pltpu.CompilerParams(dimension_semantics=("arbitrary",),
                                             vmem_limit_bytes=limit),
        name="ffn_up_conv_gate",
    )(x, g, wg, wu, cwg, cwu, cbg, cbu)


def kernel(x, positions, rel_bias, norm_gains, sb_w_qkv, sb_w_o, dil_w_qkv, dil_w_o, mla_w_in,
           mla_q_norm, mla_w_qb, mla_kv_norm, mla_w_kvb, mla_w_o, ffn_w_up, ffn_conv_w,
           ffn_conv_b, ffn_w_down):
    batch, seq, d = x.shape
    t = batch * seq
    x = x.reshape(t, d)
    for i in range(DEPTH):
        kind, j = i % N_MIXERS, i // N_MIXERS
        gain = lambda n: norm_gains[i, n].reshape(1, d)
        if kind == 0:
            qkv = _norm_proj(x, gain(0), sb_w_qkv[j].astype(BF16), "sb_qkv_proj")
            o = _sb_attention(qkv.reshape(batch, seq, -1), batch, seq).reshape(t, -1)
            x = _proj_post(o, sb_w_o[j].astype(BF16), x, gain(1), "sb_out_proj")
        elif kind == 1:
            qkv = _norm_proj(x, gain(0), dil_w_qkv[j].astype(BF16), "dil_qkv_proj")
            qkv = qkv.reshape(batch, seq, -1)
            outs, lses = [], []
            for grp, (window, dilation) in enumerate(DIL_GROUPS):
                n_back = window // dilation
                assert n_back == DIL_TILE and (seq // dilation) % DIL_TILE == 0
                bias = _dil_bias(rel_bias, grp, dilation, n_back)
                o, lse = _dil_group_attention(qkv, bias, grp, dilation, batch, seq)
                outs.append(o)
                lses.append(lse)
            x = _dil_merge_proj(outs, lses, dil_w_o[j].astype(BF16), x, gain(1))
        else:
            q, k, v = _mla_pre(x, gain(0), positions, mla_w_in[j], mla_q_norm[j], mla_w_qb[j],
                               mla_kv_norm[j], mla_w_kvb[j])
            o = _mla_attention(q.reshape(batch, seq, -1), k.reshape(batch, seq, -1),
                               v.reshape(batch, seq, -1), batch, seq).reshape(t, -1)
            x = _proj_post(o, mla_w_o[j].astype(BF16), x, gain(1), "mla_out_proj")
        act = _ffn_up(x, gain(2), ffn_w_up[i], ffn_conv_w[i], ffn_conv_b[i], seq)
        x = _proj_post(act, ffn_w_down[i].astype(BF16), x, gain(3), "ffn_down_proj")
    return x.reshape(batch, seq, d)
```

```python
import functools
import math

import jax
import jax.numpy as jnp
from jax import lax
from jax.experimental import pallas as pl
from jax.experimental.pallas import tpu as pltpu

F32 = jnp.float32
BF16 = jnp.bfloat16

D_MODEL = 1024
DEPTH = 4
N_MIXERS = 3
EPS = 1e-6
NEG = -1e30

SB_HEADS = 16
HEAD_DIM = 64

DIL_GROUPS = ((128, 1), (512, 4), (2048, 16))
DIL_HEADS = 8
N_BUCKETS = 32
BUCKET_MAX_DIST = 2048

MLA_HEADS = 16
MLA_Q_RANK = 384
MLA_KV_RANK = 256
MLA_NOPE = 64
MLA_ROPE = 32
MLA_V = 64
ROPE_THETA = 10000.0

D_FF = 2816

LANES = 128
V7X_VMEM_CAP_BYTES = 56 * 1024 * 1024

ROW_TILE = 512
ATT_TILE = 256
DIL_TILE = 128
FF_CHUNK = 256


def _vmem_limit(pipelined_bytes, resident_bytes):
    return int(min(V7X_VMEM_CAP_BYTES, 2 * pipelined_bytes + resident_bytes))


def _nbytes(shape, dtype):
    return math.prod(shape) * jnp.dtype(dtype).itemsize


def _rms_scale(x):
    return x * lax.rsqrt(jnp.mean(x * x, axis=-1, keepdims=True) + EPS)


def _dot(a, b):
    return jnp.dot(a, b, preferred_element_type=F32)


def _neg_abs(x):
    bits = lax.bitcast_convert_type(x, jnp.uint32) | jnp.uint32(0x80000000)
    return lax.bitcast_convert_type(bits, F32)


def _dot_nt(a, b):
    return lax.dot_general(a, b, (((1,), (1,)), ((), ())), preferred_element_type=F32)


def _norm_proj_kernel(x_ref, g_ref, w_ref, o_ref, *, n_chunk):
    xn = (_rms_scale(x_ref[...]) * g_ref[...]).astype(BF16)
    for c in range(o_ref.shape[1] // n_chunk):
        sl = slice(c * n_chunk, (c + 1) * n_chunk)
        o_ref[:, sl] = _dot(xn, w_ref[:, sl]).astype(o_ref.dtype)


def _norm_proj(x, g, w, name):
    t, d = x.shape
    n = w.shape[1]
    n_chunk = 512
    assert t % ROW_TILE == 0 and n % n_chunk == 0
    limit = _vmem_limit(
        _nbytes((ROW_TILE, d), F32) + _nbytes((d, n), BF16) + _nbytes((ROW_TILE, n), BF16),
        _nbytes((ROW_TILE, d), F32) + 2 * _nbytes((ROW_TILE, n_chunk), F32))
    return pl.pallas_call(
        functools.partial(_norm_proj_kernel, n_chunk=n_chunk),
        out_shape=jax.ShapeDtypeStruct((t, n), BF16),
        grid=(t // ROW_TILE,),
        in_specs=[pl.BlockSpec((ROW_TILE, d), lambda i: (i, 0)),
                  pl.BlockSpec((1, d), lambda i: (0, 0)),
                  pl.BlockSpec((d, n), lambda i: (0, 0))],
        out_specs=pl.BlockSpec((ROW_TILE, n), lambda i: (i, 0)),
        compiler_params=pltpu.CompilerParams(dimension_semantics=("parallel",),
                                             vmem_limit_bytes=limit),
        name=name,
    )(x, g, w)


def _proj_post_kernel(a_ref, w_ref, x_ref, g_ref, o_ref):
    m = _dot(a_ref[...], w_ref[...])
    o_ref[...] = x_ref[...] + _rms_scale(m) * g_ref[...]


def _proj_post(a, w, x, g, name):
    t, k = a.shape
    d = w.shape[1]
    limit = _vmem_limit(
        _nbytes((ROW_TILE, k), BF16) + _nbytes((k, d), BF16) + 2 * _nbytes((ROW_TILE, d), F32),
        2 * _nbytes((ROW_TILE, d), F32))
    return pl.pallas_call(
        _proj_post_kernel,
        out_shape=jax.ShapeDtypeStruct((t, d), F32),
        grid=(t // ROW_TILE,),
        in_specs=[pl.BlockSpec((ROW_TILE, k), lambda i: (i, 0)),
                  pl.BlockSpec((k, d), lambda i: (0, 0)),
                  pl.BlockSpec((ROW_TILE, d), lambda i: (i, 0)),
                  pl.BlockSpec((1, d), lambda i: (0, 0))],
        out_specs=pl.BlockSpec((ROW_TILE, d), lambda i: (i, 0)),
        compiler_params=pltpu.CompilerParams(dimension_semantics=("parallel",),
                                             vmem_limit_bytes=limit),
        name=name,
    )(a, w, x, g)


def _sb_kernel(q_ref, k_ref, v_ref, o_ref):
    tq = q_ref.shape[1]
    n_pairs = q_ref.shape[2] // LANES
    qi = pl.program_id(2)
    lane = lax.broadcasted_iota(jnp.int32, (tq, LANES), 1)
    first = lane < HEAD_DIM
    q_heads = []
    for p in range(n_pairs):
        q2 = q_ref[0, :, p * LANES:(p + 1) * LANES] * jnp.asarray(HEAD_DIM ** -0.5, BF16)
        zero = jnp.zeros_like(q2)
        q_heads.append(jnp.where(first, q2, zero))
        q_heads.append(jnp.where(first, zero, q2))

    row = lax.broadcasted_iota(jnp.int32, (tq, tq), 0)
    col = lax.broadcasted_iota(jnp.int32, (tq, tq), 1)
    later = jnp.where(row > col, 1.0, 0.0).astype(BF16)
    later2 = jnp.concatenate([later, later], axis=0)
    causal = col < row

    def kv_block(kb):
        start = pl.multiple_of(kb * tq, tq)
        return k_ref[0, pl.ds(start, tq), :], v_ref[0, pl.ds(start, tq), :]

    def all_heads(kblk, vblk, st, mask):
        pair = lambda h: slice((h // 2) * LANES, (h // 2 + 1) * LANES)
        heads = range(len(q_heads))
        zs = [_dot_nt(q_heads[h], kblk[:, pair(h)]) for h in heads]
        mids = []
        for h in heads:
            z = zs[h]
            softplus = jnp.maximum(z, 0.0) + jnp.log(1.0 + jnp.exp(_neg_abs(z)))
            base = (z - softplus) - st[2 * h]
            if mask is not None:
                softplus = jnp.where(mask, softplus, 0.0)
            hi = softplus.astype(BF16)
            lo = (softplus - hi.astype(F32)).astype(BF16)
            mids.append((jnp.concatenate([hi, lo], axis=1), base,
                         jnp.sum(softplus, axis=-1, keepdims=True)))
        betweens = [_dot(hilo, later2) for hilo, _, _ in mids]
        weights = []
        for h in heads:
            a = jnp.exp(mids[h][1] - betweens[h])
            if mask is not None:
                a = jnp.where(mask, a, 0.0)
            weights.append(a.astype(BF16))
        out = []
        for h in heads:
            out.append(st[2 * h] + mids[h][2])
            out.append(st[2 * h + 1] + _dot(weights[h], vblk[:, pair(h)]))
        return tuple(out)

    init = (jnp.zeros((tq, 1), F32), jnp.zeros((tq, LANES), F32)) * len(q_heads)
    st = all_heads(*kv_block(qi), init, causal)
    st = lax.fori_loop(0, qi, lambda n, st: all_heads(*kv_block(qi - 1 - n), st, None), st)
    for p in range(n_pairs):
        o_ref[0, :, p * LANES:(p + 1) * LANES] = jnp.where(
            first, st[4 * p + 1], st[4 * p + 3]).astype(o_ref.dtype)


SB_HEADS_PER_STEP = 4


def _sb_attention(qkv, batch, seq):
    width = SB_HEADS_PER_STEP * HEAD_DIM
    pairs = SB_HEADS // SB_HEADS_PER_STEP
    tq = ATT_TILE
    limit = _vmem_limit(
        2 * _nbytes((tq, width), BF16) + 2 * _nbytes((seq, width), BF16),
        12 * SB_HEADS_PER_STEP * _nbytes((tq, tq), F32))
    return pl.pallas_call(
        _sb_kernel,
        out_shape=jax.ShapeDtypeStruct((batch, seq, SB_HEADS * HEAD_DIM), BF16),
        grid=(batch, pairs, seq // tq),
        in_specs=[pl.BlockSpec((1, tq, width), lambda b, p, i: (b, i, p)),
                  pl.BlockSpec((1, seq, width), lambda b, p, i: (b, 0, pairs + p)),
                  pl.BlockSpec((1, seq, width), lambda b, p, i: (b, 0, 2 * pairs + p))],
        out_specs=pl.BlockSpec((1, tq, width), lambda b, p, i: (b, i, p)),
        compiler_params=pltpu.CompilerParams(
            dimension_semantics=("parallel", "parallel", "parallel"), vmem_limit_bytes=limit),
        name="sb_attention",
    )(qkv, qkv, qkv)


def _dil_kernel(q_ref, kp_ref, kc_ref, vp_ref, vc_ref, bias_ref, o_ref, lse_ref):
    w = q_ref.shape[1]
    ut = pl.program_id(2)
    lane = lax.broadcasted_iota(jnp.int32, (w, LANES), 1)
    first = lane < HEAD_DIM
    col = lax.broadcasted_iota(jnp.int32, (w, 2 * w), 1)
    key_ok = jnp.logical_or(col >= w, ut > 0)
    for p in range(q_ref.shape[2] // LANES):
        sl = slice(p * LANES, (p + 1) * LANES)
        q2 = q_ref[0, :, sl] * jnp.asarray(HEAD_DIM ** -0.5, BF16)
        zero = jnp.zeros_like(q2)
        k2 = jnp.concatenate([kp_ref[0, :, sl], kc_ref[0, :, sl]], axis=0)
        v2 = jnp.concatenate([vp_ref[0, :, sl], vc_ref[0, :, sl]], axis=0)
        outs, lses = [], []
        for h, qh in enumerate((jnp.where(first, q2, zero), jnp.where(first, zero, q2))):
            s = _dot_nt(qh, k2) + bias_ref[2 * p + h]
            s = jnp.where(key_ok, s, NEG)
            m = jnp.max(s, axis=-1, keepdims=True)
            e = jnp.exp(s - m)
            den = jnp.sum(e, axis=-1, keepdims=True)
            outs.append(_dot(e.astype(BF16), v2) / den)
            lses.append(m + jnp.log(den))
        o_ref[0, :, sl] = jnp.where(first, outs[0], outs[1]).astype(o_ref.dtype)
        lse_ref[0, :, sl] = jnp.where(first, lses[0], lses[1])


def _dil_bias(rel_bias, group, dilation, n_back):
    w = n_back
    dist = jnp.arange(w + 1) * dilation
    max_exact = N_BUCKETS // 2
    d = jnp.maximum(dist.astype(F32), 1.0)
    large = max_exact + (jnp.log(d / max_exact) / math.log(BUCKET_MAX_DIST / max_exact)
                         * (N_BUCKETS - max_exact)).astype(jnp.int32)
    bucket = jnp.where(dist < max_exact, dist, jnp.minimum(large, N_BUCKETS - 1))
    per_m = rel_bias[:, group * DIL_HEADS:(group + 1) * DIL_HEADS][bucket].astype(F32).T
    period = jnp.concatenate(
        [per_m[:, ::-1], jnp.full((DIL_HEADS, w), NEG, F32)], axis=1)
    tiled = jnp.tile(period, (1, w))[:, :w * 2 * w]
    return tiled.reshape(DIL_HEADS, w, 2 * w)


def _dil_group_attention(qkv, bias, group, dilation, batch, seq):
    n_groups = len(DIL_GROUPS)
    width = DIL_HEADS * HEAD_DIM
    length = seq // dilation
    w = DIL_TILE
    per_token = 3 * n_groups
    qv = qkv.reshape(batch, length, dilation * per_token * width)

    def col(which):
        return lambda b, c, u: c * per_token + which * n_groups + group

    def cur(which):
        return pl.BlockSpec((1, w, width), lambda b, c, u: (b, u, col(which)(b, c, u)))

    def prev(which):
        return pl.BlockSpec((1, w, width),
                            lambda b, c, u: (b, jnp.maximum(u - 1, 0), col(which)(b, c, u)))

    out_spec = pl.BlockSpec((1, w, width), lambda b, c, u: (b, u, c))
    limit = _vmem_limit(
        5 * _nbytes((w, width), BF16) + _nbytes((DIL_HEADS, w, 2 * w), F32)
        + _nbytes((w, width), BF16) + _nbytes((w, width), F32),
        32 * _nbytes((w, 2 * w), F32))
    o, lse = pl.pallas_call(
        _dil_kernel,
        out_shape=(jax.ShapeDtypeStruct((batch, length, dilation * width), BF16),
                   jax.ShapeDtypeStruct((batch, length, dilation * width), F32)),
        grid=(batch, dilation, length // w),
        in_specs=[cur(0), prev(1), cur(1), prev(2), cur(2),
                  pl.BlockSpec((DIL_HEADS, w, 2 * w), lambda b, c, u: (0, 0, 0))],
        out_specs=(out_spec, out_spec),
        compiler_params=pltpu.CompilerParams(
            dimension_semantics=("parallel", "parallel", "parallel"), vmem_limit_bytes=limit),
        name=f"dil_attention_g{group}",
    )(qv, qv, qv, qv, qv, bias)
    return o.reshape(batch * seq, width), lse.reshape(batch * seq, width)


def _dil_merge_kernel(o0_ref, o1_ref, o2_ref, l0_ref, l1_ref, l2_ref, w_ref, x_ref, g_ref, out_ref):
    l0, l1, l2 = l0_ref[...], l1_ref[...], l2_ref[...]
    m = jnp.maximum(jnp.maximum(l0, l1), l2)
    e0, e1, e2 = jnp.exp(l0 - m), jnp.exp(l1 - m), jnp.exp(l2 - m)
    inv = 1.0 / (e0 + e1 + e2)
    o = (o0_ref[...].astype(F32) * (e0 * inv) + o1_ref[...].astype(F32) * (e1 * inv)
         + o2_ref[...].astype(F32) * (e2 * inv))
    mo = _dot(o.astype(BF16), w_ref[...])
    out_ref[...] = x_ref[...] + _rms_scale(mo) * g_ref[...]


def _dil_merge_proj(outs, lses, w, x, g):
    t, k = outs[0].shape
    d = w.shape[1]
    row = lambda i: (i, 0)
    limit = _vmem_limit(
        3 * _nbytes((ROW_TILE, k), BF16) + 3 * _nbytes((ROW_TILE, k), F32)
        + _nbytes((k, d), BF16) + 2 * _nbytes((ROW_TILE, d), F32),
        8 * _nbytes((ROW_TILE, k), F32) + 2 * _nbytes((ROW_TILE, d), F32))
    return pl.pallas_call(
        _dil_merge_kernel,
        out_shape=jax.ShapeDtypeStruct((t, d), F32),
        grid=(t // ROW_TILE,),
        in_specs=[pl.BlockSpec((ROW_TILE, k), row)] * 6
        + [pl.BlockSpec((k, d), lambda i: (0, 0)),
           pl.BlockSpec((ROW_TILE, d), row),
           pl.BlockSpec((1, d), lambda i: (0, 0))],
        out_specs=pl.BlockSpec((ROW_TILE, d), row),
        compiler_params=pltpu.CompilerParams(dimension_semantics=("parallel",),
                                             vmem_limit_bytes=limit),
        name="dil_merge_proj",
    )(*outs, *lses, w, x, g)


def _mla_pre_kernel(x_ref, g_ref, wcq_ref, wckv_ref, wkr_ref, wkr_rot_ref, qn_ref, kvn_ref,
                    wq_ref, wq_rot_ref, wk_ref, wv_ref, cos_ref, sin_ref,
                    q_ref, k_ref, v_ref, *, q_scale):
    xn = (_rms_scale(x_ref[...]) * g_ref[...]).astype(BF16)
    cq = (_rms_scale(_dot(xn, wcq_ref[...])) * qn_ref[...]).astype(BF16)
    ckv = (_rms_scale(_dot(xn, wckv_ref[...])) * kvn_ref[...]).astype(BF16)
    cos, sin = cos_ref[...], sin_ref[...]
    k_rope = _dot(xn, wkr_ref[...]) * cos + _dot(xn, wkr_rot_ref[...]) * sin
    for h in range(MLA_HEADS):
        sl = slice(h * LANES, (h + 1) * LANES)
        q = _dot(cq, wq_ref[:, sl]) * cos + _dot(cq, wq_rot_ref[:, sl]) * sin
        q_ref[:, sl] = (q * q_scale).astype(q_ref.dtype)
        k_ref[:, sl] = (_dot(ckv, wk_ref[:, sl]) + k_rope).astype(k_ref.dtype)
    v_ref[...] = _dot(ckv, wv_ref[...]).astype(v_ref.dtype)


def _rope_rotation(w):
    half = MLA_ROPE // 2
    return jnp.concatenate([-w[..., half:], w[..., :half]], axis=-1)


def _pad_head_slabs(nope, rope):
    k, h = nope.shape[0], nope.shape[1]
    pad = jnp.zeros((k, h, LANES - MLA_NOPE - MLA_ROPE), nope.dtype)
    return jnp.concatenate([nope, rope, pad], axis=-1).reshape(k, h * LANES)


def _mla_pre(x, g, positions, w_in, q_norm, w_qb, kv_norm, w_kvb):
    t, d = x.shape
    h = MLA_HEADS
    w_cq = w_in[:, :MLA_Q_RANK].astype(BF16)
    w_ckv = w_in[:, MLA_Q_RANK:MLA_Q_RANK + MLA_KV_RANK].astype(BF16)
    w_kr = w_in[:, MLA_Q_RANK + MLA_KV_RANK:]
    zeros_nope = jnp.zeros((d, 1, MLA_NOPE), F32)
    w_kr_pad = _pad_head_slabs(zeros_nope, w_kr[:, None, :]).astype(BF16)
    w_kr_rot = _pad_head_slabs(zeros_nope, _rope_rotation(w_kr)[:, None, :]).astype(BF16)

    wq = w_qb.reshape(MLA_Q_RANK, h, MLA_NOPE + MLA_ROPE)
    wq_pad = _pad_head_slabs(wq[..., :MLA_NOPE], wq[..., MLA_NOPE:]).astype(BF16)
    wq_rot = _pad_head_slabs(jnp.zeros_like(wq[..., :MLA_NOPE]),
                             _rope_rotation(wq[..., MLA_NOPE:])).astype(BF16)
    wkv = w_kvb.reshape(MLA_KV_RANK, h, MLA_NOPE + MLA_V)
    wk_pad = _pad_head_slabs(wkv[..., :MLA_NOPE],
                             jnp.zeros((MLA_KV_RANK, h, MLA_ROPE), F32)).astype(BF16)
    wv = wkv[..., MLA_NOPE:].reshape(MLA_KV_RANK, h * MLA_V).astype(BF16)

    half = MLA_ROPE // 2
    freqs = ROPE_THETA ** (-jnp.arange(half, dtype=F32) / half)
    ang = positions.astype(F32).reshape(t, 1) * freqs
    cos, sin = jnp.cos(ang), jnp.sin(ang)
    tail = jnp.zeros((t, LANES - MLA_NOPE - MLA_ROPE), F32)
    cos_tab = jnp.concatenate([jnp.ones((t, MLA_NOPE), F32), cos, cos, tail], axis=-1)
    sin_tab = jnp.concatenate([jnp.zeros((t, MLA_NOPE), F32), sin, sin, tail], axis=-1)

    tm = ROW_TILE // 2
    full = lambda a: pl.BlockSpec(a.shape, lambda i: (0,) * a.ndim)
    row = lambda n: pl.BlockSpec((tm, n), lambda i: (i, 0))
    weights = (w_cq, w_ckv, w_kr_pad, w_kr_rot, q_norm.reshape(1, -1), kv_norm.reshape(1, -1),
               wq_pad, wq_rot, wk_pad, wv)
    limit = _vmem_limit(
        _nbytes((tm, d), F32) + sum(_nbytes(a.shape, a.dtype) for a in weights)
        + 2 * _nbytes((tm, LANES), F32) + 2 * _nbytes((tm, h * LANES), BF16)
        + _nbytes((tm, h * MLA_V), BF16),
        4 * _nbytes((tm, d), F32))
    return pl.pallas_call(
        functools.partial(_mla_pre_kernel, q_scale=(MLA_NOPE + MLA_ROPE) ** -0.5),
        out_shape=(jax.ShapeDtypeStruct((t, h * LANES), BF16),
                   jax.ShapeDtypeStruct((t, h * LANES), BF16),
                   jax.ShapeDtypeStruct((t, h * MLA_V), BF16)),
        grid=(t // tm,),
        in_specs=[row(d), full(g)] + [full(a) for a in weights] + [row(LANES), row(LANES)],
        out_specs=(row(h * LANES), row(h * LANES), row(h * MLA_V)),
        compiler_params=pltpu.CompilerParams(dimension_semantics=("parallel",),
                                             vmem_limit_bytes=limit),
        name="mla_pre",
    )(x, g, *weights, cos_tab, sin_tab)


def _mla_kernel(q_ref, k_ref, v_ref, o_ref):
    tq = q_ref.shape[1]
    qi = pl.program_id(2)
    lane = lax.broadcasted_iota(jnp.int32, (tq, LANES), 1)
    first = lane < MLA_V
    row = lax.broadcasted_iota(jnp.int32, (tq, tq), 0)
    col = lax.broadcasted_iota(jnp.int32, (tq, tq), 1)
    causal = col <= row
    n_heads = q_ref.shape[2] // LANES
    heads = range(n_heads)
    q_heads = [q_ref[0, :, h * LANES:(h + 1) * LANES] for h in heads]

    def kv_block(kb):
        start = pl.multiple_of(kb * tq, tq)
        return k_ref[0, pl.ds(start, tq), :], v_ref[0, pl.ds(start, tq), :]

    def all_heads(kblk, vblk, st, mask):
        ss = [_dot_nt(q_heads[h], kblk[:, h * LANES:(h + 1) * LANES]) for h in heads]
        probs, stats = [], []
        for h in heads:
            s, (m, den) = ss[h], st[3 * h:3 * h + 2]
            if mask is not None:
                s = jnp.where(mask, s, NEG)
            m_new = jnp.maximum(m, jnp.max(s, axis=-1, keepdims=True))
            alpha = jnp.exp(m - m_new)
            e = jnp.exp(s - m_new)
            stats.append((m_new, alpha * den + jnp.sum(e, axis=-1, keepdims=True), alpha))
            probs.append(e.astype(BF16))
        out = []
        for h in heads:
            pv = _dot(probs[h], vblk[:, (h // 2) * LANES:(h // 2 + 1) * LANES])
            out.extend((stats[h][0], stats[h][1], stats[h][2] * st[3 * h + 2] + pv))
        return tuple(out)

    init = (jnp.full((tq, 1), NEG, F32), jnp.zeros((tq, 1), F32),
            jnp.zeros((tq, LANES), F32)) * n_heads
    st = all_heads(*kv_block(qi), init, causal)
    st = lax.fori_loop(0, qi, lambda n, st: all_heads(*kv_block(qi - 1 - n), st, None), st)
    for p in range(n_heads // 2):
        a, b = 2 * p, 2 * p + 1
        o_ref[0, :, p * LANES:(p + 1) * LANES] = jnp.where(
            first, st[3 * a + 2] / st[3 * a + 1], st[3 * b + 2] / st[3 * b + 1]).astype(o_ref.dtype)


MLA_HEADS_PER_STEP = 4


def _mla_attention(q, k, v, batch, seq):
    nh = MLA_HEADS_PER_STEP
    steps = MLA_HEADS // nh
    tq = ATT_TILE
    limit = _vmem_limit(
        _nbytes((tq, nh * LANES), BF16) + _nbytes((seq, nh * LANES), BF16)
        + _nbytes((seq, nh * MLA_V), BF16) + _nbytes((tq, nh * MLA_V), BF16),
        8 * nh * _nbytes((tq, tq), F32))
    return pl.pallas_call(
        _mla_kernel,
        out_shape=jax.ShapeDtypeStruct((batch, seq, MLA_HEADS * MLA_V), BF16),
        grid=(batch, steps, seq // tq),
        in_specs=[pl.BlockSpec((1, tq, nh * LANES), lambda b, p, i: (b, i, p)),
                  pl.BlockSpec((1, seq, nh * LANES), lambda b, p, i: (b, 0, p)),
                  pl.BlockSpec((1, seq, nh * MLA_V), lambda b, p, i: (b, 0, p))],
        out_specs=pl.BlockSpec((1, tq, nh * MLA_V), lambda b, p, i: (b, i, p)),
        compiler_params=pltpu.CompilerParams(
            dimension_semantics=("parallel", "parallel", "parallel"), vmem_limit_bytes=limit),
        name="mla_attention",
    )(q, k, v)


def _ffn_up_kernel(x_ref, g_ref, wg_ref, wu_ref, cwg_ref, cwu_ref, cbg_ref, cbu_ref, o_ref,
                   tail_g_ref, tail_u_ref, *, tiles_per_seq):
    tm = x_ref.shape[0]

    @pl.when(pl.program_id(0) % tiles_per_seq == 0)
    def _():
        tail_g_ref[...] = jnp.zeros_like(tail_g_ref)
        tail_u_ref[...] = jnp.zeros_like(tail_u_ref)

    xn = (_rms_scale(x_ref[...]) * g_ref[...]).astype(BF16)
    row = lax.broadcasted_iota(jnp.int32, (tm, FF_CHUNK), 0)

    def conv(h, tail_ref, cw_ref, cb_ref, sl):
        p1 = tail_ref[7:8, sl]
        p2 = tail_ref[6:7, sl]
        h1 = jnp.where(row == 0, p1, pltpu.roll(h, 1, 0))
        h2 = jnp.where(row == 0, p2, jnp.where(row == 1, p1, pltpu.roll(h, 2, 0)))
        tail_ref[:, sl] = h[tm - 8:, :]
        return cw_ref[0:1, sl] * h2 + cw_ref[1:2, sl] * h1 + cw_ref[2:3, sl] * h + cb_ref[:, sl]

    for c in range(o_ref.shape[1] // FF_CHUNK):
        sl = slice(c * FF_CHUNK, (c + 1) * FF_CHUNK)
        gate = conv(_dot(xn, wg_ref[:, sl]), tail_g_ref, cwg_ref, cbg_ref, sl)
        val = conv(_dot(xn, wu_ref[:, sl]), tail_u_ref, cwu_ref, cbu_ref, sl)
        act = gate * (1.0 / (1.0 + jnp.exp(-gate))) * val
        o_ref[:, sl] = act.astype(o_ref.dtype)


def _ffn_up(x, g, w_up, conv_w, conv_b, seq):
    t, d = x.shape
    wg, wu = w_up[:, :D_FF].astype(BF16), w_up[:, D_FF:].astype(BF16)
    cwg, cwu = conv_w[:, :D_FF], conv_w[:, D_FF:]
    cbg, cbu = conv_b[:D_FF].reshape(1, D_FF), conv_b[D_FF:].reshape(1, D_FF)
    assert seq % ROW_TILE == 0 and D_FF % FF_CHUNK == 0
    full = lambda a: pl.BlockSpec(a.shape, lambda i: (0,) * a.ndim)
    limit = _vmem_limit(
        _nbytes((ROW_TILE, d), F32) + 2 * _nbytes((d, D_FF), BF16)
        + _nbytes((ROW_TILE, D_FF), BF16) + 8 * _nbytes((1, D_FF), F32),
        2 * _nbytes((8, D_FF), F32) + _nbytes((ROW_TILE, d), F32)
        + 12 * _nbytes((ROW_TILE, FF_CHUNK), F32))
    return pl.pallas_call(
        functools.partial(_ffn_up_kernel, tiles_per_seq=seq // ROW_TILE),
        out_shape=jax.ShapeDtypeStruct((t, D_FF), BF16),
        grid=(t // ROW_TILE,),
        in_specs=[pl.BlockSpec((ROW_TILE, d), lambda i: (i, 0)), full(g), full(wg), full(wu),
                  full(cwg), full(cwu), full(cbg), full(cbu)],
        out_specs=pl.BlockSpec((ROW_TILE, D_FF), lambda i: (i, 0)),
        scratch_shapes=[pltpu.VMEM((8, D_FF), F32), pltpu.VMEM((8, D_FF), F32)],
        compiler_params=pltpu.CompilerParams(dimension_semantics=("arbitrary",),
                                             vmem_limit_bytes=limit),
        name="ffn_up_conv_gate",
    )(x, g, wg, wu, cwg, cwu, cbg, cbu)


def kernel(x, positions, rel_bias, norm_gains, sb_w_qkv, sb_w_o, dil_w_qkv, dil_w_o, mla_w_in,
           mla_q_norm, mla_w_qb, mla_kv_norm, mla_w_kvb, mla_w_o, ffn_w_up, ffn_conv_w,
           ffn_conv_b, ffn_w_down):
    batch, seq, d = x.shape
    t = batch * seq
    x = x.reshape(t, d)
    for i in range(DEPTH):
        kind, j = i % N_MIXERS, i // N_MIXERS
        gain = lambda n: norm_gains[i, n].reshape(1, d)
        if kind == 0:
            qkv = _norm_proj(x, gain(0), sb_w_qkv[j].astype(BF16), "sb_qkv_proj")
            o = _sb_attention(qkv.reshape(batch, seq, -1), batch, seq).reshape(t, -1)
            x = _proj_post(o, sb_w_o[j].astype(BF16), x, gain(1), "sb_out_proj")
        elif kind == 1:
            qkv = _norm_proj(x, gain(0), dil_w_qkv[j].astype(BF16), "dil_qkv_proj")
            qkv = qkv.reshape(batch, seq, -1)
            outs, lses = [], []
            for grp, (window, dilation) in enumerate(DIL_GROUPS):
                n_back = window // dilation
                assert n_back == DIL_TILE and (seq // dilation) % DIL_TILE == 0
                bias = _dil_bias(rel_bias, grp, dilation, n_back)
                o, lse = _dil_group_attention(qkv, bias, grp, dilation, batch, seq)
                outs.append(o)
                lses.append(lse)
            x = _dil_merge_proj(outs, lses, dil_w_o[j].astype(BF16), x, gain(1))
        else:
            q, k, v = _mla_pre(x, gain(0), positions, mla_w_in[j], mla_q_norm[j], mla_w_qb[j],
                               mla_kv_norm[j], mla_w_kvb[j])
            o = _mla_attention(q.reshape(batch, seq, -1), k.reshape(batch, seq, -1),
                               v.reshape(batch, seq, -1), batch, seq).reshape(t, -1)
            x = _proj_post(o, mla_w_o[j].astype(BF16), x, gain(1), "mla_out_proj")
        act = _ffn_up(x, gain(2), ffn_w_up[i], ffn_conv_w[i], ffn_conv_b[i], seq)
        x = _proj_post(act, ffn_w_down[i].astype(BF16), x, gain(3), "ffn_down_proj")
    return x.reshape(batch, seq, d)
```

```python
import functools
import math

import jax
import jax.numpy as jnp
from jax import lax
from jax.experimental import pallas as pl
from jax.experimental.pallas import tpu as pltpu

F32 = jnp.float32
BF16 = jnp.bfloat16

D_MODEL = 1024
DEPTH = 4
N_MIXERS = 3
EPS = 1e-6
NEG = -1e30

SB_HEADS = 16
HEAD_DIM = 64

DIL_GROUPS = ((128, 1), (512, 4), (2048, 16))
DIL_HEADS = 8
N_BUCKETS = 32
BUCKET_MAX_DIST = 2048

MLA_HEADS = 16
MLA_Q_RANK = 384
MLA_KV_RANK = 256
MLA_NOPE = 64
MLA_ROPE = 32
MLA_V = 64
ROPE_THETA = 10000.0

D_FF = 2816

LANES = 128
V7X_VMEM_CAP_BYTES = 56 * 1024 * 1024

ROW_TILE = 512
ATT_TILE = 256
DIL_TILE = 128
FF_CHUNK = 256


def _vmem_limit(pipelined_bytes, resident_bytes):
    return int(min(V7X_VMEM_CAP_BYTES, 2 * pipelined_bytes + resident_bytes))


def _nbytes(shape, dtype):
    return math.prod(shape) * jnp.dtype(dtype).itemsize


def _rms_scale(x):
    return x * lax.rsqrt(jnp.mean(x * x, axis=-1, keepdims=True) + EPS)


def _dot(a, b):
    return jnp.dot(a, b, preferred_element_type=F32)


def _neg_abs(x):
    bits = lax.bitcast_convert_type(x, jnp.uint32) | jnp.uint32(0x80000000)
    return lax.bitcast_convert_type(bits, F32)


def _dot_nt(a, b):
    return lax.dot_general(a, b, (((1,), (1,)), ((), ())), preferred_element_type=F32)


def _norm_proj_kernel(x_ref, g_ref, w_ref, o_ref, *, n_chunk):
    xn = (_rms_scale(x_ref[...]) * g_ref[...]).astype(BF16)
    for c in range(o_ref.shape[1] // n_chunk):
        sl = slice(c * n_chunk, (c + 1) * n_chunk)
        o_ref[:, sl] = _dot(xn, w_ref[:, sl]).astype(o_ref.dtype)


def _norm_proj(x, g, w, name):
    t, d = x.shape
    n = w.shape[1]
    n_chunk = 512
    assert t % ROW_TILE == 0 and n % n_chunk == 0
    limit = _vmem_limit(
        _nbytes((ROW_TILE, d), F32) + _nbytes((d, n), BF16) + _nbytes((ROW_TILE, n), BF16),
        _nbytes((ROW_TILE, d), F32) + 2 * _nbytes((ROW_TILE, n_chunk), F32))
    return pl.pallas_call(
        functools.partial(_norm_proj_kernel, n_chunk=n_chunk),
        out_shape=jax.ShapeDtypeStruct((t, n), BF16),
        grid=(t // ROW_TILE,),
        in_specs=[pl.BlockSpec((ROW_TILE, d), lambda i: (i, 0)),
                  pl.BlockSpec((1, d), lambda i: (0, 0)),
                  pl.BlockSpec((d, n), lambda i: (0, 0))],
        out_specs=pl.BlockSpec((ROW_TILE, n), lambda i: (i, 0)),
        compiler_params=pltpu.CompilerParams(dimension_semantics=("parallel",),
                                             vmem_limit_bytes=limit),
        name=name,
    )(x, g, w)


def _proj_post_kernel(a_ref, w_ref, x_ref, g_ref, o_ref):
    m = _dot(a_ref[...], w_ref[...])
    o_ref[...] = x_ref[...] + _rms_scale(m) * g_ref[...]


def _proj_post(a, w, x, g, name):
    t, k = a.shape
    d = w.shape[1]
    limit = _vmem_limit(
        _nbytes((ROW_TILE, k), BF16) + _nbytes((k, d), BF16) + 2 * _nbytes((ROW_TILE, d), F32),
        2 * _nbytes((ROW_TILE, d), F32))
    return pl.pallas_call(
        _proj_post_kernel,
        out_shape=jax.ShapeDtypeStruct((t, d), F32),
        grid=(t // ROW_TILE,),
        in_specs=[pl.BlockSpec((ROW_TILE, k), lambda i: (i, 0)),
                  pl.BlockSpec((k, d), lambda i: (0, 0)),
                  pl.BlockSpec((ROW_TILE, d), lambda i: (i, 0)),
                  pl.BlockSpec((1, d), lambda i: (0, 0))],
        out_specs=pl.BlockSpec((ROW_TILE, d), lambda i: (i, 0)),
        compiler_params=pltpu.CompilerParams(dimension_semantics=("parallel",),
                                             vmem_limit_bytes=limit),
        name=name,
    )(a, w, x, g)


SB_UNDERFLOW = 105.0


def _sb_kernel(q_ref, k_ref, v_ref, o_ref):
    tq = q_ref.shape[1]
    n_pairs = q_ref.shape[2] // LANES
    qi = pl.program_id(2)
    lane = lax.broadcasted_iota(jnp.int32, (tq, LANES), 1)
    first = lane < HEAD_DIM
    q_heads = []
    for p in range(n_pairs):
        q2 = q_ref[0, :, p * LANES:(p + 1) * LANES] * jnp.asarray(HEAD_DIM ** -0.5, BF16)
        zero = jnp.zeros_like(q2)
        q_heads.append(jnp.where(first, q2, zero))
        q_heads.append(jnp.where(first, zero, q2))

    row = lax.broadcasted_iota(jnp.int32, (tq, tq), 0)
    col = lax.broadcasted_iota(jnp.int32, (tq, tq), 1)
    later = jnp.where(row > col, 1.0, 0.0).astype(BF16)
    causal = col < row

    def kv_block(kb):
        start = pl.multiple_of(kb * tq, tq)
        return k_ref[0, pl.ds(start, tq), :], v_ref[0, pl.ds(start, tq), :]

    def all_heads(kblk, vblk, st, mask):
        pair = lambda h: slice((h // 2) * LANES, (h // 2 + 1) * LANES)
        heads = range(len(q_heads))
        zs = [_dot_nt(q_heads[h], kblk[:, pair(h)]) for h in heads]
        mids = []
        for h in heads:
            z = zs[h]
            softplus = jnp.maximum(z, 0.0) + jnp.log(1.0 + jnp.exp(_neg_abs(z)))
            base = (z - softplus) - st[2 * h]
            if mask is not None:
                softplus = jnp.where(mask, softplus, 0.0)
            mids.append((softplus.astype(BF16), base, jnp.sum(softplus, axis=-1, keepdims=True)))
        betweens = [_dot(sp, later) for sp, _, _ in mids]
        weights = []
        for h in heads:
            a = jnp.exp(mids[h][1] - betweens[h])
            if mask is not None:
                a = jnp.where(mask, a, 0.0)
            weights.append(a.astype(BF16))
        out = []
        for h in heads:
            out.append(st[2 * h] + mids[h][2])
            out.append(st[2 * h + 1] + _dot(weights[h], vblk[:, pair(h)]))
        return tuple(out)

    def smallest_carry(st):
        return functools.reduce(jnp.minimum, [jnp.min(c) for c in st[0::2]])

    init = (jnp.zeros((tq, 1), F32), jnp.zeros((tq, LANES), F32)) * len(q_heads)
    st = all_heads(*kv_block(qi), init, causal)

    def more(carry):
        n, low, _ = carry
        return jnp.logical_and(n < qi, low < SB_UNDERFLOW)

    def body(carry):
        n, _, st = carry
        st = all_heads(*kv_block(qi - 1 - n), st, None)
        return n + 1, smallest_carry(st), st

    _, _, st = lax.while_loop(more, body, (jnp.int32(0), smallest_carry(st), st))
    for p in range(n_pairs):
        o_ref[0, :, p * LANES:(p + 1) * LANES] = jnp.where(
            first, st[4 * p + 1], st[4 * p + 3]).astype(o_ref.dtype)


SB_HEADS_PER_STEP = 4


def _sb_attention(qkv, batch, seq):
    width = SB_HEADS_PER_STEP * HEAD_DIM
    pairs = SB_HEADS // SB_HEADS_PER_STEP
    tq = ATT_TILE
    limit = _vmem_limit(
        2 * _nbytes((tq, width), BF16) + 2 * _nbytes((seq, width), BF16),
        12 * SB_HEADS_PER_STEP * _nbytes((tq, tq), F32))
    return pl.pallas_call(
        _sb_kernel,
        out_shape=jax.ShapeDtypeStruct((batch, seq, SB_HEADS * HEAD_DIM), BF16),
        grid=(batch, pairs, seq // tq),
        in_specs=[pl.BlockSpec((1, tq, width), lambda b, p, i: (b, i, p)),
                  pl.BlockSpec((1, seq, width), lambda b, p, i: (b, 0, pairs + p)),
                  pl.BlockSpec((1, seq, width), lambda b, p, i: (b, 0, 2 * pairs + p))],
        out_specs=pl.BlockSpec((1, tq, width), lambda b, p, i: (b, i, p)),
        compiler_params=pltpu.CompilerParams(
            dimension_semantics=("parallel", "parallel", "parallel"), vmem_limit_bytes=limit),
        name="sb_attention",
    )(qkv, qkv, qkv)


def _dil_kernel(q_ref, kp_ref, kc_ref, vp_ref, vc_ref, bias_ref, o_ref, lse_ref):
    w = q_ref.shape[1]
    ut = pl.program_id(2)
    lane = lax.broadcasted_iota(jnp.int32, (w, LANES), 1)
    first = lane < HEAD_DIM
    col = lax.broadcasted_iota(jnp.int32, (w, 2 * w), 1)
    key_ok = jnp.logical_or(col >= w, ut > 0)
    for p in range(q_ref.shape[2] // LANES):
        sl = slice(p * LANES, (p + 1) * LANES)
        q2 = q_ref[0, :, sl] * jnp.asarray(HEAD_DIM ** -0.5, BF16)
        zero = jnp.zeros_like(q2)
        k2 = jnp.concatenate([kp_ref[0, :, sl], kc_ref[0, :, sl]], axis=0)
        v2 = jnp.concatenate([vp_ref[0, :, sl], vc_ref[0, :, sl]], axis=0)
        outs, lses = [], []
        for h, qh in enumerate((jnp.where(first, q2, zero), jnp.where(first, zero, q2))):
            s = _dot_nt(qh, k2) + bias_ref[2 * p + h]
            s = jnp.where(key_ok, s, NEG)
            m = jnp.max(s, axis=-1, keepdims=True)
            e = jnp.exp(s - m)
            den = jnp.sum(e, axis=-1, keepdims=True)
            outs.append(_dot(e.astype(BF16), v2) / den)
            lses.append(m + jnp.log(den))
        o_ref[0, :, sl] = jnp.where(first, outs[0], outs[1]).astype(o_ref.dtype)
        lse_ref[0, :, sl] = jnp.where(first, lses[0], lses[1])


def _dil_bias(rel_bias, group, dilation, n_back):
    w = n_back
    dist = jnp.arange(w + 1) * dilation
    max_exact = N_BUCKETS // 2
    d = jnp.maximum(dist.astype(F32), 1.0)
    large = max_exact + (jnp.log(d / max_exact) / math.log(BUCKET_MAX_DIST / max_exact)
                         * (N_BUCKETS - max_exact)).astype(jnp.int32)
    bucket = jnp.where(dist < max_exact, dist, jnp.minimum(large, N_BUCKETS - 1))
    per_m = rel_bias[:, group * DIL_HEADS:(group + 1) * DIL_HEADS][bucket].astype(F32).T
    period = jnp.concatenate(
        [per_m[:, ::-1], jnp.full((DIL_HEADS, w), NEG, F32)], axis=1)
    tiled = jnp.tile(period, (1, w))[:, :w * 2 * w]
    return tiled.reshape(DIL_HEADS, w, 2 * w)


def _dil_group_attention(qkv, bias, group, dilation, batch, seq):
    n_groups = len(DIL_GROUPS)
    width = DIL_HEADS * HEAD_DIM
    length = seq // dilation
    w = DIL_TILE
    if dilation == 1:
        qv = qkv
        index = lambda which: (lambda b, c, u: (b, u, which * n_groups + group))
    else:
        qv = qkv.reshape(batch, length, dilation, 3, n_groups, width)[:, :, :, :, group]
        qv = qv.transpose(0, 2, 3, 1, 4).reshape(batch * dilation * 3, length, width)
        index = lambda which: (lambda b, c, u: ((b * dilation + c) * 3 + which, u, 0))

    def cur(which):
        return pl.BlockSpec((1, w, width), index(which))

    def prev(which):
        at = index(which)
        return pl.BlockSpec((1, w, width), lambda b, c, u: at(b, c, jnp.maximum(u - 1, 0)))

    out_spec = pl.BlockSpec((1, w, width), lambda b, c, u: (b, u, c))
    limit = _vmem_limit(
        5 * _nbytes((w, width), BF16) + _nbytes((DIL_HEADS, w, 2 * w), F32)
        + _nbytes((w, width), BF16) + _nbytes((w, width), F32),
        32 * _nbytes((w, 2 * w), F32))
    o, lse = pl.pallas_call(
        _dil_kernel,
        out_shape=(jax.ShapeDtypeStruct((batch, length, dilation * width), BF16),
                   jax.ShapeDtypeStruct((batch, length, dilation * width), F32)),
        grid=(batch, dilation, length // w),
        in_specs=[cur(0), prev(1), cur(1), prev(2), cur(2),
                  pl.BlockSpec((DIL_HEADS, w, 2 * w), lambda b, c, u: (0, 0, 0))],
        out_specs=(out_spec, out_spec),
        compiler_params=pltpu.CompilerParams(
            dimension_semantics=("parallel", "parallel", "parallel"), vmem_limit_bytes=limit),
        name=f"dil_attention_g{group}",
    )(qv, qv, qv, qv, qv, bias)
    return o.reshape(batch * seq, width), lse.reshape(batch * seq, width)


def _dil_merge_kernel(o0_ref, o1_ref, o2_ref, l0_ref, l1_ref, l2_ref, w_ref, x_ref, g_ref, out_ref):
    l0, l1, l2 = l0_ref[...], l1_ref[...], l2_ref[...]
    m = jnp.maximum(jnp.maximum(l0, l1), l2)
    e0, e1, e2 = jnp.exp(l0 - m), jnp.exp(l1 - m), jnp.exp(l2 - m)
    inv = 1.0 / (e0 + e1 + e2)
    o = (o0_ref[...].astype(F32) * (e0 * inv) + o1_ref[...].astype(F32) * (e1 * inv)
         + o2_ref[...].astype(F32) * (e2 * inv))
    mo = _dot(o.astype(BF16), w_ref[...])
    out_ref[...] = x_ref[...] + _rms_scale(mo) * g_ref[...]


def _dil_merge_proj(outs, lses, w, x, g):
    t, k = outs[0].shape
    d = w.shape[1]
    row = lambda i: (i, 0)
    limit = _vmem_limit(
        3 * _nbytes((ROW_TILE, k), BF16) + 3 * _nbytes((ROW_TILE, k), F32)
        + _nbytes((k, d), BF16) + 2 * _nbytes((ROW_TILE, d), F32),
        8 * _nbytes((ROW_TILE, k), F32) + 2 * _nbytes((ROW_TILE, d), F32))
    return pl.pallas_call(
        _dil_merge_kernel,
        out_shape=jax.ShapeDtypeStruct((t, d), F32),
        grid=(t // ROW_TILE,),
        in_specs=[pl.BlockSpec((ROW_TILE, k), row)] * 6
        + [pl.BlockSpec((k, d), lambda i: (0, 0)),
           pl.BlockSpec((ROW_TILE, d), row),
           pl.BlockSpec((1, d), lambda i: (0, 0))],
        out_specs=pl.BlockSpec((ROW_TILE, d), row),
        compiler_params=pltpu.CompilerParams(dimension_semantics=("parallel",),
                                             vmem_limit_bytes=limit),
        name="dil_merge_proj",
    )(*outs, *lses, w, x, g)


def _mla_pre_kernel(x_ref, g_ref, wcq_ref, wckv_ref, wkr_ref, wkr_rot_ref, qn_ref, kvn_ref,
                    wq_ref, wq_rot_ref, wk_ref, wv_ref, cos_ref, sin_ref,
                    q_ref, k_ref, v_ref, *, q_scale):
    xn = (_rms_scale(x_ref[...]) * g_ref[...]).astype(BF16)
    cq = (_rms_scale(_dot(xn, wcq_ref[...])) * qn_ref[...]).astype(BF16)
    ckv = (_rms_scale(_dot(xn, wckv_ref[...])) * kvn_ref[...]).astype(BF16)
    cos, sin = cos_ref[...], sin_ref[...]
    k_rope = _dot(xn, wkr_ref[...]) * cos + _dot(xn, wkr_rot_ref[...]) * sin
    for h in range(MLA_HEADS):
        sl = slice(h * LANES, (h + 1) * LANES)
        q = _dot(cq, wq_ref[:, sl]) * cos + _dot(cq, wq_rot_ref[:, sl]) * sin
        q_ref[:, sl] = (q * q_scale).astype(q_ref.dtype)
        k_ref[:, sl] = (_dot(ckv, wk_ref[:, sl]) + k_rope).astype(k_ref.dtype)
    v_ref[...] = _dot(ckv, wv_ref[...]).astype(v_ref.dtype)


def _rope_rotation(w):
    half = MLA_ROPE // 2
    return jnp.concatenate([-w[..., half:], w[..., :half]], axis=-1)


def _pad_head_slabs(nope, rope):
    k, h = nope.shape[0], nope.shape[1]
    pad = jnp.zeros((k, h, LANES - MLA_NOPE - MLA_ROPE), nope.dtype)
    return jnp.concatenate([nope, rope, pad], axis=-1).reshape(k, h * LANES)


def _mla_pre(x, g, positions, w_in, q_norm, w_qb, kv_norm, w_kvb):
    t, d = x.shape
    h = MLA_HEADS
    w_cq = w_in[:, :MLA_Q_RANK].astype(BF16)
    w_ckv = w_in[:, MLA_Q_RANK:MLA_Q_RANK + MLA_KV_RANK].astype(BF16)
    w_kr = w_in[:, MLA_Q_RANK + MLA_KV_RANK:]
    zeros_nope = jnp.zeros((d, 1, MLA_NOPE), F32)
    w_kr_pad = _pad_head_slabs(zeros_nope, w_kr[:, None, :]).astype(BF16)
    w_kr_rot = _pad_head_slabs(zeros_nope, _rope_rotation(w_kr)[:, None, :]).astype(BF16)

    wq = w_qb.reshape(MLA_Q_RANK, h, MLA_NOPE + MLA_ROPE)
    wq_pad = _pad_head_slabs(wq[..., :MLA_NOPE], wq[..., MLA_NOPE:]).astype(BF16)
    wq_rot = _pad_head_slabs(jnp.zeros_like(wq[..., :MLA_NOPE]),
                             _rope_rotation(wq[..., MLA_NOPE:])).astype(BF16)
    wkv = w_kvb.reshape(MLA_KV_RANK, h, MLA_NOPE + MLA_V)
    wk_pad = _pad_head_slabs(wkv[..., :MLA_NOPE],
                             jnp.zeros((MLA_KV_RANK, h, MLA_ROPE), F32)).astype(BF16)
    wv = wkv[..., MLA_NOPE:].reshape(MLA_KV_RANK, h * MLA_V).astype(BF16)

    half = MLA_ROPE // 2
    freqs = ROPE_THETA ** (-jnp.arange(half, dtype=F32) / half)
    ang = positions.astype(F32).reshape(t, 1) * freqs
    cos, sin = jnp.cos(ang), jnp.sin(ang)
    tail = jnp.zeros((t, LANES - MLA_NOPE - MLA_ROPE), F32)
    cos_tab = jnp.concatenate([jnp.ones((t, MLA_NOPE), F32), cos, cos, tail], axis=-1)
    sin_tab = jnp.concatenate([jnp.zeros((t, MLA_NOPE), F32), sin, sin, tail], axis=-1)

    tm = ROW_TILE // 2
    full = lambda a: pl.BlockSpec(a.shape, lambda i: (0,) * a.ndim)
    row = lambda n: pl.BlockSpec((tm, n), lambda i: (i, 0))
    weights = (w_cq, w_ckv, w_kr_pad, w_kr_rot, q_norm.reshape(1, -1), kv_norm.reshape(1, -1),
               wq_pad, wq_rot, wk_pad, wv)
    limit = _vmem_limit(
        _nbytes((tm, d), F32) + sum(_nbytes(a.shape, a.dtype) for a in weights)
        + 2 * _nbytes((tm, LANES), F32) + 2 * _nbytes((tm, h * LANES), BF16)
        + _nbytes((tm, h * MLA_V), BF16),
        4 * _nbytes((tm, d), F32))
    return pl.pallas_call(
        functools.partial(_mla_pre_kernel, q_scale=(MLA_NOPE + MLA_ROPE) ** -0.5),
        out_shape=(jax.ShapeDtypeStruct((t, h * LANES), BF16),
                   jax.ShapeDtypeStruct((t, h * LANES), BF16),
                   jax.ShapeDtypeStruct((t, h * MLA_V), BF16)),
        grid=(t // tm,),
        in_specs=[row(d), full(g)] + [full(a) for a in weights] + [row(LANES), row(LANES)],
        out_specs=(row(h * LANES), row(h * LANES), row(h * MLA_V)),
        compiler_params=pltpu.CompilerParams(dimension_semantics=("parallel",),
                                             vmem_limit_bytes=limit),
        name="mla_pre",
    )(x, g, *weights, cos_tab, sin_tab)


def _mla_kernel(q_ref, k_ref, v_ref, o_ref):
    tq = q_ref.shape[1]
    qi = pl.program_id(2)
    lane = lax.broadcasted_iota(jnp.int32, (tq, LANES), 1)
    first = lane < MLA_V
    row = lax.broadcasted_iota(jnp.int32, (tq, tq), 0)
    col = lax.broadcasted_iota(jnp.int32, (tq, tq), 1)
    causal = col <= row
    n_heads = q_ref.shape[2] // LANES
    heads = range(n_heads)
    q_heads = [q_ref[0, :, h * LANES:(h + 1) * LANES] for h in heads]

    def kv_block(kb):
        start = pl.multiple_of(kb * tq, tq)
        return k_ref[0, pl.ds(start, tq), :], v_ref[0, pl.ds(start, tq), :]

    def all_heads(kblk, vblk, st, mask):
        ss = [_dot_nt(q_heads[h], kblk[:, h * LANES:(h + 1) * LANES]) for h in heads]
        probs, stats = [], []
        for h in heads:
            s, (m, den) = ss[h], st[3 * h:3 * h + 2]
            if mask is not None:
                s = jnp.where(mask, s, NEG)
            m_new = jnp.maximum(m, jnp.max(s, axis=-1, keepdims=True))
            alpha = jnp.exp(m - m_new)
            e = jnp.exp(s - m_new)
            stats.append((m_new, alpha * den + jnp.sum(e, axis=-1, keepdims=True), alpha))
            probs.append(e.astype(BF16))
        out = []
        for h in heads:
            pv = _dot(probs[h], vblk[:, (h // 2) * LANES:(h // 2 + 1) * LANES])
            out.extend((stats[h][0], stats[h][1], stats[h][2] * st[3 * h + 2] + pv))
        return tuple(out)

    init = (jnp.full((tq, 1), NEG, F32), jnp.zeros((tq, 1), F32),
            jnp.zeros((tq, LANES), F32)) * n_heads
    st = all_heads(*kv_block(qi), init, causal)
    st = lax.fori_loop(0, qi, lambda n, st: all_heads(*kv_block(qi - 1 - n), st, None), st)
    for p in range(n_heads // 2):
        a, b = 2 * p, 2 * p + 1
        o_ref[0, :, p * LANES:(p + 1) * LANES] = jnp.where(
            first, st[3 * a + 2] / st[3 * a + 1], st[3 * b + 2] / st[3 * b + 1]).astype(o_ref.dtype)


MLA_HEADS_PER_STEP = 4


def _mla_attention(q, k, v, batch, seq):
    nh = MLA_HEADS_PER_STEP
    steps = MLA_HEADS // nh
    tq = ATT_TILE
    limit = _vmem_limit(
        _nbytes((tq, nh * LANES), BF16) + _nbytes((seq, nh * LANES), BF16)
        + _nbytes((seq, nh * MLA_V), BF16) + _nbytes((tq, nh * MLA_V), BF16),
        8 * nh * _nbytes((tq, tq), F32))
    return pl.pallas_call(
        _mla_kernel,
        out_shape=jax.ShapeDtypeStruct((batch, seq, MLA_HEADS * MLA_V), BF16),
        grid=(batch, steps, seq // tq),
        in_specs=[pl.BlockSpec((1, tq, nh * LANES), lambda b, p, i: (b, i, p)),
                  pl.BlockSpec((1, seq, nh * LANES), lambda b, p, i: (b, 0, p)),
                  pl.BlockSpec((1, seq, nh * MLA_V), lambda b, p, i: (b, 0, p))],
        out_specs=pl.BlockSpec((1, tq, nh * MLA_V), lambda b, p, i: (b, i, p)),
        compiler_params=pltpu.CompilerParams(
            dimension_semantics=("parallel", "parallel", "parallel"), vmem_limit_bytes=limit),
        name="mla_attention",
    )(q, k, v)


def _ffn_up_kernel(x_ref, g_ref, wg_ref, wu_ref, cwg_ref, cwu_ref, cbg_ref, cbu_ref, o_ref,
                   tail_g_ref, tail_u_ref, *, tiles_per_seq):
    tm = x_ref.shape[0]

    @pl.when(pl.program_id(0) % tiles_per_seq == 0)
    def _():
        tail_g_ref[...] = jnp.zeros_like(tail_g_ref)
        tail_u_ref[...] = jnp.zeros_like(tail_u_ref)

    xn = (_rms_scale(x_ref[...]) * g_ref[...]).astype(BF16)
    row = lax.broadcasted_iota(jnp.int32, (tm, FF_CHUNK), 0)

    def conv(h, tail_ref, cw_ref, cb_ref, sl):
        p1 = tail_ref[7:8, sl]
        p2 = tail_ref[6:7, sl]
        h1 = jnp.where(row == 0, p1, pltpu.roll(h, 1, 0))
        h2 = jnp.where(row == 0, p2, jnp.where(row == 1, p1, pltpu.roll(h, 2, 0)))
        tail_ref[:, sl] = h[tm - 8:, :]
        return cw_ref[0:1, sl] * h2 + cw_ref[1:2, sl] * h1 + cw_ref[2:3, sl] * h + cb_ref[:, sl]

    for c in range(o_ref.shape[1] // FF_CHUNK):
        sl = slice(c * FF_CHUNK, (c + 1) * FF_CHUNK)
        gate = conv(_dot(xn, wg_ref[:, sl]), tail_g_ref, cwg_ref, cbg_ref, sl)
        val = conv(_dot(xn, wu_ref[:, sl]), tail_u_ref, cwu_ref, cbu_ref, sl)
        act = gate * (1.0 / (1.0 + jnp.exp(-gate))) * val
        o_ref[:, sl] = act.astype(o_ref.dtype)


def _ffn_up(x, g, w_up, conv_w, conv_b, seq):
    t, d = x.shape
    wg, wu = w_up[:, :D_FF].astype(BF16), w_up[:, D_FF:].astype(BF16)
    cwg, cwu = conv_w[:, :D_FF], conv_w[:, D_FF:]
    cbg, cbu = conv_b[:D_FF].reshape(1, D_FF), conv_b[D_FF:].reshape(1, D_FF)
    assert seq % ROW_TILE == 0 and D_FF % FF_CHUNK == 0
    full = lambda a: pl.BlockSpec(a.shape, lambda i: (0,) * a.ndim)
    limit = _vmem_limit(
        _nbytes((ROW_TILE, d), F32) + 2 * _nbytes((d, D_FF), BF16)
        + _nbytes((ROW_TILE, D_FF), BF16) + 8 * _nbytes((1, D_FF), F32),
        2 * _nbytes((8, D_FF), F32) + _nbytes((ROW_TILE, d), F32)
        + 12 * _nbytes((ROW_TILE, FF_CHUNK), F32))
    return pl.pallas_call(
        functools.partial(_ffn_up_kernel, tiles_per_seq=seq // ROW_TILE),
        out_shape=jax.ShapeDtypeStruct((t, D_FF), BF16),
        grid=(t // ROW_TILE,),
        in_specs=[pl.BlockSpec((ROW_TILE, d), lambda i: (i, 0)), full(g), full(wg), full(wu),
                  full(cwg), full(cwu), full(cbg), full(cbu)],
        out_specs=pl.BlockSpec((ROW_TILE, D_FF), lambda i: (i, 0)),
        scratch_shapes=[pltpu.VMEM((8, D_FF), F32), pltpu.VMEM((8, D_FF), F32)],
        compiler_params=pltpu.CompilerParams(dimension_semantics=("arbitrary",),
                                             vmem_limit_bytes=limit),
        name="ffn_up_conv_gate",
    )(x, g, wg, wu, cwg, cwu, cbg, cbu)


def kernel(x, positions, rel_bias, norm_gains, sb_w_qkv, sb_w_o, dil_w_qkv, dil_w_o, mla_w_in,
           mla_q_norm, mla_w_qb, mla_kv_norm, mla_w_kvb, mla_w_o, ffn_w_up, ffn_conv_w,
           ffn_conv_b, ffn_w_down):
    batch, seq, d = x.shape
    t = batch * seq
    x = x.reshape(t, d)
    for i in range(DEPTH):
        kind, j = i % N_MIXERS, i // N_MIXERS
        gain = lambda n: norm_gains[i, n].reshape(1, d)
        if kind == 0:
            qkv = _norm_proj(x, gain(0), sb_w_qkv[j].astype(BF16), "sb_qkv_proj")
            o = _sb_attention(qkv.reshape(batch, seq, -1), batch, seq).reshape(t, -1)
            x = _proj_post(o, sb_w_o[j].astype(BF16), x, gain(1), "sb_out_proj")
        elif kind == 1:
            qkv = _norm_proj(x, gain(0), dil_w_qkv[j].astype(BF16), "dil_qkv_proj")
            qkv = qkv.reshape(batch, seq, -1)
            outs, lses = [], []
            for grp, (window, dilation) in enumerate(DIL_GROUPS):
                n_back = window // dilation
                assert n_back == DIL_TILE and (seq // dilation) % DIL_TILE == 0
                bias = _dil_bias(rel_bias, grp, dilation, n_back)
                o, lse = _dil_group_attention(qkv, bias, grp, dilation, batch, seq)
                outs.append(o)
                lses.append(lse)
            x = _dil_merge_proj(outs, lses, dil_w_o[j].astype(BF16), x, gain(1))
        else:
            q, k, v = _mla_pre(x, gain(0), positions, mla_w_in[j], mla_q_norm[j], mla_w_qb[j],
                               mla_kv_norm[j], mla_w_kvb[j])
            o = _mla_attention(q.reshape(batch, seq, -1), k.reshape(batch, seq, -1),
                               v.reshape(batch, seq, -1), batch, seq).reshape(t, -1)
            x = _proj_post(o, mla_w_o[j].astype(BF16), x, gain(1), "mla_out_proj")
        act = _ffn_up(x, gain(2), ffn_w_up[i], ffn_conv_w[i], ffn_conv_b[i], seq)
        x = _proj_post(act, ffn_w_down[i].astype(BF16), x, gain(3), "ffn_down_proj")
    return x.reshape(batch, seq, d)
```

```python
import functools
import math

import jax
import jax.numpy as jnp
from jax import lax
from jax.experimental import pallas as pl
from jax.experimental.pallas import tpu as pltpu

F32 = jnp.float32
BF16 = jnp.bfloat16

D_MODEL = 1024
DEPTH = 4
N_MIXERS = 3
EPS = 1e-6
NEG = -1e30

SB_HEADS = 16
HEAD_DIM = 64

DIL_GROUPS = ((128, 1), (512, 4), (2048, 16))
DIL_HEADS = 8
N_BUCKETS = 32
BUCKET_MAX_DIST = 2048

MLA_HEADS = 16
MLA_Q_RANK = 384
MLA_KV_RANK = 256
MLA_NOPE = 64
MLA_ROPE = 32
MLA_V = 64
ROPE_THETA = 10000.0

D_FF = 2816

LANES = 128
V7X_VMEM_CAP_BYTES = 56 * 1024 * 1024

ROW_TILE = 512
ATT_TILE = 256
DIL_TILE = 128
FF_CHUNK = 256


def _vmem_limit(pipelined_bytes, resident_bytes):
    return int(min(V7X_VMEM_CAP_BYTES, 2 * pipelined_bytes + resident_bytes))


def _nbytes(shape, dtype):
    return math.prod(shape) * jnp.dtype(dtype).itemsize


def _rms_scale(x):
    return x * lax.rsqrt(jnp.mean(x * x, axis=-1, keepdims=True) + EPS)


def _dot(a, b):
    return jnp.dot(a, b, preferred_element_type=F32)


def _neg_abs(x):
    bits = lax.bitcast_convert_type(x, jnp.uint32) | jnp.uint32(0x80000000)
    return lax.bitcast_convert_type(bits, F32)


def _dot_nt(a, b):
    return lax.dot_general(a, b, (((1,), (1,)), ((), ())), preferred_element_type=F32)


def _norm_proj_kernel(x_ref, g_ref, w_ref, o_ref, *, n_chunk):
    xn = (_rms_scale(x_ref[...]) * g_ref[...]).astype(BF16)
    for c in range(o_ref.shape[1] // n_chunk):
        sl = slice(c * n_chunk, (c + 1) * n_chunk)
        o_ref[:, sl] = _dot(xn, w_ref[:, sl]).astype(o_ref.dtype)


def _norm_proj(x, g, w, name):
    t, d = x.shape
    n = w.shape[1]
    n_chunk = 512
    assert t % ROW_TILE == 0 and n % n_chunk == 0
    limit = _vmem_limit(
        _nbytes((ROW_TILE, d), F32) + _nbytes((d, n), BF16) + _nbytes((ROW_TILE, n), BF16),
        _nbytes((ROW_TILE, d), F32) + 2 * _nbytes((ROW_TILE, n_chunk), F32))
    return pl.pallas_call(
        functools.partial(_norm_proj_kernel, n_chunk=n_chunk),
        out_shape=jax.ShapeDtypeStruct((t, n), BF16),
        grid=(t // ROW_TILE,),
        in_specs=[pl.BlockSpec((ROW_TILE, d), lambda i: (i, 0)),
                  pl.BlockSpec((1, d), lambda i: (0, 0)),
                  pl.BlockSpec((d, n), lambda i: (0, 0))],
        out_specs=pl.BlockSpec((ROW_TILE, n), lambda i: (i, 0)),
        compiler_params=pltpu.CompilerParams(dimension_semantics=("parallel",),
                                             vmem_limit_bytes=limit),
        name=name,
    )(x, g, w)


def _proj_post_kernel(a_ref, w_ref, x_ref, g_ref, o_ref):
    m = _dot(a_ref[...], w_ref[...])
    o_ref[...] = x_ref[...] + _rms_scale(m) * g_ref[...]


def _proj_post(a, w, x, g, name):
    t, k = a.shape
    d = w.shape[1]
    limit = _vmem_limit(
        _nbytes((ROW_TILE, k), BF16) + _nbytes((k, d), BF16) + 2 * _nbytes((ROW_TILE, d), F32),
        2 * _nbytes((ROW_TILE, d), F32))
    return pl.pallas_call(
        _proj_post_kernel,
        out_shape=jax.ShapeDtypeStruct((t, d), F32),
        grid=(t // ROW_TILE,),
        in_specs=[pl.BlockSpec((ROW_TILE, k), lambda i: (i, 0)),
                  pl.BlockSpec((k, d), lambda i: (0, 0)),
                  pl.BlockSpec((ROW_TILE, d), lambda i: (i, 0)),
                  pl.BlockSpec((1, d), lambda i: (0, 0))],
        out_specs=pl.BlockSpec((ROW_TILE, d), lambda i: (i, 0)),
        compiler_params=pltpu.CompilerParams(dimension_semantics=("parallel",),
                                             vmem_limit_bytes=limit),
        name=name,
    )(a, w, x, g)


SB_UNDERFLOW = 105.0


def _sb_kernel(q_ref, k_ref, v_ref, o_ref):
    tq = q_ref.shape[1]
    n_pairs = q_ref.shape[2] // LANES
    qi = pl.program_id(2)
    lane = lax.broadcasted_iota(jnp.int32, (tq, LANES), 1)
    first = lane < HEAD_DIM
    q_heads = []
    for p in range(n_pairs):
        q2 = q_ref[0, :, p * LANES:(p + 1) * LANES] * jnp.asarray(HEAD_DIM ** -0.5, BF16)
        zero = jnp.zeros_like(q2)
        q_heads.append(jnp.where(first, q2, zero))
        q_heads.append(jnp.where(first, zero, q2))

    row = lax.broadcasted_iota(jnp.int32, (tq, tq), 0)
    col = lax.broadcasted_iota(jnp.int32, (tq, tq), 1)
    later = jnp.where(row > col, 1.0, 0.0).astype(BF16)
    causal = col < row

    def kv_block(kb):
        start = pl.multiple_of(kb * tq, tq)
        return k_ref[0, pl.ds(start, tq), :], v_ref[0, pl.ds(start, tq), :]

    def all_heads(kblk, vblk, st, mask):
        pair = lambda h: slice((h // 2) * LANES, (h // 2 + 1) * LANES)
        heads = range(len(q_heads))
        zs = [_dot_nt(q_heads[h], kblk[:, pair(h)]) for h in heads]
        mids = []
        for h in heads:
            z = zs[h]
            softplus = jnp.maximum(z, 0.0) + jnp.log(1.0 + jnp.exp(_neg_abs(z)))
            base = (z - softplus) - st[2 * h]
            if mask is not None:
                softplus = jnp.where(mask, softplus, 0.0)
            mids.append((softplus.astype(BF16), base, jnp.sum(softplus, axis=-1, keepdims=True)))
        betweens = [_dot(sp, later) for sp, _, _ in mids]
        weights = []
        for h in heads:
            a = jnp.exp(mids[h][1] - betweens[h])
            if mask is not None:
                a = jnp.where(mask, a, 0.0)
            weights.append(a.astype(BF16))
        out = []
        for h in heads:
            out.append(st[2 * h] + mids[h][2])
            out.append(st[2 * h + 1] + _dot(weights[h], vblk[:, pair(h)]))
        return tuple(out)

    def smallest_carry(st):
        return functools.reduce(jnp.minimum, [jnp.min(c) for c in st[0::2]])

    init = (jnp.zeros((tq, 1), F32), jnp.zeros((tq, LANES), F32)) * len(q_heads)
    st = all_heads(*kv_block(qi), init, causal)

    def more(carry):
        n, low, _ = carry
        return jnp.logical_and(n < qi, low < SB_UNDERFLOW)

    def body(carry):
        n, _, st = carry
        st = all_heads(*kv_block(qi - 1 - n), st, None)
        return n + 1, smallest_carry(st), st

    _, _, st = lax.while_loop(more, body, (jnp.int32(0), smallest_carry(st), st))
    for p in range(n_pairs):
        o_ref[0, :, p * LANES:(p + 1) * LANES] = jnp.where(
            first, st[4 * p + 1], st[4 * p + 3]).astype(o_ref.dtype)


SB_HEADS_PER_STEP = 4


def _sb_attention(qkv, batch, seq):
    width = SB_HEADS_PER_STEP * HEAD_DIM
    pairs = SB_HEADS // SB_HEADS_PER_STEP
    tq = ATT_TILE
    limit = _vmem_limit(
        2 * _nbytes((tq, width), BF16) + 2 * _nbytes((seq, width), BF16),
        12 * SB_HEADS_PER_STEP * _nbytes((tq, tq), F32))
    return pl.pallas_call(
        _sb_kernel,
        out_shape=jax.ShapeDtypeStruct((batch, seq, SB_HEADS * HEAD_DIM), BF16),
        grid=(batch, pairs, seq // tq),
        in_specs=[pl.BlockSpec((1, tq, width), lambda b, p, i: (b, i, p)),
                  pl.BlockSpec((1, seq, width), lambda b, p, i: (b, 0, pairs + p)),
                  pl.BlockSpec((1, seq, width), lambda b, p, i: (b, 0, 2 * pairs + p))],
        out_specs=pl.BlockSpec((1, tq, width), lambda b, p, i: (b, i, p)),
        compiler_params=pltpu.CompilerParams(
            dimension_semantics=("parallel", "parallel", "parallel"), vmem_limit_bytes=limit),
        name="sb_attention",
    )(qkv, qkv, qkv)


PERM_BLOCK = 256


def _dil_proj_kernel(x_ref, g_ref, w_ref, o_ref, xp_ref, *, dilation, n_chunk):
    seq = x_ref.shape[0]
    run = PERM_BLOCK // dilation
    length = seq // dilation
    if dilation > 1:
        row = lax.broadcasted_iota(jnp.int32, (PERM_BLOCK, PERM_BLOCK), 0)
        col = lax.broadcasted_iota(jnp.int32, (PERM_BLOCK, PERM_BLOCK), 1)
        src = (row & (run - 1)) * dilation + lax.shift_right_logical(row, run.bit_length() - 1)
        perm = jnp.where(col == src, 1.0, 0.0).astype(BF16)
    for blk in range(seq // PERM_BLOCK):
        rows = slice(blk * PERM_BLOCK, (blk + 1) * PERM_BLOCK)
        xn = (_rms_scale(x_ref[rows, :]) * g_ref[...]).astype(BF16)
        if dilation == 1:
            xp_ref[rows, :] = xn
        else:
            moved = _dot(perm, xn).astype(BF16)
            for c in range(dilation):
                dst = c * length + blk * run
                xp_ref[dst:dst + run, :] = moved[c * run:(c + 1) * run, :]
    for c in range(w_ref.shape[1] // n_chunk):
        sl = slice(c * n_chunk, (c + 1) * n_chunk)
        for m in range(seq // ROW_TILE):
            rows = slice(m * ROW_TILE, (m + 1) * ROW_TILE)
            o_ref[rows, sl] = _dot(xp_ref[rows, :], w_ref[:, sl]).astype(o_ref.dtype)


def _dil_proj(x, g, w, dilation, batch, seq, name):
    d = x.shape[1]
    n = w.shape[1]
    n_chunk = 512
    assert seq % PERM_BLOCK == 0 and PERM_BLOCK % (16 * dilation) == 0 and n % n_chunk == 0
    limit = _vmem_limit(
        _nbytes((seq, d), F32) + _nbytes((d, n), BF16) + _nbytes((seq, n), BF16),
        _nbytes((seq, d), BF16) + 4 * _nbytes((ROW_TILE, d), F32))
    return pl.pallas_call(
        functools.partial(_dil_proj_kernel, dilation=dilation, n_chunk=n_chunk),
        out_shape=jax.ShapeDtypeStruct((batch * seq, n), BF16),
        grid=(batch,),
        in_specs=[pl.BlockSpec((seq, d), lambda b: (b, 0)),
                  pl.BlockSpec((1, d), lambda b: (0, 0)),
                  pl.BlockSpec((d, n), lambda b: (0, 0))],
        out_specs=pl.BlockSpec((seq, n), lambda b: (b, 0)),
        scratch_shapes=[pltpu.VMEM((seq, d), BF16)],
        compiler_params=pltpu.CompilerParams(dimension_semantics=("parallel",),
                                             vmem_limit_bytes=limit),
        name=name,
    )(x, g, w)


def _dil_kernel(q_ref, kp_ref, kc_ref, vp_ref, vc_ref, bias_ref, o_ref, lse_ref):
    w = q_ref.shape[1]
    ut = pl.program_id(2)
    lane = lax.broadcasted_iota(jnp.int32, (w, LANES), 1)
    first = lane < HEAD_DIM
    col = lax.broadcasted_iota(jnp.int32, (w, 2 * w), 1)
    key_ok = jnp.logical_or(col >= w, ut > 0)
    for p in range(q_ref.shape[2] // LANES):
        sl = slice(p * LANES, (p + 1) * LANES)
        q2 = q_ref[0, :, sl] * jnp.asarray(HEAD_DIM ** -0.5, BF16)
        zero = jnp.zeros_like(q2)
        k2 = jnp.concatenate([kp_ref[0, :, sl], kc_ref[0, :, sl]], axis=0)
        v2 = jnp.concatenate([vp_ref[0, :, sl], vc_ref[0, :, sl]], axis=0)
        outs, lses = [], []
        for h, qh in enumerate((jnp.where(first, q2, zero), jnp.where(first, zero, q2))):
            s = _dot_nt(qh, k2) + bias_ref[2 * p + h]
            s = jnp.where(key_ok, s, NEG)
            m = jnp.max(s, axis=-1, keepdims=True)
            e = jnp.exp(s - m)
            den = jnp.sum(e, axis=-1, keepdims=True)
            outs.append(_dot(e.astype(BF16), v2) / den)
            lses.append(m + jnp.log(den))
        o_ref[0, :, sl] = jnp.where(first, outs[0], outs[1]).astype(o_ref.dtype)
        lse_ref[0, :, sl] = jnp.where(first, lses[0], lses[1])


def _dil_bias(rel_bias, group, dilation, n_back):
    w = n_back
    dist = jnp.arange(w + 1) * dilation
    max_exact = N_BUCKETS // 2
    d = jnp.maximum(dist.astype(F32), 1.0)
    large = max_exact + (jnp.log(d / max_exact) / math.log(BUCKET_MAX_DIST / max_exact)
                         * (N_BUCKETS - max_exact)).astype(jnp.int32)
    bucket = jnp.where(dist < max_exact, dist, jnp.minimum(large, N_BUCKETS - 1))
    per_m = rel_bias[:, group * DIL_HEADS:(group + 1) * DIL_HEADS][bucket].astype(F32).T
    period = jnp.concatenate(
        [per_m[:, ::-1], jnp.full((DIL_HEADS, w), NEG, F32)], axis=1)
    tiled = jnp.tile(period, (1, w))[:, :w * 2 * w]
    return tiled.reshape(DIL_HEADS, w, 2 * w)


def _dil_group_attention(qkv, bias, group, dilation, batch, seq):
    width = DIL_HEADS * HEAD_DIM
    length = seq // dilation
    w = DIL_TILE
    qv = qkv.reshape(batch * dilation, length, 3 * width)

    def cur(which):
        return pl.BlockSpec((1, w, width), lambda b, c, u: (b * dilation + c, u, which))

    def prev(which):
        return pl.BlockSpec((1, w, width),
                            lambda b, c, u: (b * dilation + c, jnp.maximum(u - 1, 0), which))

    out_spec = pl.BlockSpec((1, w, width), lambda b, c, u: (b, u, c))
    limit = _vmem_limit(
        5 * _nbytes((w, width), BF16) + _nbytes((DIL_HEADS, w, 2 * w), F32)
        + _nbytes((w, width), BF16) + _nbytes((w, width), F32),
        32 * _nbytes((w, 2 * w), F32))
    o, lse = pl.pallas_call(
        _dil_kernel,
        out_shape=(jax.ShapeDtypeStruct((batch, length, dilation * width), BF16),
                   jax.ShapeDtypeStruct((batch, length, dilation * width), F32)),
        grid=(batch, dilation, length // w),
        in_specs=[cur(0), prev(1), cur(1), prev(2), cur(2),
                  pl.BlockSpec((DIL_HEADS, w, 2 * w), lambda b, c, u: (0, 0, 0))],
        out_specs=(out_spec, out_spec),
        compiler_params=pltpu.CompilerParams(
            dimension_semantics=("parallel", "parallel", "parallel"), vmem_limit_bytes=limit),
        name=f"dil_attention_g{group}",
    )(qv, qv, qv, qv, qv, bias)
    return o.reshape(batch * seq, width), lse.reshape(batch * seq, width)


def _dil_merge_kernel(o0_ref, o1_ref, o2_ref, l0_ref, l1_ref, l2_ref, w_ref, x_ref, g_ref, out_ref):
    l0, l1, l2 = l0_ref[...], l1_ref[...], l2_ref[...]
    m = jnp.maximum(jnp.maximum(l0, l1), l2)
    e0, e1, e2 = jnp.exp(l0 - m), jnp.exp(l1 - m), jnp.exp(l2 - m)
    inv = 1.0 / (e0 + e1 + e2)
    o = (o0_ref[...].astype(F32) * (e0 * inv) + o1_ref[...].astype(F32) * (e1 * inv)
         + o2_ref[...].astype(F32) * (e2 * inv))
    mo = _dot(o.astype(BF16), w_ref[...])
    out_ref[...] = x_ref[...] + _rms_scale(mo) * g_ref[...]


def _dil_merge_proj(outs, lses, w, x, g):
    t, k = outs[0].shape
    d = w.shape[1]
    row = lambda i: (i, 0)
    limit = _vmem_limit(
        3 * _nbytes((ROW_TILE, k), BF16) + 3 * _nbytes((ROW_TILE, k), F32)
        + _nbytes((k, d), BF16) + 2 * _nbytes((ROW_TILE, d), F32),
        8 * _nbytes((ROW_TILE, k), F32) + 2 * _nbytes((ROW_TILE, d), F32))
    return pl.pallas_call(
        _dil_merge_kernel,
        out_shape=jax.ShapeDtypeStruct((t, d), F32),
        grid=(t // ROW_TILE,),
        in_specs=[pl.BlockSpec((ROW_TILE, k), row)] * 6
        + [pl.BlockSpec((k, d), lambda i: (0, 0)),
           pl.BlockSpec((ROW_TILE, d), row),
           pl.BlockSpec((1, d), lambda i: (0, 0))],
        out_specs=pl.BlockSpec((ROW_TILE, d), row),
        compiler_params=pltpu.CompilerParams(dimension_semantics=("parallel",),
                                             vmem_limit_bytes=limit),
        name="dil_merge_proj",
    )(*outs, *lses, w, x, g)


def _mla_pre_kernel(x_ref, g_ref, wcq_ref, wckv_ref, wkr_ref, wkr_rot_ref, qn_ref, kvn_ref,
                    wq_ref, wq_rot_ref, wk_ref, wv_ref, cos_ref, sin_ref,
                    q_ref, k_ref, v_ref, *, q_scale):
    xn = (_rms_scale(x_ref[...]) * g_ref[...]).astype(BF16)
    cq = (_rms_scale(_dot(xn, wcq_ref[...])) * qn_ref[...]).astype(BF16)
    ckv = (_rms_scale(_dot(xn, wckv_ref[...])) * kvn_ref[...]).astype(BF16)
    cos, sin = cos_ref[...], sin_ref[...]
    k_rope = _dot(xn, wkr_ref[...]) * cos + _dot(xn, wkr_rot_ref[...]) * sin
    for h in range(MLA_HEADS):
        sl = slice(h * LANES, (h + 1) * LANES)
        q = _dot(cq, wq_ref[:, sl]) * cos + _dot(cq, wq_rot_ref[:, sl]) * sin
        q_ref[:, sl] = (q * q_scale).astype(q_ref.dtype)
        k_ref[:, sl] = (_dot(ckv, wk_ref[:, sl]) + k_rope).astype(k_ref.dtype)
    v_ref[...] = _dot(ckv, wv_ref[...]).astype(v_ref.dtype)


def _rope_rotation(w):
    half = MLA_ROPE // 2
    return jnp.concatenate([-w[..., half:], w[..., :half]], axis=-1)


def _pad_head_slabs(nope, rope):
    k, h = nope.shape[0], nope.shape[1]
    pad = jnp.zeros((k, h, LANES - MLA_NOPE - MLA_ROPE), nope.dtype)
    return jnp.concatenate([nope, rope, pad], axis=-1).reshape(k, h * LANES)


def _mla_pre(x, g, positions, w_in, q_norm, w_qb, kv_norm, w_kvb):
    t, d = x.shape
    h = MLA_HEADS
    w_cq = w_in[:, :MLA_Q_RANK].astype(BF16)
    w_ckv = w_in[:, MLA_Q_RANK:MLA_Q_RANK + MLA_KV_RANK].astype(BF16)
    w_kr = w_in[:, MLA_Q_RANK + MLA_KV_RANK:]
    zeros_nope = jnp.zeros((d, 1, MLA_NOPE), F32)
    w_kr_pad = _pad_head_slabs(zeros_nope, w_kr[:, None, :]).astype(BF16)
    w_kr_rot = _pad_head_slabs(zeros_nope, _rope_rotation(w_kr)[:, None, :]).astype(BF16)

    wq = w_qb.reshape(MLA_Q_RANK, h, MLA_NOPE + MLA_ROPE)
    wq_pad = _pad_head_slabs(wq[..., :MLA_NOPE], wq[..., MLA_NOPE:]).astype(BF16)
    wq_rot = _pad_head_slabs(jnp.zeros_like(wq[..., :MLA_NOPE]),
                             _rope_rotation(wq[..., MLA_NOPE:])).astype(BF16)
    wkv = w_kvb.reshape(MLA_KV_RANK, h, MLA_NOPE + MLA_V)
    wk_pad = _pad_head_slabs(wkv[..., :MLA_NOPE],
                             jnp.zeros((MLA_KV_RANK, h, MLA_ROPE), F32)).astype(BF16)
    wv = wkv[..., MLA_NOPE:].reshape(MLA_KV_RANK, h * MLA_V).astype(BF16)

    half = MLA_ROPE // 2
    freqs = ROPE_THETA ** (-jnp.arange(half, dtype=F32) / half)
    ang = positions.astype(F32).reshape(t, 1) * freqs
    cos, sin = jnp.cos(ang), jnp.sin(ang)
    tail = jnp.zeros((t, LANES - MLA_NOPE - MLA_ROPE), F32)
    cos_tab = jnp.concatenate([jnp.ones((t, MLA_NOPE), F32), cos, cos, tail], axis=-1)
    sin_tab = jnp.concatenate([jnp.zeros((t, MLA_NOPE), F32), sin, sin, tail], axis=-1)

    tm = ROW_TILE // 2
    full = lambda a: pl.BlockSpec(a.shape, lambda i: (0,) * a.ndim)
    row = lambda n: pl.BlockSpec((tm, n), lambda i: (i, 0))
    weights = (w_cq, w_ckv, w_kr_pad, w_kr_rot, q_norm.reshape(1, -1), kv_norm.reshape(1, -1),
               wq_pad, wq_rot, wk_pad, wv)
    limit = _vmem_limit(
        _nbytes((tm, d), F32) + sum(_nbytes(a.shape, a.dtype) for a in weights)
        + 2 * _nbytes((tm, LANES), F32) + 2 * _nbytes((tm, h * LANES), BF16)
        + _nbytes((tm, h * MLA_V), BF16),
        4 * _nbytes((tm, d), F32))
    return pl.pallas_call(
        functools.partial(_mla_pre_kernel, q_scale=(MLA_NOPE + MLA_ROPE) ** -0.5),
        out_shape=(jax.ShapeDtypeStruct((t, h * LANES), BF16),
                   jax.ShapeDtypeStruct((t, h * LANES), BF16),
                   jax.ShapeDtypeStruct((t, h * MLA_V), BF16)),
        grid=(t // tm,),
        in_specs=[row(d), full(g)] + [full(a) for a in weights] + [row(LANES), row(LANES)],
        out_specs=(row(h * LANES), row(h * LANES), row(h * MLA_V)),
        compiler_params=pltpu.CompilerParams(dimension_semantics=("parallel",),
                                             vmem_limit_bytes=limit),
        name="mla_pre",
    )(x, g, *weights, cos_tab, sin_tab)


def _mla_kernel(q_ref, k_ref, v_ref, o_ref):
    tq = q_ref.shape[1]
    qi = pl.program_id(2)
    lane = lax.broadcasted_iota(jnp.int32, (tq, LANES), 1)
    first = lane < MLA_V
    row = lax.broadcasted_iota(jnp.int32, (tq, tq), 0)
    col = lax.broadcasted_iota(jnp.int32, (tq, tq), 1)
    causal = col <= row
    n_heads = q_ref.shape[2] // LANES
    heads = range(n_heads)
    q_heads = [q_ref[0, :, h * LANES:(h + 1) * LANES] for h in heads]

    def kv_block(kb):
        start = pl.multiple_of(kb * tq, tq)
        return k_ref[0, pl.ds(start, tq), :], v_ref[0, pl.ds(start, tq), :]

    def all_heads(kblk, vblk, st, mask):
        ss = [_dot_nt(q_heads[h], kblk[:, h * LANES:(h + 1) * LANES]) for h in heads]
        probs, stats = [], []
        for h in heads:
            s, (m, den) = ss[h], st[3 * h:3 * h + 2]
            if mask is not None:
                s = jnp.where(mask, s, NEG)
            m_new = jnp.maximum(m, jnp.max(s, axis=-1, keepdims=True))
            alpha = jnp.exp(m - m_new)
            e = jnp.exp(s - m_new)
            stats.append((m_new, alpha * den + jnp.sum(e, axis=-1, keepdims=True), alpha))
            probs.append(e.astype(BF16))
        out = []
        for h in heads:
            pv = _dot(probs[h], vblk[:, (h // 2) * LANES:(h // 2 + 1) * LANES])
            out.extend((stats[h][0], stats[h][1], stats[h][2] * st[3 * h + 2] + pv))
        return tuple(out)

    init = (jnp.full((tq, 1), NEG, F32), jnp.zeros((tq, 1), F32),
            jnp.zeros((tq, LANES), F32)) * n_heads
    st = all_heads(*kv_block(qi), init, causal)
    st = lax.fori_loop(0, qi, lambda n, st: all_heads(*kv_block(qi - 1 - n), st, None), st)
    for p in range(n_heads // 2):
        a, b = 2 * p, 2 * p + 1
        o_ref[0, :, p * LANES:(p + 1) * LANES] = jnp.where(
            first, st[3 * a + 2] / st[3 * a + 1], st[3 * b + 2] / st[3 * b + 1]).astype(o_ref.dtype)


MLA_HEADS_PER_STEP = 4


def _mla_attention(q, k, v, batch, seq):
    nh = MLA_HEADS_PER_STEP
    steps = MLA_HEADS // nh
    tq = ATT_TILE
    limit = _vmem_limit(
        _nbytes((tq, nh * LANES), BF16) + _nbytes((seq, nh * LANES), BF16)
        + _nbytes((seq, nh * MLA_V), BF16) + _nbytes((tq, nh * MLA_V), BF16),
        8 * nh * _nbytes((tq, tq), F32))
    return pl.pallas_call(
        _mla_kernel,
        out_shape=jax.ShapeDtypeStruct((batch, seq, MLA_HEADS * MLA_V), BF16),
        grid=(batch, steps, seq // tq),
        in_specs=[pl.BlockSpec((1, tq, nh * LANES), lambda b, p, i: (b, i, p)),
                  pl.BlockSpec((1, seq, nh * LANES), lambda b, p, i: (b, 0, p)),
                  pl.BlockSpec((1, seq, nh * MLA_V), lambda b, p, i: (b, 0, p))],
        out_specs=pl.BlockSpec((1, tq, nh * MLA_V), lambda b, p, i: (b, i, p)),
        compiler_params=pltpu.CompilerParams(
            dimension_semantics=("parallel", "parallel", "parallel"), vmem_limit_bytes=limit),
        name="mla_attention",
    )(q, k, v)


def _ffn_up_kernel(x_ref, g_ref, wg_ref, wu_ref, cwg_ref, cwu_ref, cbg_ref, cbu_ref, o_ref,
                   tail_g_ref, tail_u_ref, *, tiles_per_seq):
    tm = x_ref.shape[0]

    @pl.when(pl.program_id(0) % tiles_per_seq == 0)
    def _():
        tail_g_ref[...] = jnp.zeros_like(tail_g_ref)
        tail_u_ref[...] = jnp.zeros_like(tail_u_ref)

    xn = (_rms_scale(x_ref[...]) * g_ref[...]).astype(BF16)
    row = lax.broadcasted_iota(jnp.int32, (tm, FF_CHUNK), 0)

    def conv(h, tail_ref, cw_ref, cb_ref, sl):
        p1 = tail_ref[7:8, sl]
        p2 = tail_ref[6:7, sl]
        h1 = jnp.where(row == 0, p1, pltpu.roll(h, 1, 0))
        h2 = jnp.where(row == 0, p2, jnp.where(row == 1, p1, pltpu.roll(h, 2, 0)))
        tail_ref[:, sl] = h[tm - 8:, :]
        return cw_ref[0:1, sl] * h2 + cw_ref[1:2, sl] * h1 + cw_ref[2:3, sl] * h + cb_ref[:, sl]

    for c in range(o_ref.shape[1] // FF_CHUNK):
        sl = slice(c * FF_CHUNK, (c + 1) * FF_CHUNK)
        gate = conv(_dot(xn, wg_ref[:, sl]), tail_g_ref, cwg_ref, cbg_ref, sl)
        val = conv(_dot(xn, wu_ref[:, sl]), tail_u_ref, cwu_ref, cbu_ref, sl)
        act = gate * (1.0 / (1.0 + jnp.exp(-gate))) * val
        o_ref[:, sl] = act.astype(o_ref.dtype)


def _ffn_up(x, g, w_up, conv_w, conv_b, seq):
    t, d = x.shape
    wg, wu = w_up[:, :D_FF].astype(BF16), w_up[:, D_FF:].astype(BF16)
    cwg, cwu = conv_w[:, :D_FF], conv_w[:, D_FF:]
    cbg, cbu = conv_b[:D_FF].reshape(1, D_FF), conv_b[D_FF:].reshape(1, D_FF)
    assert seq % ROW_TILE == 0 and D_FF % FF_CHUNK == 0
    full = lambda a: pl.BlockSpec(a.shape, lambda i: (0,) * a.ndim)
    limit = _vmem_limit(
        _nbytes((ROW_TILE, d), F32) + 2 * _nbytes((d, D_FF), BF16)
        + _nbytes((ROW_TILE, D_FF), BF16) + 8 * _nbytes((1, D_FF), F32),
        2 * _nbytes((8, D_FF), F32) + _nbytes((ROW_TILE, d), F32)
        + 12 * _nbytes((ROW_TILE, FF_CHUNK), F32))
    return pl.pallas_call(
        functools.partial(_ffn_up_kernel, tiles_per_seq=seq // ROW_TILE),
        out_shape=jax.ShapeDtypeStruct((t, D_FF), BF16),
        grid=(t // ROW_TILE,),
        in_specs=[pl.BlockSpec((ROW_TILE, d), lambda i: (i, 0)), full(g), full(wg), full(wu),
                  full(cwg), full(cwu), full(cbg), full(cbu)],
        out_specs=pl.BlockSpec((ROW_TILE, D_FF), lambda i: (i, 0)),
        scratch_shapes=[pltpu.VMEM((8, D_FF), F32), pltpu.VMEM((8, D_FF), F32)],
        compiler_params=pltpu.CompilerParams(dimension_semantics=("arbitrary",),
                                             vmem_limit_bytes=limit),
        name="ffn_up_conv_gate",
    )(x, g, wg, wu, cwg, cwu, cbg, cbu)


def kernel(x, positions, rel_bias, norm_gains, sb_w_qkv, sb_w_o, dil_w_qkv, dil_w_o, mla_w_in,
           mla_q_norm, mla_w_qb, mla_kv_norm, mla_w_kvb, mla_w_o, ffn_w_up, ffn_conv_w,
           ffn_conv_b, ffn_w_down):
    batch, seq, d = x.shape
    t = batch * seq
    x = x.reshape(t, d)
    for i in range(DEPTH):
        kind, j = i % N_MIXERS, i // N_MIXERS
        gain = lambda n: norm_gains[i, n].reshape(1, d)
        if kind == 0:
            qkv = _norm_proj(x, gain(0), sb_w_qkv[j].astype(BF16), "sb_qkv_proj")
            o = _sb_attention(qkv.reshape(batch, seq, -1), batch, seq).reshape(t, -1)
            x = _proj_post(o, sb_w_o[j].astype(BF16), x, gain(1), "sb_out_proj")
        elif kind == 1:
            width = DIL_HEADS * HEAD_DIM
            w_groups = dil_w_qkv[j].reshape(d, 3, len(DIL_GROUPS), width).astype(BF16)
            outs, lses = [], []
            for grp, (window, dilation) in enumerate(DIL_GROUPS):
                n_back = window // dilation
                assert n_back == DIL_TILE and (seq // dilation) % DIL_TILE == 0
                qkv = _dil_proj(x, gain(0), w_groups[:, :, grp].reshape(d, 3 * width), dilation,
                                batch, seq, f"dil_qkv_proj_g{grp}")
                bias = _dil_bias(rel_bias, grp, dilation, n_back)
                o, lse = _dil_group_attention(qkv, bias, grp, dilation, batch, seq)
                outs.append(o)
                lses.append(lse)
            x = _dil_merge_proj(outs, lses, dil_w_o[j].astype(BF16), x, gain(1))
        else:
            q, k, v = _mla_pre(x, gain(0), positions, mla_w_in[j], mla_q_norm[j], mla_w_qb[j],
                               mla_kv_norm[j], mla_w_kvb[j])
            o = _mla_attention(q.reshape(batch, seq, -1), k.reshape(batch, seq, -1),
                               v.reshape(batch, seq, -1), batch, seq).reshape(t, -1)
            x = _proj_post(o, mla_w_o[j].astype(BF16), x, gain(1), "mla_out_proj")
        act = _ffn_up(x, gain(2), ffn_w_up[i], ffn_conv_w[i], ffn_conv_b[i], seq)
        x = _proj_post(act, ffn_w_down[i].astype(BF16), x, gain(3), "ffn_down_proj")
    return x.reshape(batch, seq, d)
```

```python
import functools
import math

import jax
import jax.numpy as jnp
from jax import lax
from jax.experimental import pallas as pl
from jax.experimental.pallas import tpu as pltpu

F32 = jnp.float32
BF16 = jnp.bfloat16

D_MODEL = 1024
DEPTH = 4
N_MIXERS = 3
EPS = 1e-6
NEG = -1e30

SB_HEADS = 16
HEAD_DIM = 64

DIL_GROUPS = ((128, 1), (512, 4), (2048, 16))
DIL_HEADS = 8
N_BUCKETS = 32
BUCKET_MAX_DIST = 2048

MLA_HEADS = 16
MLA_Q_RANK = 384
MLA_KV_RANK = 256
MLA_NOPE = 64
MLA_ROPE = 32
MLA_V = 64
ROPE_THETA = 10000.0

D_FF = 2816

LANES = 128
V7X_VMEM_CAP_BYTES = 56 * 1024 * 1024

ROW_TILE = 512
ATT_TILE = 256
DIL_TILE = 128
FF_CHUNK = 256


def _vmem_limit(pipelined_bytes, resident_bytes):
    return int(min(V7X_VMEM_CAP_BYTES, 2 * pipelined_bytes + resident_bytes))


def _nbytes(shape, dtype):
    return math.prod(shape) * jnp.dtype(dtype).itemsize


def _rms_scale(x):
    return x * lax.rsqrt(jnp.mean(x * x, axis=-1, keepdims=True) + EPS)


def _dot(a, b):
    return jnp.dot(a, b, preferred_element_type=F32)


def _neg_abs(x):
    bits = lax.bitcast_convert_type(x, jnp.uint32) | jnp.uint32(0x80000000)
    return lax.bitcast_convert_type(bits, F32)


def _dot_nt(a, b):
    return lax.dot_general(a, b, (((1,), (1,)), ((), ())), preferred_element_type=F32)


def _proj_post_kernel(a_ref, w_ref, x_ref, g_ref, o_ref):
    m = _dot(a_ref[...], w_ref[...])
    o_ref[...] = x_ref[...] + _rms_scale(m) * g_ref[...]


def _proj_post(a, w, x, g, name):
    t, k = a.shape
    d = w.shape[1]
    limit = _vmem_limit(
        _nbytes((ROW_TILE, k), BF16) + _nbytes((k, d), BF16) + 2 * _nbytes((ROW_TILE, d), F32),
        2 * _nbytes((ROW_TILE, d), F32))
    return pl.pallas_call(
        _proj_post_kernel,
        out_shape=jax.ShapeDtypeStruct((t, d), F32),
        grid=(t // ROW_TILE,),
        in_specs=[pl.BlockSpec((ROW_TILE, k), lambda i: (i, 0)),
                  pl.BlockSpec((k, d), lambda i: (0, 0)),
                  pl.BlockSpec((ROW_TILE, d), lambda i: (i, 0)),
                  pl.BlockSpec((1, d), lambda i: (0, 0))],
        out_specs=pl.BlockSpec((ROW_TILE, d), lambda i: (i, 0)),
        compiler_params=pltpu.CompilerParams(dimension_semantics=("parallel",),
                                             vmem_limit_bytes=limit),
        name=name,
    )(a, w, x, g)


def _sb_proj_kernel(x_ref, g_ref, wqk_ref, wvt_ref, qk_ref, vt_ref, *, n_chunk):
    xn = (_rms_scale(x_ref[...]) * g_ref[...]).astype(BF16)
    for c in range(qk_ref.shape[1] // n_chunk):
        sl = slice(c * n_chunk, (c + 1) * n_chunk)
        qk_ref[:, sl] = _dot(xn, wqk_ref[:, sl]).astype(qk_ref.dtype)
    tk = vt_ref.shape[2]
    for c in range(wvt_ref.shape[0] // n_chunk):
        sl = slice(c * n_chunk, (c + 1) * n_chunk)
        vt = _dot_nt(wvt_ref[sl, :], xn).astype(vt_ref.dtype)
        for j in range(vt_ref.shape[0]):
            vt_ref[j, sl, :] = vt[:, j * tk:(j + 1) * tk]


def _sb_proj(x, g, w_qkv):
    t, d = x.shape
    width = SB_HEADS * HEAD_DIM
    wqk = w_qkv[:, :2 * width].astype(BF16)
    wvt = w_qkv[:, 2 * width:].T.astype(BF16)
    n_chunk = 512
    tk = ATT_TILE
    assert t % ROW_TILE == 0 and ROW_TILE % tk == 0 and width % n_chunk == 0
    limit = _vmem_limit(
        _nbytes((ROW_TILE, d), F32) + _nbytes((d, 3 * width), BF16)
        + _nbytes((ROW_TILE, 3 * width), BF16),
        _nbytes((ROW_TILE, d), F32) + 2 * _nbytes((ROW_TILE, n_chunk), F32))
    return pl.pallas_call(
        functools.partial(_sb_proj_kernel, n_chunk=n_chunk),
        out_shape=(jax.ShapeDtypeStruct((t, 2 * width), BF16),
                   jax.ShapeDtypeStruct((t // tk, width, tk), BF16)),
        grid=(t // ROW_TILE,),
        in_specs=[pl.BlockSpec((ROW_TILE, d), lambda i: (i, 0)),
                  pl.BlockSpec((1, d), lambda i: (0, 0)),
                  pl.BlockSpec((d, 2 * width), lambda i: (0, 0)),
                  pl.BlockSpec((width, d), lambda i: (0, 0))],
        out_specs=(pl.BlockSpec((ROW_TILE, 2 * width), lambda i: (i, 0)),
                   pl.BlockSpec((ROW_TILE // tk, width, tk), lambda i: (i, 0, 0))),
        compiler_params=pltpu.CompilerParams(dimension_semantics=("parallel",),
                                             vmem_limit_bytes=limit),
        name="sb_qkv_proj",
    )(x, g, wqk, wvt)


SB_UNDERFLOW = 105.0
SB_HEADS_PER_STEP = 8


def _sb_kernel(q_ref, k_ref, vt_ref, o_ref):
    tq = q_ref.shape[1]
    n_pairs = q_ref.shape[2] // LANES
    qi = pl.program_id(2)
    lane = lax.broadcasted_iota(jnp.int32, (tq, LANES), 1)
    first = lane < HEAD_DIM
    q_heads = []
    for p in range(n_pairs):
        q2 = q_ref[0, :, p * LANES:(p + 1) * LANES] * jnp.asarray(HEAD_DIM ** -0.5, BF16)
        zero = jnp.zeros_like(q2)
        q_heads.append(jnp.where(first, q2, zero))
        q_heads.append(jnp.where(first, zero, q2))
    heads = range(len(q_heads))

    key = lax.broadcasted_iota(jnp.int32, (tq, tq), 0)
    other = lax.broadcasted_iota(jnp.int32, (tq, tq), 1)
    later = jnp.where(other > key, 1.0, 0.0).astype(BF16)
    causal = key < other

    def all_heads(kb, st, mask):
        kblk = k_ref[0, pl.ds(pl.multiple_of(kb * tq, tq), tq), :]
        vt = vt_ref[kb]
        pair = lambda h: slice((h // 2) * LANES, (h // 2 + 1) * LANES)
        zs = [_dot_nt(kblk[:, pair(h)], q_heads[h]) for h in heads]
        mids = []
        for h in heads:
            z = zs[h]
            softplus = jnp.maximum(z, 0.0) + jnp.log(1.0 + jnp.exp(_neg_abs(z)))
            base = (z - softplus) - st[2 * h]
            if mask is not None:
                softplus = jnp.where(mask, softplus, 0.0)
            mids.append((softplus.astype(BF16), base, jnp.sum(softplus, axis=0, keepdims=True)))
        betweens = [_dot(later, sp) for sp, _, _ in mids]
        weights = []
        for h in heads:
            a = jnp.exp(mids[h][1] - betweens[h])
            if mask is not None:
                a = jnp.where(mask, a, 0.0)
            weights.append(a.astype(BF16))
        out = []
        for h in heads:
            out.append(st[2 * h] + mids[h][2])
            out.append(st[2 * h + 1] + _dot(vt[h * HEAD_DIM:(h + 1) * HEAD_DIM, :], weights[h]))
        return tuple(out)

    def smallest_carry(st):
        return functools.reduce(jnp.minimum, [jnp.min(c) for c in st[0::2]])

    init = (jnp.zeros((1, tq), F32), jnp.zeros((HEAD_DIM, tq), F32)) * len(q_heads)
    st = all_heads(qi, init, causal)

    def more(carry):
        n, low, _ = carry
        return jnp.logical_and(n < qi, low < SB_UNDERFLOW)

    def body(carry):
        n, _, st = carry
        st = all_heads(qi - 1 - n, st, None)
        return n + 1, smallest_carry(st), st

    _, _, st = lax.while_loop(more, body, (jnp.int32(0), smallest_carry(st), st))
    for p in range(n_pairs):
        pair_t = jnp.concatenate([st[4 * p + 1], st[4 * p + 3]], axis=0)
        o_ref[0, :, p * LANES:(p + 1) * LANES] = pair_t.T.astype(o_ref.dtype)


def _sb_attention(qk, vt, batch, seq):
    width = SB_HEADS_PER_STEP * HEAD_DIM
    groups = SB_HEADS // SB_HEADS_PER_STEP
    tq = ATT_TILE
    nb = seq // tq
    limit = _vmem_limit(
        2 * _nbytes((tq, width), BF16) + 2 * _nbytes((seq, width), BF16),
        12 * SB_HEADS_PER_STEP * _nbytes((tq, tq), F32))
    return pl.pallas_call(
        _sb_kernel,
        out_shape=jax.ShapeDtypeStruct((batch, seq, SB_HEADS * HEAD_DIM), BF16),
        grid=(batch, groups, nb),
        in_specs=[pl.BlockSpec((1, tq, width), lambda b, p, i: (b, i, p)),
                  pl.BlockSpec((1, seq, width), lambda b, p, i: (b, 0, groups + p)),
                  pl.BlockSpec((nb, width, tq), lambda b, p, i: (b, p, 0))],
        out_specs=pl.BlockSpec((1, tq, width), lambda b, p, i: (b, i, p)),
        compiler_params=pltpu.CompilerParams(
            dimension_semantics=("parallel", "parallel", "parallel"), vmem_limit_bytes=limit),
        name="sb_attention",
    )(qk, qk, vt)


PERM_BLOCK = 256


def _dil_proj_kernel(x_ref, g_ref, w_ref, o_ref, xp_ref, *, dilation, n_chunk):
    seq = x_ref.shape[0]
    run = PERM_BLOCK // dilation
    length = seq // dilation
    if dilation > 1:
        row = lax.broadcasted_iota(jnp.int32, (PERM_BLOCK, PERM_BLOCK), 0)
        col = lax.broadcasted_iota(jnp.int32, (PERM_BLOCK, PERM_BLOCK), 1)
        src = (row & (run - 1)) * dilation + lax.shift_right_logical(row, run.bit_length() - 1)
        perm = jnp.where(col == src, 1.0, 0.0).astype(BF16)
    for blk in range(seq // PERM_BLOCK):
        rows = slice(blk * PERM_BLOCK, (blk + 1) * PERM_BLOCK)
        xn = (_rms_scale(x_ref[rows, :]) * g_ref[...]).astype(BF16)
        if dilation == 1:
            xp_ref[rows, :] = xn
        else:
            moved = _dot(perm, xn).astype(BF16)
            for c in range(dilation):
                dst = c * length + blk * run
                xp_ref[dst:dst + run, :] = moved[c * run:(c + 1) * run, :]
    for c in range(w_ref.shape[1] // n_chunk):
        sl = slice(c * n_chunk, (c + 1) * n_chunk)
        for m in range(seq // ROW_TILE):
            rows = slice(m * ROW_TILE, (m + 1) * ROW_TILE)
            o_ref[rows, sl] = _dot(xp_ref[rows, :], w_ref[:, sl]).astype(o_ref.dtype)


def _dil_proj(x, g, w, dilation, batch, seq, name):
    d = x.shape[1]
    n = w.shape[1]
    n_chunk = 512
    assert seq % PERM_BLOCK == 0 and PERM_BLOCK % (16 * dilation) == 0 and n % n_chunk == 0
    limit = _vmem_limit(
        _nbytes((seq, d), F32) + _nbytes((d, n), BF16) + _nbytes((seq, n), BF16),
        _nbytes((seq, d), BF16) + 4 * _nbytes((ROW_TILE, d), F32))
    return pl.pallas_call(
        functools.partial(_dil_proj_kernel, dilation=dilation, n_chunk=n_chunk),
        out_shape=jax.ShapeDtypeStruct((batch * seq, n), BF16),
        grid=(batch,),
        in_specs=[pl.BlockSpec((seq, d), lambda b: (b, 0)),
                  pl.BlockSpec((1, d), lambda b: (0, 0)),
                  pl.BlockSpec((d, n), lambda b: (0, 0))],
        out_specs=pl.BlockSpec((seq, n), lambda b: (b, 0)),
        scratch_shapes=[pltpu.VMEM((seq, d), BF16)],
        compiler_params=pltpu.CompilerParams(dimension_semantics=("parallel",),
                                             vmem_limit_bytes=limit),
        name=name,
    )(x, g, w)


def _dil_kernel(q_ref, kp_ref, kc_ref, vp_ref, vc_ref, bias_ref, o_ref, lse_ref):
    w = q_ref.shape[1]
    ut = pl.program_id(2)
    lane = lax.broadcasted_iota(jnp.int32, (w, LANES), 1)
    first = lane < HEAD_DIM
    col = lax.broadcasted_iota(jnp.int32, (w, 2 * w), 1)
    key_ok = jnp.logical_or(col >= w, ut > 0)
    n_pairs = q_ref.shape[2] // LANES
    slab = lambda p: slice(p * LANES, (p + 1) * LANES)
    q_heads, k2, v2 = [], [], []
    for p in range(n_pairs):
        q2 = q_ref[0, :, slab(p)] * jnp.asarray(HEAD_DIM ** -0.5, BF16)
        zero = jnp.zeros_like(q2)
        q_heads += [jnp.where(first, q2, zero), jnp.where(first, zero, q2)]
        k2.append(jnp.concatenate([kp_ref[0, :, slab(p)], kc_ref[0, :, slab(p)]], axis=0))
        v2.append(jnp.concatenate([vp_ref[0, :, slab(p)], vc_ref[0, :, slab(p)]], axis=0))
    heads = range(2 * n_pairs)
    scores = [_dot_nt(q_heads[h], k2[h // 2]) for h in heads]
    probs, dens, lses = [], [], []
    for h in heads:
        s = jnp.where(key_ok, scores[h] + bias_ref[h], NEG)
        m = jnp.max(s, axis=-1, keepdims=True)
        e = jnp.exp(s - m)
        den = jnp.sum(e, axis=-1, keepdims=True)
        probs.append(e.astype(BF16))
        dens.append(den)
        lses.append(m + jnp.log(den))
    outs = [_dot(probs[h], v2[h // 2]) / dens[h] for h in heads]
    for p in range(n_pairs):
        o_ref[0, :, slab(p)] = jnp.where(first, outs[2 * p], outs[2 * p + 1]).astype(o_ref.dtype)
        lse_ref[0, :, slab(p)] = jnp.where(first, lses[2 * p], lses[2 * p + 1])


def _dil_bias(rel_bias, group, dilation, n_back):
    w = n_back
    dist = jnp.arange(w + 1) * dilation
    max_exact = N_BUCKETS // 2
    d = jnp.maximum(dist.astype(F32), 1.0)
    large = max_exact + (jnp.log(d / max_exact) / math.log(BUCKET_MAX_DIST / max_exact)
                         * (N_BUCKETS - max_exact)).astype(jnp.int32)
    bucket = jnp.where(dist < max_exact, dist, jnp.minimum(large, N_BUCKETS - 1))
    per_m = rel_bias[:, group * DIL_HEADS:(group + 1) * DIL_HEADS][bucket].astype(F32).T
    period = jnp.concatenate(
        [per_m[:, ::-1], jnp.full((DIL_HEADS, w), NEG, F32)], axis=1)
    tiled = jnp.tile(period, (1, w))[:, :w * 2 * w]
    return tiled.reshape(DIL_HEADS, w, 2 * w)


def _dil_group_attention(qkv, bias, group, dilation, batch, seq):
    width = DIL_HEADS * HEAD_DIM
    length = seq // dilation
    w = DIL_TILE
    qv = qkv.reshape(batch * dilation, length, 3 * width)

    def cur(which):
        return pl.BlockSpec((1, w, width), lambda b, c, u: (b * dilation + c, u, which))

    def prev(which):
        return pl.BlockSpec((1, w, width),
                            lambda b, c, u: (b * dilation + c, jnp.maximum(u - 1, 0), which))

    out_spec = pl.BlockSpec((1, w, width), lambda b, c, u: (b, u, c))
    limit = _vmem_limit(
        5 * _nbytes((w, width), BF16) + _nbytes((DIL_HEADS, w, 2 * w), F32)
        + _nbytes((w, width), BF16) + _nbytes((w, width), F32),
        32 * _nbytes((w, 2 * w), F32))
    o, lse = pl.pallas_call(
        _dil_kernel,
        out_shape=(jax.ShapeDtypeStruct((batch, length, dilation * width), BF16),
                   jax.ShapeDtypeStruct((batch, length, dilation * width), F32)),
        grid=(batch, dilation, length // w),
        in_specs=[cur(0), prev(1), cur(1), prev(2), cur(2),
                  pl.BlockSpec((DIL_HEADS, w, 2 * w), lambda b, c, u: (0, 0, 0))],
        out_specs=(out_spec, out_spec),
        compiler_params=pltpu.CompilerParams(
            dimension_semantics=("parallel", "parallel", "parallel"), vmem_limit_bytes=limit),
        name=f"dil_attention_g{group}",
    )(qv, qv, qv, qv, qv, bias)
    return o.reshape(batch * seq, width), lse.reshape(batch * seq, width)


def _dil_merge_kernel(o0_ref, o1_ref, o2_ref, l0_ref, l1_ref, l2_ref, w_ref, x_ref, g_ref, out_ref):
    l0, l1, l2 = l0_ref[...], l1_ref[...], l2_ref[...]
    m = jnp.maximum(jnp.maximum(l0, l1), l2)
    e0, e1, e2 = jnp.exp(l0 - m), jnp.exp(l1 - m), jnp.exp(l2 - m)
    inv = 1.0 / (e0 + e1 + e2)
    o = (o0_ref[...].astype(F32) * (e0 * inv) + o1_ref[...].astype(F32) * (e1 * inv)
         + o2_ref[...].astype(F32) * (e2 * inv))
    mo = _dot(o.astype(BF16), w_ref[...])
    out_ref[...] = x_ref[...] + _rms_scale(mo) * g_ref[...]


def _dil_merge_proj(outs, lses, w, x, g):
    t, k = outs[0].shape
    d = w.shape[1]
    row = lambda i: (i, 0)
    limit = _vmem_limit(
        3 * _nbytes((ROW_TILE, k), BF16) + 3 * _nbytes((ROW_TILE, k), F32)
        + _nbytes((k, d), BF16) + 2 * _nbytes((ROW_TILE, d), F32),
        8 * _nbytes((ROW_TILE, k), F32) + 2 * _nbytes((ROW_TILE, d), F32))
    return pl.pallas_call(
        _dil_merge_kernel,
        out_shape=jax.ShapeDtypeStruct((t, d), F32),
        grid=(t // ROW_TILE,),
        in_specs=[pl.BlockSpec((ROW_TILE, k), row)] * 6
        + [pl.BlockSpec((k, d), lambda i: (0, 0)),
           pl.BlockSpec((ROW_TILE, d), row),
           pl.BlockSpec((1, d), lambda i: (0, 0))],
        out_specs=pl.BlockSpec((ROW_TILE, d), row),
        compiler_params=pltpu.CompilerParams(dimension_semantics=("parallel",),
                                             vmem_limit_bytes=limit),
        name="dil_merge_proj",
    )(*outs, *lses, w, x, g)


def _mla_pre_kernel(x_ref, g_ref, wcq_ref, wckv_ref, wkr_ref, wkr_rot_ref, qn_ref, kvn_ref,
                    wq_ref, wq_rot_ref, wk_ref, wv_ref, cos_ref, sin_ref,
                    q_ref, k_ref, vt_ref, *, q_scale):
    xn = (_rms_scale(x_ref[...]) * g_ref[...]).astype(BF16)
    cq = (_rms_scale(_dot(xn, wcq_ref[...])) * qn_ref[...]).astype(BF16)
    ckv = (_rms_scale(_dot(xn, wckv_ref[...])) * kvn_ref[...]).astype(BF16)
    cos, sin = cos_ref[...], sin_ref[...]
    k_rope = _dot(xn, wkr_ref[...]) * cos + _dot(xn, wkr_rot_ref[...]) * sin
    for h in range(MLA_HEADS):
        sl = slice(h * LANES, (h + 1) * LANES)
        q = _dot(cq, wq_ref[:, sl]) * cos + _dot(cq, wq_rot_ref[:, sl]) * sin
        q_ref[:, sl] = (q * q_scale).astype(q_ref.dtype)
        k_ref[:, sl] = (_dot(ckv, wk_ref[:, sl]) + k_rope).astype(k_ref.dtype)
    vt_ref[0] = _dot_nt(wv_ref[...], ckv).astype(vt_ref.dtype)


def _rope_rotation(w):
    half = MLA_ROPE // 2
    return jnp.concatenate([-w[..., half:], w[..., :half]], axis=-1)


def _pad_head_slabs(nope, rope):
    k, h = nope.shape[0], nope.shape[1]
    pad = jnp.zeros((k, h, LANES - MLA_NOPE - MLA_ROPE), nope.dtype)
    return jnp.concatenate([nope, rope, pad], axis=-1).reshape(k, h * LANES)


def _mla_pre(x, g, positions, w_in, q_norm, w_qb, kv_norm, w_kvb):
    t, d = x.shape
    h = MLA_HEADS
    w_cq = w_in[:, :MLA_Q_RANK].astype(BF16)
    w_ckv = w_in[:, MLA_Q_RANK:MLA_Q_RANK + MLA_KV_RANK].astype(BF16)
    w_kr = w_in[:, MLA_Q_RANK + MLA_KV_RANK:]
    zeros_nope = jnp.zeros((d, 1, MLA_NOPE), F32)
    w_kr_pad = _pad_head_slabs(zeros_nope, w_kr[:, None, :]).astype(BF16)
    w_kr_rot = _pad_head_slabs(zeros_nope, _rope_rotation(w_kr)[:, None, :]).astype(BF16)

    wq = w_qb.reshape(MLA_Q_RANK, h, MLA_NOPE + MLA_ROPE)
    wq_pad = _pad_head_slabs(wq[..., :MLA_NOPE], wq[..., MLA_NOPE:]).astype(BF16)
    wq_rot = _pad_head_slabs(jnp.zeros_like(wq[..., :MLA_NOPE]),
                             _rope_rotation(wq[..., MLA_NOPE:])).astype(BF16)
    wkv = w_kvb.reshape(MLA_KV_RANK, h, MLA_NOPE + MLA_V)
    wk_pad = _pad_head_slabs(wkv[..., :MLA_NOPE],
                             jnp.zeros((MLA_KV_RANK, h, MLA_ROPE), F32)).astype(BF16)
    wv_t = wkv[..., MLA_NOPE:].reshape(MLA_KV_RANK, h * MLA_V).T.astype(BF16)

    half = MLA_ROPE // 2
    freqs = ROPE_THETA ** (-jnp.arange(half, dtype=F32) / half)
    ang = positions.astype(F32).reshape(t, 1) * freqs
    cos, sin = jnp.cos(ang), jnp.sin(ang)
    tail = jnp.zeros((t, LANES - MLA_NOPE - MLA_ROPE), F32)
    cos_tab = jnp.concatenate([jnp.ones((t, MLA_NOPE), F32), cos, cos, tail], axis=-1)
    sin_tab = jnp.concatenate([jnp.zeros((t, MLA_NOPE), F32), sin, sin, tail], axis=-1)

    tm = ROW_TILE // 2
    full = lambda a: pl.BlockSpec(a.shape, lambda i: (0,) * a.ndim)
    row = lambda n: pl.BlockSpec((tm, n), lambda i: (i, 0))
    weights = (w_cq, w_ckv, w_kr_pad, w_kr_rot, q_norm.reshape(1, -1), kv_norm.reshape(1, -1),
               wq_pad, wq_rot, wk_pad, wv_t)
    assert tm == ATT_TILE
    limit = _vmem_limit(
        _nbytes((tm, d), F32) + sum(_nbytes(a.shape, a.dtype) for a in weights)
        + 2 * _nbytes((tm, LANES), F32) + 2 * _nbytes((tm, h * LANES), BF16)
        + _nbytes((tm, h * MLA_V), BF16),
        4 * _nbytes((tm, d), F32))
    return pl.pallas_call(
        functools.partial(_mla_pre_kernel, q_scale=(MLA_NOPE + MLA_ROPE) ** -0.5),
        out_shape=(jax.ShapeDtypeStruct((t, h * LANES), BF16),
                   jax.ShapeDtypeStruct((t, h * LANES), BF16),
                   jax.ShapeDtypeStruct((t // tm, h * MLA_V, tm), BF16)),
        grid=(t // tm,),
        in_specs=[row(d), full(g)] + [full(a) for a in weights] + [row(LANES), row(LANES)],
        out_specs=(row(h * LANES), row(h * LANES),
                   pl.BlockSpec((1, h * MLA_V, tm), lambda i: (i, 0, 0))),
        compiler_params=pltpu.CompilerParams(dimension_semantics=("parallel",),
                                             vmem_limit_bytes=limit),
        name="mla_pre",
    )(x, g, *weights, cos_tab, sin_tab)


def _mla_kernel(q_ref, k_ref, vt_ref, o_ref):
    tq = q_ref.shape[1]
    qi = pl.program_id(2)
    key = lax.broadcasted_iota(jnp.int32, (tq, tq), 0)
    query = lax.broadcasted_iota(jnp.int32, (tq, tq), 1)
    causal = key <= query
    n_heads = q_ref.shape[2] // LANES
    heads = range(n_heads)
    q_heads = [q_ref[0, :, h * LANES:(h + 1) * LANES] for h in heads]

    def all_heads(kb, st, mask):
        kblk = k_ref[0, pl.ds(pl.multiple_of(kb * tq, tq), tq), :]
        vt = vt_ref[kb]
        ss = [_dot_nt(kblk[:, h * LANES:(h + 1) * LANES], q_heads[h]) for h in heads]
        probs, stats = [], []
        for h in heads:
            s, (m, den) = ss[h], st[3 * h:3 * h + 2]
            if mask is not None:
                s = jnp.where(mask, s, NEG)
            m_new = jnp.maximum(m, jnp.max(s, axis=0, keepdims=True))
            alpha = jnp.exp(m - m_new)
            e = jnp.exp(s - m_new)
            stats.append((m_new, alpha * den + jnp.sum(e, axis=0, keepdims=True), alpha))
            probs.append(e.astype(BF16))
        out = []
        for h in heads:
            pv = _dot(vt[h * MLA_V:(h + 1) * MLA_V, :], probs[h])
            out.extend((stats[h][0], stats[h][1], stats[h][2] * st[3 * h + 2] + pv))
        return tuple(out)

    init = (jnp.full((1, tq), NEG, F32), jnp.zeros((1, tq), F32),
            jnp.zeros((MLA_V, tq), F32)) * n_heads
    st = all_heads(qi, init, causal)
    st = lax.fori_loop(0, qi, lambda n, st: all_heads(qi - 1 - n, st, None), st)
    for p in range(n_heads // 2):
        a, b = 2 * p, 2 * p + 1
        pair_t = jnp.concatenate([st[3 * a + 2] / st[3 * a + 1], st[3 * b + 2] / st[3 * b + 1]],
                                 axis=0)
        o_ref[0, :, p * LANES:(p + 1) * LANES] = pair_t.T.astype(o_ref.dtype)


MLA_HEADS_PER_STEP = 8


def _mla_attention(q, k, vt, batch, seq):
    nh = MLA_HEADS_PER_STEP
    steps = MLA_HEADS // nh
    tq = ATT_TILE
    nb = seq // tq
    limit = _vmem_limit(
        _nbytes((tq, nh * LANES), BF16) + _nbytes((seq, nh * LANES), BF16)
        + _nbytes((seq, nh * MLA_V), BF16) + _nbytes((tq, nh * MLA_V), BF16),
        8 * nh * _nbytes((tq, tq), F32))
    return pl.pallas_call(
        _mla_kernel,
        out_shape=jax.ShapeDtypeStruct((batch, seq, MLA_HEADS * MLA_V), BF16),
        grid=(batch, steps, nb),
        in_specs=[pl.BlockSpec((1, tq, nh * LANES), lambda b, p, i: (b, i, p)),
                  pl.BlockSpec((1, seq, nh * LANES), lambda b, p, i: (b, 0, p)),
                  pl.BlockSpec((nb, nh * MLA_V, tq), lambda b, p, i: (b, p, 0))],
        out_specs=pl.BlockSpec((1, tq, nh * MLA_V), lambda b, p, i: (b, i, p)),
        compiler_params=pltpu.CompilerParams(
            dimension_semantics=("parallel", "parallel", "parallel"), vmem_limit_bytes=limit),
        name="mla_attention",
    )(q, k, vt)


def _ffn_up_kernel(x_ref, g_ref, wg_ref, wu_ref, cwg_ref, cwu_ref, cbg_ref, cbu_ref, o_ref,
                   tail_g_ref, tail_u_ref, *, tiles_per_seq):
    tm = x_ref.shape[0]

    @pl.when(pl.program_id(0) % tiles_per_seq == 0)
    def _():
        tail_g_ref[...] = jnp.zeros_like(tail_g_ref)
        tail_u_ref[...] = jnp.zeros_like(tail_u_ref)

    xn = (_rms_scale(x_ref[...]) * g_ref[...]).astype(BF16)
    row = lax.broadcasted_iota(jnp.int32, (tm, FF_CHUNK), 0)

    def conv(h, tail_ref, cw_ref, cb_ref, sl):
        p1 = tail_ref[7:8, sl]
        p2 = tail_ref[6:7, sl]
        h1 = jnp.where(row == 0, p1, pltpu.roll(h, 1, 0))
        h2 = jnp.where(row == 0, p2, jnp.where(row == 1, p1, pltpu.roll(h, 2, 0)))
        tail_ref[:, sl] = h[tm - 8:, :]
        return cw_ref[0:1, sl] * h2 + cw_ref[1:2, sl] * h1 + cw_ref[2:3, sl] * h + cb_ref[:, sl]

    for c in range(o_ref.shape[1] // FF_CHUNK):
        sl = slice(c * FF_CHUNK, (c + 1) * FF_CHUNK)
        gate = conv(_dot(xn, wg_ref[:, sl]), tail_g_ref, cwg_ref, cbg_ref, sl)
        val = conv(_dot(xn, wu_ref[:, sl]), tail_u_ref, cwu_ref, cbu_ref, sl)
        act = gate * (1.0 / (1.0 + jnp.exp(-gate))) * val
        o_ref[:, sl] = act.astype(o_ref.dtype)


def _ffn_up(x, g, w_up, conv_w, conv_b, seq):
    t, d = x.shape
    wg, wu = w_up[:, :D_FF].astype(BF16), w_up[:, D_FF:].astype(BF16)
    cwg, cwu = conv_w[:, :D_FF], conv_w[:, D_FF:]
    cbg, cbu = conv_b[:D_FF].reshape(1, D_FF), conv_b[D_FF:].reshape(1, D_FF)
    assert seq % ROW_TILE == 0 and D_FF % FF_CHUNK == 0
    full = lambda a: pl.BlockSpec(a.shape, lambda i: (0,) * a.ndim)
    limit = _vmem_limit(
        _nbytes((ROW_TILE, d), F32) + 2 * _nbytes((d, D_FF), BF16)
        + _nbytes((ROW_TILE, D_FF), BF16) + 8 * _nbytes((1, D_FF), F32),
        2 * _nbytes((8, D_FF), F32) + _nbytes((ROW_TILE, d), F32)
        + 12 * _nbytes((ROW_TILE, FF_CHUNK), F32))
    return pl.pallas_call(
        functools.partial(_ffn_up_kernel, tiles_per_seq=seq // ROW_TILE),
        out_shape=jax.ShapeDtypeStruct((t, D_FF), BF16),
        grid=(t // ROW_TILE,),
        in_specs=[pl.BlockSpec((ROW_TILE, d), lambda i: (i, 0)), full(g), full(wg), full(wu),
                  full(cwg), full(cwu), full(cbg), full(cbu)],
        out_specs=pl.BlockSpec((ROW_TILE, D_FF), lambda i: (i, 0)),
        scratch_shapes=[pltpu.VMEM((8, D_FF), F32), pltpu.VMEM((8, D_FF), F32)],
        compiler_params=pltpu.CompilerParams(dimension_semantics=("arbitrary",),
                                             vmem_limit_bytes=limit),
        name="ffn_up_conv_gate",
    )(x, g, wg, wu, cwg, cwu, cbg, cbu)


def kernel(x, positions, rel_bias, norm_gains, sb_w_qkv, sb_w_o, dil_w_qkv, dil_w_o, mla_w_in,
           mla_q_norm, mla_w_qb, mla_kv_norm, mla_w_kvb, mla_w_o, ffn_w_up, ffn_conv_w,
           ffn_conv_b, ffn_w_down):
    batch, seq, d = x.shape
    t = batch * seq
    x = x.reshape(t, d)
    for i in range(DEPTH):
        kind, j = i % N_MIXERS, i // N_MIXERS
        gain = lambda n: norm_gains[i, n].reshape(1, d)
        if kind == 0:
            qk, vt = _sb_proj(x, gain(0), sb_w_qkv[j])
            o = _sb_attention(qk.reshape(batch, seq, -1), vt, batch, seq).reshape(t, -1)
            x = _proj_post(o, sb_w_o[j].astype(BF16), x, gain(1), "sb_out_proj")
        elif kind == 1:
            width = DIL_HEADS * HEAD_DIM
            w_groups = dil_w_qkv[j].reshape(d, 3, len(DIL_GROUPS), width).astype(BF16)
            outs, lses = [], []
            for grp, (window, dilation) in enumerate(DIL_GROUPS):
                n_back = window // dilation
                assert n_back == DIL_TILE and (seq // dilation) % DIL_TILE == 0
                qkv = _dil_proj(x, gain(0), w_groups[:, :, grp].reshape(d, 3 * width), dilation,
                                batch, seq, f"dil_qkv_proj_g{grp}")
                bias = _dil_bias(rel_bias, grp, dilation, n_back)
                o, lse = _dil_group_attention(qkv, bias, grp, dilation, batch, seq)
                outs.append(o)
                lses.append(lse)
            x = _dil_merge_proj(outs, lses, dil_w_o[j].astype(BF16), x, gain(1))
        else:
            q, k, vt = _mla_pre(x, gain(0), positions, mla_w_in[j], mla_q_norm[j], mla_w_qb[j],
                                mla_kv_norm[j], mla_w_kvb[j])
            o = _mla_attention(q.reshape(batch, seq, -1), k.reshape(batch, seq, -1),
                               vt, batch, seq).reshape(t, -1)
            x = _proj_post(o, mla_w_o[j].astype(BF16), x, gain(1), "mla_out_proj")
        act = _ffn_up(x, gain(2), ffn_w_up[i], ffn_conv_w[i], ffn_conv_b[i], seq)
        x = _proj_post(act, ffn_w_down[i].astype(BF16), x, gain(3), "ffn_down_proj")
    return x.reshape(batch, seq, d)
```

```python
import functools
import math

import jax
import jax.numpy as jnp
from jax import lax
from jax.experimental import pallas as pl
from jax.experimental.pallas import tpu as pltpu

F32 = jnp.float32
BF16 = jnp.bfloat16

D_MODEL = 1024
DEPTH = 4
N_MIXERS = 3
EPS = 1e-6
NEG = -1e30

SB_HEADS = 16
HEAD_DIM = 64

DIL_GROUPS = ((128, 1), (512, 4), (2048, 16))
DIL_HEADS = 8
N_BUCKETS = 32
BUCKET_MAX_DIST = 2048

MLA_HEADS = 16
MLA_Q_RANK = 384
MLA_KV_RANK = 256
MLA_NOPE = 64
MLA_ROPE = 32
MLA_V = 64
ROPE_THETA = 10000.0

D_FF = 2816

LANES = 128
V7X_VMEM_CAP_BYTES = 56 * 1024 * 1024

ROW_TILE = 512
ATT_TILE = 256
DIL_TILE = 128
FF_CHUNK = 256
FFN_ROW_TILE = 256


def _vmem_limit(pipelined_bytes, resident_bytes):
    return int(min(V7X_VMEM_CAP_BYTES, 2 * pipelined_bytes + resident_bytes))


def _nbytes(shape, dtype):
    return math.prod(shape) * jnp.dtype(dtype).itemsize


def _rms_scale(x):
    return x * lax.rsqrt(jnp.mean(x * x, axis=-1, keepdims=True) + EPS)


def _dot(a, b):
    return jnp.dot(a, b, preferred_element_type=F32)


def _neg_abs(x):
    bits = lax.bitcast_convert_type(x, jnp.uint32) | jnp.uint32(0x80000000)
    return lax.bitcast_convert_type(bits, F32)


def _dot_nt(a, b):
    return lax.dot_general(a, b, (((1,), (1,)), ((), ())), preferred_element_type=F32)


def _proj_post_kernel(a_ref, w_ref, x_ref, g_ref, o_ref):
    m = _dot(a_ref[...], w_ref[...])
    o_ref[...] = x_ref[...] + _rms_scale(m) * g_ref[...]


def _proj_post(a, w, x, g, name):
    t, k = a.shape
    d = w.shape[1]
    limit = _vmem_limit(
        _nbytes((ROW_TILE, k), BF16) + _nbytes((k, d), BF16) + 2 * _nbytes((ROW_TILE, d), F32),
        2 * _nbytes((ROW_TILE, d), F32))
    return pl.pallas_call(
        _proj_post_kernel,
        out_shape=jax.ShapeDtypeStruct((t, d), F32),
        grid=(t // ROW_TILE,),
        in_specs=[pl.BlockSpec((ROW_TILE, k), lambda i: (i, 0)),
                  pl.BlockSpec((k, d), lambda i: (0, 0)),
                  pl.BlockSpec((ROW_TILE, d), lambda i: (i, 0)),
                  pl.BlockSpec((1, d), lambda i: (0, 0))],
        out_specs=pl.BlockSpec((ROW_TILE, d), lambda i: (i, 0)),
        compiler_params=pltpu.CompilerParams(dimension_semantics=("parallel",),
                                             vmem_limit_bytes=limit),
        name=name,
    )(a, w, x, g)


def _sb_proj_kernel(x_ref, g_ref, wqk_ref, wvt_ref, qk_ref, vt_ref, *, n_chunk):
    xn = (_rms_scale(x_ref[...]) * g_ref[...]).astype(BF16)
    for c in range(qk_ref.shape[1] // n_chunk):
        sl = slice(c * n_chunk, (c + 1) * n_chunk)
        qk_ref[:, sl] = _dot(xn, wqk_ref[:, sl]).astype(qk_ref.dtype)
    tk = vt_ref.shape[2]
    for c in range(wvt_ref.shape[0] // n_chunk):
        sl = slice(c * n_chunk, (c + 1) * n_chunk)
        vt = _dot_nt(wvt_ref[sl, :], xn).astype(vt_ref.dtype)
        for j in range(vt_ref.shape[0]):
            vt_ref[j, sl, :] = vt[:, j * tk:(j + 1) * tk]


def _sb_proj(x, g, w_qkv):
    t, d = x.shape
    width = SB_HEADS * HEAD_DIM
    wqk = w_qkv[:, :2 * width].astype(BF16)
    wvt = w_qkv[:, 2 * width:].T.astype(BF16)
    n_chunk = 512
    tk = ATT_TILE
    assert t % ROW_TILE == 0 and ROW_TILE % tk == 0 and width % n_chunk == 0
    limit = _vmem_limit(
        _nbytes((ROW_TILE, d), F32) + _nbytes((d, 3 * width), BF16)
        + _nbytes((ROW_TILE, 3 * width), BF16),
        _nbytes((ROW_TILE, d), F32) + 2 * _nbytes((ROW_TILE, n_chunk), F32))
    return pl.pallas_call(
        functools.partial(_sb_proj_kernel, n_chunk=n_chunk),
        out_shape=(jax.ShapeDtypeStruct((t, 2 * width), BF16),
                   jax.ShapeDtypeStruct((t // tk, width, tk), BF16)),
        grid=(t // ROW_TILE,),
        in_specs=[pl.BlockSpec((ROW_TILE, d), lambda i: (i, 0)),
                  pl.BlockSpec((1, d), lambda i: (0, 0)),
                  pl.BlockSpec((d, 2 * width), lambda i: (0, 0)),
                  pl.BlockSpec((width, d), lambda i: (0, 0))],
        out_specs=(pl.BlockSpec((ROW_TILE, 2 * width), lambda i: (i, 0)),
                   pl.BlockSpec((ROW_TILE // tk, width, tk), lambda i: (i, 0, 0))),
        compiler_params=pltpu.CompilerParams(dimension_semantics=("parallel",),
                                             vmem_limit_bytes=limit),
        name="sb_qkv_proj",
    )(x, g, wqk, wvt)


SB_UNDERFLOW = 105.0
SB_HEADS_PER_STEP = 8


def _sb_kernel(q_ref, k_ref, vt_ref, o_ref):
    tq = q_ref.shape[1]
    n_pairs = q_ref.shape[2] // LANES
    qi = pl.program_id(2)
    lane = lax.broadcasted_iota(jnp.int32, (tq, LANES), 1)
    first = lane < HEAD_DIM
    q_heads = []
    for p in range(n_pairs):
        q2 = q_ref[0, :, p * LANES:(p + 1) * LANES] * jnp.asarray(HEAD_DIM ** -0.5, BF16)
        zero = jnp.zeros_like(q2)
        q_heads.append(jnp.where(first, q2, zero))
        q_heads.append(jnp.where(first, zero, q2))
    heads = range(len(q_heads))

    key = lax.broadcasted_iota(jnp.int32, (tq, tq), 0)
    other = lax.broadcasted_iota(jnp.int32, (tq, tq), 1)
    later = jnp.where(other > key, 1.0, 0.0).astype(BF16)
    causal = key < other

    def all_heads(kb, st, mask):
        kblk = k_ref[0, pl.ds(pl.multiple_of(kb * tq, tq), tq), :]
        vt = vt_ref[kb]
        pair = lambda h: slice((h // 2) * LANES, (h // 2 + 1) * LANES)
        zs = [_dot_nt(kblk[:, pair(h)], q_heads[h]) for h in heads]
        mids = []
        for h in heads:
            z = zs[h]
            softplus = jnp.maximum(z, 0.0) + jnp.log(1.0 + jnp.exp(_neg_abs(z)))
            base = (z - softplus) - st[2 * h]
            if mask is not None:
                softplus = jnp.where(mask, softplus, 0.0)
            mids.append((softplus.astype(BF16), base, jnp.sum(softplus, axis=0, keepdims=True)))
        betweens = [_dot(later, sp) for sp, _, _ in mids]
        weights = []
        for h in heads:
            a = jnp.exp(mids[h][1] - betweens[h])
            if mask is not None:
                a = jnp.where(mask, a, 0.0)
            weights.append(a.astype(BF16))
        out = []
        for h in heads:
            out.append(st[2 * h] + mids[h][2])
            out.append(st[2 * h + 1] + _dot(vt[h * HEAD_DIM:(h + 1) * HEAD_DIM, :], weights[h]))
        return tuple(out)

    def smallest_carry(st):
        return functools.reduce(jnp.minimum, [jnp.min(c) for c in st[0::2]])

    init = (jnp.zeros((1, tq), F32), jnp.zeros((HEAD_DIM, tq), F32)) * len(q_heads)
    st = all_heads(qi, init, causal)

    def more(carry):
        n, low, _ = carry
        return jnp.logical_and(n < qi, low < SB_UNDERFLOW)

    def body(carry):
        n, _, st = carry
        st = all_heads(qi - 1 - n, st, None)
        return n + 1, smallest_carry(st), st

    _, _, st = lax.while_loop(more, body, (jnp.int32(0), smallest_carry(st), st))
    for p in range(n_pairs):
        pair_t = jnp.concatenate([st[4 * p + 1], st[4 * p + 3]], axis=0)
        o_ref[0, :, p * LANES:(p + 1) * LANES] = pair_t.T.astype(o_ref.dtype)


def _sb_attention(qk, vt, batch, seq):
    width = SB_HEADS_PER_STEP * HEAD_DIM
    groups = SB_HEADS // SB_HEADS_PER_STEP
    tq = ATT_TILE
    nb = seq // tq
    limit = _vmem_limit(
        2 * _nbytes((tq, width), BF16) + 2 * _nbytes((seq, width), BF16),
        12 * SB_HEADS_PER_STEP * _nbytes((tq, tq), F32))
    return pl.pallas_call(
        _sb_kernel,
        out_shape=jax.ShapeDtypeStruct((batch, seq, SB_HEADS * HEAD_DIM), BF16),
        grid=(batch, groups, nb),
        in_specs=[pl.BlockSpec((1, tq, width), lambda b, p, i: (b, i, p)),
                  pl.BlockSpec((1, seq, width), lambda b, p, i: (b, 0, groups + p)),
                  pl.BlockSpec((nb, width, tq), lambda b, p, i: (b, p, 0))],
        out_specs=pl.BlockSpec((1, tq, width), lambda b, p, i: (b, i, p)),
        compiler_params=pltpu.CompilerParams(
            dimension_semantics=("parallel", "parallel", "parallel"), vmem_limit_bytes=limit),
        name="sb_attention",
    )(qk, qk, vt)


PERM_BLOCK = 256


def _dil_proj_kernel(x_ref, g_ref, w_ref, o_ref, xp_ref, *, dilation, n_chunk):
    seq = x_ref.shape[0]
    run = PERM_BLOCK // dilation
    length = seq // dilation
    if dilation > 1:
        row = lax.broadcasted_iota(jnp.int32, (PERM_BLOCK, PERM_BLOCK), 0)
        col = lax.broadcasted_iota(jnp.int32, (PERM_BLOCK, PERM_BLOCK), 1)
        src = (row & (run - 1)) * dilation + lax.shift_right_logical(row, run.bit_length() - 1)
        perm = jnp.where(col == src, 1.0, 0.0).astype(BF16)
    for blk in range(seq // PERM_BLOCK):
        rows = slice(blk * PERM_BLOCK, (blk + 1) * PERM_BLOCK)
        xn = (_rms_scale(x_ref[rows, :]) * g_ref[...]).astype(BF16)
        if dilation == 1:
            xp_ref[rows, :] = xn
        else:
            moved = _dot(perm, xn).astype(BF16)
            for c in range(dilation):
                dst = c * length + blk * run
                xp_ref[dst:dst + run, :] = moved[c * run:(c + 1) * run, :]
    for c in range(w_ref.shape[1] // n_chunk):
        sl = slice(c * n_chunk, (c + 1) * n_chunk)
        for m in range(seq // ROW_TILE):
            rows = slice(m * ROW_TILE, (m + 1) * ROW_TILE)
            o_ref[rows, sl] = _dot(xp_ref[rows, :], w_ref[:, sl]).astype(o_ref.dtype)


def _dil_proj(x, g, w, dilation, batch, seq, name):
    d = x.shape[1]
    n = w.shape[1]
    n_chunk = 512
    assert seq % PERM_BLOCK == 0 and PERM_BLOCK % (16 * dilation) == 0 and n % n_chunk == 0
    limit = _vmem_limit(
        _nbytes((seq, d), F32) + _nbytes((d, n), BF16) + _nbytes((seq, n), BF16),
        _nbytes((seq, d), BF16) + 4 * _nbytes((ROW_TILE, d), F32))
    return pl.pallas_call(
        functools.partial(_dil_proj_kernel, dilation=dilation, n_chunk=n_chunk),
        out_shape=jax.ShapeDtypeStruct((batch * seq, n), BF16),
        grid=(batch,),
        in_specs=[pl.BlockSpec((seq, d), lambda b: (b, 0)),
                  pl.BlockSpec((1, d), lambda b: (0, 0)),
                  pl.BlockSpec((d, n), lambda b: (0, 0))],
        out_specs=pl.BlockSpec((seq, n), lambda b: (b, 0)),
        scratch_shapes=[pltpu.VMEM((seq, d), BF16)],
        compiler_params=pltpu.CompilerParams(dimension_semantics=("parallel",),
                                             vmem_limit_bytes=limit),
        name=name,
    )(x, g, w)


def _dil_kernel(q_ref, kp_ref, kc_ref, vp_ref, vc_ref, bias_ref, o_ref, lse_ref):
    w = q_ref.shape[1]
    ut = pl.program_id(2)
    lane = lax.broadcasted_iota(jnp.int32, (w, LANES), 1)
    first = lane < HEAD_DIM
    col = lax.broadcasted_iota(jnp.int32, (w, 2 * w), 1)
    key_ok = jnp.logical_or(col >= w, ut > 0)
    n_pairs = q_ref.shape[2] // LANES
    slab = lambda p: slice(p * LANES, (p + 1) * LANES)
    q_heads, k2, v2 = [], [], []
    for p in range(n_pairs):
        q2 = q_ref[0, :, slab(p)] * jnp.asarray(HEAD_DIM ** -0.5, BF16)
        zero = jnp.zeros_like(q2)
        q_heads += [jnp.where(first, q2, zero), jnp.where(first, zero, q2)]
        k2.append(jnp.concatenate([kp_ref[0, :, slab(p)], kc_ref[0, :, slab(p)]], axis=0))
        v2.append(jnp.concatenate([vp_ref[0, :, slab(p)], vc_ref[0, :, slab(p)]], axis=0))
    heads = range(2 * n_pairs)
    scores = [_dot_nt(q_heads[h], k2[h // 2]) for h in heads]
    probs, dens, lses = [], [], []
    for h in heads:
        s = jnp.where(key_ok, scores[h] + bias_ref[h], NEG)
        m = jnp.max(s, axis=-1, keepdims=True)
        e = jnp.exp(s - m)
        den = jnp.sum(e, axis=-1, keepdims=True)
        probs.append(e.astype(BF16))
        dens.append(den)
        lses.append(m + jnp.log(den))
    outs = [_dot(probs[h], v2[h // 2]) / dens[h] for h in heads]
    for p in range(n_pairs):
        o_ref[0, :, slab(p)] = jnp.where(first, outs[2 * p], outs[2 * p + 1]).astype(o_ref.dtype)
        lse_ref[0, :, slab(p)] = jnp.where(first, lses[2 * p], lses[2 * p + 1])


def _dil_bias(rel_bias, group, dilation, n_back):
    w = n_back
    dist = jnp.arange(w + 1) * dilation
    max_exact = N_BUCKETS // 2
    d = jnp.maximum(dist.astype(F32), 1.0)
    large = max_exact + (jnp.log(d / max_exact) / math.log(BUCKET_MAX_DIST / max_exact)
                         * (N_BUCKETS - max_exact)).astype(jnp.int32)
    bucket = jnp.where(dist < max_exact, dist, jnp.minimum(large, N_BUCKETS - 1))
    per_m = rel_bias[:, group * DIL_HEADS:(group + 1) * DIL_HEADS][bucket].astype(F32).T
    period = jnp.concatenate(
        [per_m[:, ::-1], jnp.full((DIL_HEADS, w), NEG, F32)], axis=1)
    tiled = jnp.tile(period, (1, w))[:, :w * 2 * w]
    return tiled.reshape(DIL_HEADS, w, 2 * w)


def _dil_group_attention(qkv, bias, group, dilation, batch, seq):
    width = DIL_HEADS * HEAD_DIM
    length = seq // dilation
    w = DIL_TILE
    qv = qkv.reshape(batch * dilation, length, 3 * width)

    def cur(which):
        return pl.BlockSpec((1, w, width), lambda b, c, u: (b * dilation + c, u, which))

    def prev(which):
        return pl.BlockSpec((1, w, width),
                            lambda b, c, u: (b * dilation + c, jnp.maximum(u - 1, 0), which))

    out_spec = pl.BlockSpec((1, w, width), lambda b, c, u: (b, u, c))
    limit = _vmem_limit(
        5 * _nbytes((w, width), BF16) + _nbytes((DIL_HEADS, w, 2 * w), F32)
        + _nbytes((w, width), BF16) + _nbytes((w, width), F32),
        32 * _nbytes((w, 2 * w), F32))
    o, lse = pl.pallas_call(
        _dil_kernel,
        out_shape=(jax.ShapeDtypeStruct((batch, length, dilation * width), BF16),
                   jax.ShapeDtypeStruct((batch, length, dilation * width), F32)),
        grid=(batch, dilation, length // w),
        in_specs=[cur(0), prev(1), cur(1), prev(2), cur(2),
                  pl.BlockSpec((DIL_HEADS, w, 2 * w), lambda b, c, u: (0, 0, 0))],
        out_specs=(out_spec, out_spec),
        compiler_params=pltpu.CompilerParams(
            dimension_semantics=("parallel", "parallel", "parallel"), vmem_limit_bytes=limit),
        name=f"dil_attention_g{group}",
    )(qv, qv, qv, qv, qv, bias)
    return o.reshape(batch * seq, width), lse.reshape(batch * seq, width)


def _dil_merge_kernel(o0_ref, o1_ref, o2_ref, l0_ref, l1_ref, l2_ref, w_ref, x_ref, g_ref, out_ref):
    l0, l1, l2 = l0_ref[...], l1_ref[...], l2_ref[...]
    m = jnp.maximum(jnp.maximum(l0, l1), l2)
    e0, e1, e2 = jnp.exp(l0 - m), jnp.exp(l1 - m), jnp.exp(l2 - m)
    inv = 1.0 / (e0 + e1 + e2)
    o = (o0_ref[...].astype(F32) * (e0 * inv) + o1_ref[...].astype(F32) * (e1 * inv)
         + o2_ref[...].astype(F32) * (e2 * inv))
    mo = _dot(o.astype(BF16), w_ref[...])
    out_ref[...] = x_ref[...] + _rms_scale(mo) * g_ref[...]


def _dil_merge_proj(outs, lses, w, x, g):
    t, k = outs[0].shape
    d = w.shape[1]
    row = lambda i: (i, 0)
    limit = _vmem_limit(
        3 * _nbytes((ROW_TILE, k), BF16) + 3 * _nbytes((ROW_TILE, k), F32)
        + _nbytes((k, d), BF16) + 2 * _nbytes((ROW_TILE, d), F32),
        8 * _nbytes((ROW_TILE, k), F32) + 2 * _nbytes((ROW_TILE, d), F32))
    return pl.pallas_call(
        _dil_merge_kernel,
        out_shape=jax.ShapeDtypeStruct((t, d), F32),
        grid=(t // ROW_TILE,),
        in_specs=[pl.BlockSpec((ROW_TILE, k), row)] * 6
        + [pl.BlockSpec((k, d), lambda i: (0, 0)),
           pl.BlockSpec((ROW_TILE, d), row),
           pl.BlockSpec((1, d), lambda i: (0, 0))],
        out_specs=pl.BlockSpec((ROW_TILE, d), row),
        compiler_params=pltpu.CompilerParams(dimension_semantics=("parallel",),
                                             vmem_limit_bytes=limit),
        name="dil_merge_proj",
    )(*outs, *lses, w, x, g)


def _mla_pre_kernel(x_ref, g_ref, wcq_ref, wckv_ref, wkr_ref, wkr_rot_ref, qn_ref, kvn_ref,
                    wq_ref, wq_rot_ref, wk_ref, wv_ref, cos_ref, sin_ref,
                    q_ref, k_ref, vt_ref, *, q_scale):
    xn = (_rms_scale(x_ref[...]) * g_ref[...]).astype(BF16)
    cq = (_rms_scale(_dot(xn, wcq_ref[...])) * qn_ref[...]).astype(BF16)
    ckv = (_rms_scale(_dot(xn, wckv_ref[...])) * kvn_ref[...]).astype(BF16)
    cos, sin = cos_ref[...], sin_ref[...]
    k_rope = _dot(xn, wkr_ref[...]) * cos + _dot(xn, wkr_rot_ref[...]) * sin
    cos2, sin2, k_rope2 = (jnp.concatenate([a, a], axis=1) for a in (cos, sin, k_rope))
    for p in range(MLA_HEADS // 2):
        sl = slice(2 * p * LANES, 2 * (p + 1) * LANES)
        q = _dot(cq, wq_ref[:, sl]) * cos2 + _dot(cq, wq_rot_ref[:, sl]) * sin2
        q_ref[:, sl] = (q * q_scale).astype(q_ref.dtype)
        k_ref[:, sl] = (_dot(ckv, wk_ref[:, sl]) + k_rope2).astype(k_ref.dtype)
    vt_ref[0] = _dot_nt(wv_ref[...], ckv).astype(vt_ref.dtype)


def _rope_rotation(w):
    half = MLA_ROPE // 2
    return jnp.concatenate([-w[..., half:], w[..., :half]], axis=-1)


def _pad_head_slabs(nope, rope):
    k, h = nope.shape[0], nope.shape[1]
    pad = jnp.zeros((k, h, LANES - MLA_NOPE - MLA_ROPE), nope.dtype)
    return jnp.concatenate([nope, rope, pad], axis=-1).reshape(k, h * LANES)


def _mla_pre(x, g, positions, w_in, q_norm, w_qb, kv_norm, w_kvb):
    t, d = x.shape
    h = MLA_HEADS
    w_cq = w_in[:, :MLA_Q_RANK].astype(BF16)
    w_ckv = w_in[:, MLA_Q_RANK:MLA_Q_RANK + MLA_KV_RANK].astype(BF16)
    w_kr = w_in[:, MLA_Q_RANK + MLA_KV_RANK:]
    zeros_nope = jnp.zeros((d, 1, MLA_NOPE), F32)
    w_kr_pad = _pad_head_slabs(zeros_nope, w_kr[:, None, :]).astype(BF16)
    w_kr_rot = _pad_head_slabs(zeros_nope, _rope_rotation(w_kr)[:, None, :]).astype(BF16)

    wq = w_qb.reshape(MLA_Q_RANK, h, MLA_NOPE + MLA_ROPE)
    wq_pad = _pad_head_slabs(wq[..., :MLA_NOPE], wq[..., MLA_NOPE:]).astype(BF16)
    wq_rot = _pad_head_slabs(jnp.zeros_like(wq[..., :MLA_NOPE]),
                             _rope_rotation(wq[..., MLA_NOPE:])).astype(BF16)
    wkv = w_kvb.reshape(MLA_KV_RANK, h, MLA_NOPE + MLA_V)
    wk_pad = _pad_head_slabs(wkv[..., :MLA_NOPE],
                             jnp.zeros((MLA_KV_RANK, h, MLA_ROPE), F32)).astype(BF16)
    wv_t = wkv[..., MLA_NOPE:].reshape(MLA_KV_RANK, h * MLA_V).T.astype(BF16)

    half = MLA_ROPE // 2
    freqs = ROPE_THETA ** (-jnp.arange(half, dtype=F32) / half)
    ang = positions.astype(F32).reshape(t, 1) * freqs
    cos, sin = jnp.cos(ang), jnp.sin(ang)
    tail = jnp.zeros((t, LANES - MLA_NOPE - MLA_ROPE), F32)
    cos_tab = jnp.concatenate([jnp.ones((t, MLA_NOPE), F32), cos, cos, tail], axis=-1)
    sin_tab = jnp.concatenate([jnp.zeros((t, MLA_NOPE), F32), sin, sin, tail], axis=-1)

    tm = ROW_TILE // 2
    full = lambda a: pl.BlockSpec(a.shape, lambda i: (0,) * a.ndim)
    row = lambda n: pl.BlockSpec((tm, n), lambda i: (i, 0))
    weights = (w_cq, w_ckv, w_kr_pad, w_kr_rot, q_norm.reshape(1, -1), kv_norm.reshape(1, -1),
               wq_pad, wq_rot, wk_pad, wv_t)
    assert tm == ATT_TILE
    limit = _vmem_limit(
        _nbytes((tm, d), F32) + sum(_nbytes(a.shape, a.dtype) for a in weights)
        + 2 * _nbytes((tm, LANES), F32) + 2 * _nbytes((tm, h * LANES), BF16)
        + _nbytes((tm, h * MLA_V), BF16),
        4 * _nbytes((tm, d), F32))
    return pl.pallas_call(
        functools.partial(_mla_pre_kernel, q_scale=(MLA_NOPE + MLA_ROPE) ** -0.5),
        out_shape=(jax.ShapeDtypeStruct((t, h * LANES), BF16),
                   jax.ShapeDtypeStruct((t, h * LANES), BF16),
                   jax.ShapeDtypeStruct((t // tm, h * MLA_V, tm), BF16)),
        grid=(t // tm,),
        in_specs=[row(d), full(g)] + [full(a) for a in weights] + [row(LANES), row(LANES)],
        out_specs=(row(h * LANES), row(h * LANES),
                   pl.BlockSpec((1, h * MLA_V, tm), lambda i: (i, 0, 0))),
        compiler_params=pltpu.CompilerParams(dimension_semantics=("parallel",),
                                             vmem_limit_bytes=limit),
        name="mla_pre",
    )(x, g, *weights, cos_tab, sin_tab)


def _mla_kernel(q_ref, k_ref, vt_ref, o_ref):
    tq = q_ref.shape[1]
    qi = pl.program_id(2)
    key = lax.broadcasted_iota(jnp.int32, (tq, tq), 0)
    query = lax.broadcasted_iota(jnp.int32, (tq, tq), 1)
    causal = key <= query
    n_heads = q_ref.shape[2] // LANES
    heads = range(n_heads)
    q_heads = [q_ref[0, :, h * LANES:(h + 1) * LANES] for h in heads]

    def all_heads(kb, st, mask):
        kblk = k_ref[0, pl.ds(pl.multiple_of(kb * tq, tq), tq), :]
        vt = vt_ref[kb]
        ss = [_dot_nt(kblk[:, h * LANES:(h + 1) * LANES], q_heads[h]) for h in heads]
        probs, stats = [], []
        for h in heads:
            s, (m, den) = ss[h], st[3 * h:3 * h + 2]
            if mask is not None:
                s = jnp.where(mask, s, NEG)
            m_new = jnp.maximum(m, jnp.max(s, axis=0, keepdims=True))
            alpha = jnp.exp(m - m_new)
            e = jnp.exp(s - m_new)
            stats.append((m_new, alpha * den + jnp.sum(e, axis=0, keepdims=True), alpha))
            probs.append(e.astype(BF16))
        out = []
        for h in heads:
            pv = _dot(vt[h * MLA_V:(h + 1) * MLA_V, :], probs[h])
            out.extend((stats[h][0], stats[h][1], stats[h][2] * st[3 * h + 2] + pv))
        return tuple(out)

    init = (jnp.full((1, tq), NEG, F32), jnp.zeros((1, tq), F32),
            jnp.zeros((MLA_V, tq), F32)) * n_heads
    st = all_heads(qi, init, causal)
    st = lax.fori_loop(0, qi, lambda n, st: all_heads(qi - 1 - n, st, None), st)
    for p in range(n_heads // 2):
        a, b = 2 * p, 2 * p + 1
        pair_t = jnp.concatenate([st[3 * a + 2] / st[3 * a + 1], st[3 * b + 2] / st[3 * b + 1]],
                                 axis=0)
        o_ref[0, :, p * LANES:(p + 1) * LANES] = pair_t.T.astype(o_ref.dtype)


MLA_HEADS_PER_STEP = 8


def _mla_attention(q, k, vt, batch, seq):
    nh = MLA_HEADS_PER_STEP
    steps = MLA_HEADS // nh
    tq = ATT_TILE
    nb = seq // tq
    limit = _vmem_limit(
        _nbytes((tq, nh * LANES), BF16) + _nbytes((seq, nh * LANES), BF16)
        + _nbytes((seq, nh * MLA_V), BF16) + _nbytes((tq, nh * MLA_V), BF16),
        8 * nh * _nbytes((tq, tq), F32))
    return pl.pallas_call(
        _mla_kernel,
        out_shape=jax.ShapeDtypeStruct((batch, seq, MLA_HEADS * MLA_V), BF16),
        grid=(batch, steps, nb),
        in_specs=[pl.BlockSpec((1, tq, nh * LANES), lambda b, p, i: (b, i, p)),
                  pl.BlockSpec((1, seq, nh * LANES), lambda b, p, i: (b, 0, p)),
                  pl.BlockSpec((nb, nh * MLA_V, tq), lambda b, p, i: (b, p, 0))],
        out_specs=pl.BlockSpec((1, tq, nh * MLA_V), lambda b, p, i: (b, i, p)),
        compiler_params=pltpu.CompilerParams(
            dimension_semantics=("parallel", "parallel", "parallel"), vmem_limit_bytes=limit),
        name="mla_attention",
    )(q, k, vt)


def _ffn_kernel(x_ref, g_in_ref, wg_ref, wu_ref, cwg_ref, cwu_ref, cbg_ref, cbu_ref, wd_ref,
                g_out_ref, o_ref, tail_g_ref, tail_u_ref, acc_ref, *, tiles_per_seq):
    tm = x_ref.shape[0]

    @pl.when(pl.program_id(0) % tiles_per_seq == 0)
    def _():
        tail_g_ref[...] = jnp.zeros_like(tail_g_ref)
        tail_u_ref[...] = jnp.zeros_like(tail_u_ref)

    x = x_ref[...]
    xn = (_rms_scale(x) * g_in_ref[...]).astype(BF16)
    row = lax.broadcasted_iota(jnp.int32, (tm, FF_CHUNK), 0)

    def conv(h, tail_ref, cw_ref, cb_ref, sl):
        p1 = tail_ref[7:8, sl]
        p2 = tail_ref[6:7, sl]
        h1 = jnp.where(row == 0, p1, pltpu.roll(h, 1, 0))
        h2 = jnp.where(row == 0, p2, jnp.where(row == 1, p1, pltpu.roll(h, 2, 0)))
        tail_ref[:, sl] = h[tm - 8:, :]
        return cw_ref[0:1, sl] * h2 + cw_ref[1:2, sl] * h1 + cw_ref[2:3, sl] * h + cb_ref[:, sl]

    chunk = lambda c: slice(c * FF_CHUNK, (c + 1) * FF_CHUNK)
    n_chunks = wg_ref.shape[1] // FF_CHUNK
    pending = None
    for c in range(n_chunks):
        hg = _dot(xn, wg_ref[:, chunk(c)])
        hu = _dot(xn, wu_ref[:, chunk(c)])
        if pending is not None:
            part = _dot(pending[1], wd_ref[chunk(pending[0]), :])
            if pending[0] == 0:
                acc_ref[...] = part
            else:
                acc_ref[...] += part
        gate = conv(hg, tail_g_ref, cwg_ref, cbg_ref, chunk(c))
        val = conv(hu, tail_u_ref, cwu_ref, cbu_ref, chunk(c))
        pending = (c, (gate * (1.0 / (1.0 + jnp.exp(-gate))) * val).astype(BF16))
    m = acc_ref[...] + _dot(pending[1], wd_ref[chunk(pending[0]), :])
    o_ref[...] = x + _rms_scale(m) * g_out_ref[...]


def _ffn(x, g_in, g_out, w_up, conv_w, conv_b, w_down, seq):
    t, d = x.shape
    wg, wu = w_up[:, :D_FF].astype(BF16), w_up[:, D_FF:].astype(BF16)
    wd = w_down.astype(BF16)
    cwg, cwu = conv_w[:, :D_FF], conv_w[:, D_FF:]
    cbg, cbu = conv_b[:D_FF].reshape(1, D_FF), conv_b[D_FF:].reshape(1, D_FF)
    tm = FFN_ROW_TILE
    assert seq % tm == 0 and D_FF % FF_CHUNK == 0
    full = lambda a: pl.BlockSpec(a.shape, lambda i: (0,) * a.ndim, pipeline_mode=pl.Buffered(1))
    row = pl.BlockSpec((tm, d), lambda i: (i, 0))
    limit = _vmem_limit(
        2 * _nbytes((tm, d), F32),
        3 * _nbytes((d, D_FF), BF16) + 8 * _nbytes((8, D_FF), F32)
        + 3 * _nbytes((tm, d), F32) + 16 * _nbytes((tm, FF_CHUNK), F32))
    return pl.pallas_call(
        functools.partial(_ffn_kernel, tiles_per_seq=seq // tm),
        out_shape=jax.ShapeDtypeStruct((t, d), F32),
        grid=(t // tm,),
        in_specs=[row, full(g_in), full(wg), full(wu), full(cwg), full(cwu), full(cbg), full(cbu),
                  full(wd), full(g_out)],
        out_specs=row,
        scratch_shapes=[pltpu.VMEM((8, D_FF), F32), pltpu.VMEM((8, D_FF), F32),
                        pltpu.VMEM((tm, d), F32)],
        compiler_params=pltpu.CompilerParams(dimension_semantics=("arbitrary",),
                                             vmem_limit_bytes=limit),
        name="conv_ffn",
    )(x, g_in, wg, wu, cwg, cwu, cbg, cbu, wd, g_out)


def kernel(x, positions, rel_bias, norm_gains, sb_w_qkv, sb_w_o, dil_w_qkv, dil_w_o, mla_w_in,
           mla_q_norm, mla_w_qb, mla_kv_norm, mla_w_kvb, mla_w_o, ffn_w_up, ffn_conv_w,
           ffn_conv_b, ffn_w_down):
    batch, seq, d = x.shape
    t = batch * seq
    x = x.reshape(t, d)
    for i in range(DEPTH):
        kind, j = i % N_MIXERS, i // N_MIXERS
        gain = lambda n: norm_gains[i, n].reshape(1, d)
        if kind == 0:
            qk, vt = _sb_proj(x, gain(0), sb_w_qkv[j])
            o = _sb_attention(qk.reshape(batch, seq, -1), vt, batch, seq).reshape(t, -1)
            x = _proj_post(o, sb_w_o[j].astype(BF16), x, gain(1), "sb_out_proj")
        elif kind == 1:
            width = DIL_HEADS * HEAD_DIM
            w_groups = dil_w_qkv[j].reshape(d, 3, len(DIL_GROUPS), width).astype(BF16)
            outs, lses = [], []
            for grp, (window, dilation) in enumerate(DIL_GROUPS):
                n_back = window // dilation
                assert n_back == DIL_TILE and (seq // dilation) % DIL_TILE == 0
                qkv = _dil_proj(x, gain(0), w_groups[:, :, grp].reshape(d, 3 * width), dilation,
                                batch, seq, f"dil_qkv_proj_g{grp}")
                bias = _dil_bias(rel_bias, grp, dilation, n_back)
                o, lse = _dil_group_attention(qkv, bias, grp, dilation, batch, seq)
                outs.append(o)
                lses.append(lse)
            x = _dil_merge_proj(outs, lses, dil_w_o[j].astype(BF16), x, gain(1))
        else:
            q, k, vt = _mla_pre(x, gain(0), positions, mla_w_in[j], mla_q_norm[j], mla_w_qb[j],
                                mla_kv_norm[j], mla_w_kvb[j])
            o = _mla_attention(q.reshape(batch, seq, -1), k.reshape(batch, seq, -1),
                               vt, batch, seq).reshape(t, -1)
            x = _proj_post(o, mla_w_o[j].astype(BF16), x, gain(1), "mla_out_proj")
        x = _ffn(x, gain(2), gain(3), ffn_w_up[i], ffn_conv_w[i], ffn_conv_b[i], ffn_w_down[i], seq)
    return x.reshape(batch, seq, d)
```

```python
import functools
import math

import jax
import jax.numpy as jnp
from jax import lax
from jax.experimental import pallas as pl
from jax.experimental.pallas import tpu as pltpu

F32 = jnp.float32
BF16 = jnp.bfloat16

D_MODEL = 1024
DEPTH = 4
N_MIXERS = 3
EPS = 1e-6
NEG = -1e30

SB_HEADS = 16
HEAD_DIM = 64

DIL_GROUPS = ((128, 1), (512, 4), (2048, 16))
DIL_HEADS = 8
N_BUCKETS = 32
BUCKET_MAX_DIST = 2048

MLA_HEADS = 16
MLA_Q_RANK = 384
MLA_KV_RANK = 256
MLA_NOPE = 64
MLA_ROPE = 32
MLA_V = 64
ROPE_THETA = 10000.0

D_FF = 2816

LANES = 128
V7X_VMEM_CAP_BYTES = 56 * 1024 * 1024

ROW_TILE = 512
ATT_TILE = 256
DIL_TILE = 128
FF_CHUNK = 256
FFN_ROW_TILE = 512
FFN_ROW_GROUPS = 4


def _vmem_limit(pipelined_bytes, resident_bytes):
    return int(min(V7X_VMEM_CAP_BYTES, 2 * pipelined_bytes + resident_bytes))


def _nbytes(shape, dtype):
    return math.prod(shape) * jnp.dtype(dtype).itemsize


def _rms_scale(x):
    return x * lax.rsqrt(jnp.mean(x * x, axis=-1, keepdims=True) + EPS)


def _dot(a, b):
    return jnp.dot(a, b, preferred_element_type=F32)


def _neg_abs(x):
    bits = lax.bitcast_convert_type(x, jnp.uint32) | jnp.uint32(0x80000000)
    return lax.bitcast_convert_type(bits, F32)


def _dot_nt(a, b):
    return lax.dot_general(a, b, (((1,), (1,)), ((), ())), preferred_element_type=F32)


def _proj_post_kernel(a_ref, w_ref, x_ref, g_ref, o_ref):
    m = _dot(a_ref[...], w_ref[...])
    o_ref[...] = x_ref[...] + _rms_scale(m) * g_ref[...]


def _proj_post(a, w, x, g, name):
    t, k = a.shape
    d = w.shape[1]
    limit = _vmem_limit(
        _nbytes((ROW_TILE, k), BF16) + _nbytes((k, d), BF16) + 2 * _nbytes((ROW_TILE, d), F32),
        2 * _nbytes((ROW_TILE, d), F32))
    return pl.pallas_call(
        _proj_post_kernel,
        out_shape=jax.ShapeDtypeStruct((t, d), F32),
        grid=(t // ROW_TILE,),
        in_specs=[pl.BlockSpec((ROW_TILE, k), lambda i: (i, 0)),
                  pl.BlockSpec((k, d), lambda i: (0, 0)),
                  pl.BlockSpec((ROW_TILE, d), lambda i: (i, 0)),
                  pl.BlockSpec((1, d), lambda i: (0, 0))],
        out_specs=pl.BlockSpec((ROW_TILE, d), lambda i: (i, 0)),
        compiler_params=pltpu.CompilerParams(dimension_semantics=("parallel",),
                                             vmem_limit_bytes=limit),
        name=name,
    )(a, w, x, g)


def _sb_proj_kernel(x_ref, g_ref, wqk_ref, wvt_ref, qk_ref, vt_ref, *, n_chunk):
    xn = (_rms_scale(x_ref[...]) * g_ref[...]).astype(BF16)
    for c in range(qk_ref.shape[1] // n_chunk):
        sl = slice(c * n_chunk, (c + 1) * n_chunk)
        qk_ref[:, sl] = _dot(xn, wqk_ref[:, sl]).astype(qk_ref.dtype)
    tk = vt_ref.shape[2]
    for c in range(wvt_ref.shape[0] // n_chunk):
        sl = slice(c * n_chunk, (c + 1) * n_chunk)
        vt = _dot_nt(wvt_ref[sl, :], xn).astype(vt_ref.dtype)
        for j in range(vt_ref.shape[0]):
            vt_ref[j, sl, :] = vt[:, j * tk:(j + 1) * tk]


def _sb_proj(x, g, w_qkv):
    t, d = x.shape
    width = SB_HEADS * HEAD_DIM
    wqk = w_qkv[:, :2 * width].astype(BF16)
    wvt = w_qkv[:, 2 * width:].T.astype(BF16)
    n_chunk = 512
    tk = ATT_TILE
    assert t % ROW_TILE == 0 and ROW_TILE % tk == 0 and width % n_chunk == 0
    limit = _vmem_limit(
        _nbytes((ROW_TILE, d), F32) + _nbytes((d, 3 * width), BF16)
        + _nbytes((ROW_TILE, 3 * width), BF16),
        _nbytes((ROW_TILE, d), F32) + 2 * _nbytes((ROW_TILE, n_chunk), F32))
    return pl.pallas_call(
        functools.partial(_sb_proj_kernel, n_chunk=n_chunk),
        out_shape=(jax.ShapeDtypeStruct((t, 2 * width), BF16),
                   jax.ShapeDtypeStruct((t // tk, width, tk), BF16)),
        grid=(t // ROW_TILE,),
        in_specs=[pl.BlockSpec((ROW_TILE, d), lambda i: (i, 0)),
                  pl.BlockSpec((1, d), lambda i: (0, 0)),
                  pl.BlockSpec((d, 2 * width), lambda i: (0, 0)),
                  pl.BlockSpec((width, d), lambda i: (0, 0))],
        out_specs=(pl.BlockSpec((ROW_TILE, 2 * width), lambda i: (i, 0)),
                   pl.BlockSpec((ROW_TILE // tk, width, tk), lambda i: (i, 0, 0))),
        compiler_params=pltpu.CompilerParams(dimension_semantics=("parallel",),
                                             vmem_limit_bytes=limit),
        name="sb_qkv_proj",
    )(x, g, wqk, wvt)


SB_UNDERFLOW = 105.0
SB_HEADS_PER_STEP = 8


def _sb_kernel(q_ref, k_ref, vt_ref, o_ref):
    tq = q_ref.shape[1]
    n_pairs = q_ref.shape[2] // LANES
    qi = pl.program_id(2)
    lane = lax.broadcasted_iota(jnp.int32, (tq, LANES), 1)
    first = lane < HEAD_DIM
    q_heads = []
    for p in range(n_pairs):
        q2 = q_ref[0, :, p * LANES:(p + 1) * LANES] * jnp.asarray(HEAD_DIM ** -0.5, BF16)
        zero = jnp.zeros_like(q2)
        q_heads.append(jnp.where(first, q2, zero))
        q_heads.append(jnp.where(first, zero, q2))
    heads = range(len(q_heads))

    key = lax.broadcasted_iota(jnp.int32, (tq, tq), 0)
    other = lax.broadcasted_iota(jnp.int32, (tq, tq), 1)
    later = jnp.where(other > key, 1.0, 0.0).astype(BF16)
    causal = key < other

    def all_heads(kb, st, mask):
        kblk = k_ref[0, pl.ds(pl.multiple_of(kb * tq, tq), tq), :]
        vt = vt_ref[kb]
        pair = lambda h: slice((h // 2) * LANES, (h // 2 + 1) * LANES)
        zs = [_dot_nt(kblk[:, pair(h)], q_heads[h]) for h in heads]
        mids = []
        for h in heads:
            z = zs[h]
            softplus = jnp.maximum(z, 0.0) + jnp.log(1.0 + jnp.exp(_neg_abs(z)))
            base = (z - softplus) - st[2 * h]
            if mask is not None:
                softplus = jnp.where(mask, softplus, 0.0)
            mids.append((softplus.astype(BF16), base, jnp.sum(softplus, axis=0, keepdims=True)))
        betweens = [_dot(later, sp) for sp, _, _ in mids]
        weights = []
        for h in heads:
            a = jnp.exp(mids[h][1] - betweens[h])
            if mask is not None:
                a = jnp.where(mask, a, 0.0)
            weights.append(a.astype(BF16))
        out = []
        for h in heads:
            out.append(st[2 * h] + mids[h][2])
            out.append(st[2 * h + 1] + _dot(vt[h * HEAD_DIM:(h + 1) * HEAD_DIM, :], weights[h]))
        return tuple(out)

    def smallest_carry(st):
        return functools.reduce(jnp.minimum, [jnp.min(c) for c in st[0::2]])

    init = (jnp.zeros((1, tq), F32), jnp.zeros((HEAD_DIM, tq), F32)) * len(q_heads)
    st = all_heads(qi, init, causal)

    def more(carry):
        n, low, _ = carry
        return jnp.logical_and(n < qi, low < SB_UNDERFLOW)

    def body(carry):
        n, _, st = carry
        st = all_heads(qi - 1 - n, st, None)
        return n + 1, smallest_carry(st), st

    _, _, st = lax.while_loop(more, body, (jnp.int32(0), smallest_carry(st), st))
    for p in range(n_pairs):
        pair_t = jnp.concatenate([st[4 * p + 1], st[4 * p + 3]], axis=0)
        o_ref[0, :, p * LANES:(p + 1) * LANES] = pair_t.T.astype(o_ref.dtype)


def _sb_attention(qk, vt, batch, seq):
    width = SB_HEADS_PER_STEP * HEAD_DIM
    groups = SB_HEADS // SB_HEADS_PER_STEP
    tq = ATT_TILE
    nb = seq // tq
    limit = _vmem_limit(
        2 * _nbytes((tq, width), BF16) + 2 * _nbytes((seq, width), BF16),
        12 * SB_HEADS_PER_STEP * _nbytes((tq, tq), F32))
    return pl.pallas_call(
        _sb_kernel,
        out_shape=jax.ShapeDtypeStruct((batch, seq, SB_HEADS * HEAD_DIM), BF16),
        grid=(batch, groups, nb),
        in_specs=[pl.BlockSpec((1, tq, width), lambda b, p, i: (b, i, p)),
                  pl.BlockSpec((1, seq, width), lambda b, p, i: (b, 0, groups + p)),
                  pl.BlockSpec((nb, width, tq), lambda b, p, i: (b, p, 0))],
        out_specs=pl.BlockSpec((1, tq, width), lambda b, p, i: (b, i, p)),
        compiler_params=pltpu.CompilerParams(
            dimension_semantics=("parallel", "parallel", "parallel"), vmem_limit_bytes=limit),
        name="sb_attention",
    )(qk, qk, vt)


PERM_BLOCK = 256


def _dil_proj_kernel(x_ref, g_ref, w_ref, o_ref, xp_ref, *, dilation, n_chunk):
    seq = x_ref.shape[0]
    run = PERM_BLOCK // dilation
    length = seq // dilation
    if dilation > 1:
        row = lax.broadcasted_iota(jnp.int32, (PERM_BLOCK, PERM_BLOCK), 0)
        col = lax.broadcasted_iota(jnp.int32, (PERM_BLOCK, PERM_BLOCK), 1)
        src = (row & (run - 1)) * dilation + lax.shift_right_logical(row, run.bit_length() - 1)
        perm = jnp.where(col == src, 1.0, 0.0).astype(BF16)
    for blk in range(seq // PERM_BLOCK):
        rows = slice(blk * PERM_BLOCK, (blk + 1) * PERM_BLOCK)
        xn = (_rms_scale(x_ref[rows, :]) * g_ref[...]).astype(BF16)
        if dilation == 1:
            xp_ref[rows, :] = xn
        else:
            moved = _dot(perm, xn).astype(BF16)
            for c in range(dilation):
                dst = c * length + blk * run
                xp_ref[dst:dst + run, :] = moved[c * run:(c + 1) * run, :]
    for c in range(w_ref.shape[1] // n_chunk):
        sl = slice(c * n_chunk, (c + 1) * n_chunk)
        for m in range(seq // ROW_TILE):
            rows = slice(m * ROW_TILE, (m + 1) * ROW_TILE)
            o_ref[rows, sl] = _dot(xp_ref[rows, :], w_ref[:, sl]).astype(o_ref.dtype)


def _dil_proj(x, g, w, dilation, batch, seq, name):
    d = x.shape[1]
    n = w.shape[1]
    n_chunk = 512
    assert seq % PERM_BLOCK == 0 and PERM_BLOCK % (16 * dilation) == 0 and n % n_chunk == 0
    limit = _vmem_limit(
        _nbytes((seq, d), F32) + _nbytes((d, n), BF16) + _nbytes((seq, n), BF16),
        _nbytes((seq, d), BF16) + 4 * _nbytes((ROW_TILE, d), F32))
    return pl.pallas_call(
        functools.partial(_dil_proj_kernel, dilation=dilation, n_chunk=n_chunk),
        out_shape=jax.ShapeDtypeStruct((batch * seq, n), BF16),
        grid=(batch,),
        in_specs=[pl.BlockSpec((seq, d), lambda b: (b, 0)),
                  pl.BlockSpec((1, d), lambda b: (0, 0)),
                  pl.BlockSpec((d, n), lambda b: (0, 0))],
        out_specs=pl.BlockSpec((seq, n), lambda b: (b, 0)),
        scratch_shapes=[pltpu.VMEM((seq, d), BF16)],
        compiler_params=pltpu.CompilerParams(dimension_semantics=("parallel",),
                                             vmem_limit_bytes=limit),
        name=name,
    )(x, g, w)


def _dil_kernel(q_ref, kp_ref, kc_ref, vp_ref, vc_ref, bias_ref, o_ref, lse_ref):
    w = q_ref.shape[1]
    ut = pl.program_id(2)
    lane = lax.broadcasted_iota(jnp.int32, (w, LANES), 1)
    first = lane < HEAD_DIM
    col = lax.broadcasted_iota(jnp.int32, (w, 2 * w), 1)
    key_ok = jnp.logical_or(col >= w, ut > 0)
    n_pairs = q_ref.shape[2] // LANES
    slab = lambda p: slice(p * LANES, (p + 1) * LANES)
    q_heads, k2, v2 = [], [], []
    for p in range(n_pairs):
        q2 = q_ref[0, :, slab(p)] * jnp.asarray(HEAD_DIM ** -0.5, BF16)
        zero = jnp.zeros_like(q2)
        q_heads += [jnp.where(first, q2, zero), jnp.where(first, zero, q2)]
        k2.append(jnp.concatenate([kp_ref[0, :, slab(p)], kc_ref[0, :, slab(p)]], axis=0))
        v2.append(jnp.concatenate([vp_ref[0, :, slab(p)], vc_ref[0, :, slab(p)]], axis=0))
    heads = range(2 * n_pairs)
    scores = [_dot_nt(q_heads[h], k2[h // 2]) for h in heads]
    probs, dens, lses = [], [], []
    for h in heads:
        s = jnp.where(key_ok, scores[h] + bias_ref[h], NEG)
        m = jnp.max(s, axis=-1, keepdims=True)
        e = jnp.exp(s - m)
        den = jnp.sum(e, axis=-1, keepdims=True)
        probs.append(e.astype(BF16))
        dens.append(den)
        lses.append(m + jnp.log(den))
    outs = [_dot(probs[h], v2[h // 2]) / dens[h] for h in heads]
    for p in range(n_pairs):
        o_ref[0, :, slab(p)] = jnp.where(first, outs[2 * p], outs[2 * p + 1]).astype(o_ref.dtype)
    lse = jnp.zeros((w, LANES), F32)
    for h in heads:
        lse = jnp.where(lane == h, lses[h], lse)
    lse_ref[0] = lse


def _dil_bias(rel_bias, group, dilation, n_back):
    w = n_back
    dist = jnp.arange(w + 1) * dilation
    max_exact = N_BUCKETS // 2
    d = jnp.maximum(dist.astype(F32), 1.0)
    large = max_exact + (jnp.log(d / max_exact) / math.log(BUCKET_MAX_DIST / max_exact)
                         * (N_BUCKETS - max_exact)).astype(jnp.int32)
    bucket = jnp.where(dist < max_exact, dist, jnp.minimum(large, N_BUCKETS - 1))
    per_m = rel_bias[:, group * DIL_HEADS:(group + 1) * DIL_HEADS][bucket].astype(F32).T
    period = jnp.concatenate(
        [per_m[:, ::-1], jnp.full((DIL_HEADS, w), NEG, F32)], axis=1)
    tiled = jnp.tile(period, (1, w))[:, :w * 2 * w]
    return tiled.reshape(DIL_HEADS, w, 2 * w)


def _dil_group_attention(qkv, bias, group, dilation, batch, seq):
    width = DIL_HEADS * HEAD_DIM
    length = seq // dilation
    w = DIL_TILE
    qv = qkv.reshape(batch * dilation, length, 3 * width)

    def cur(which):
        return pl.BlockSpec((1, w, width), lambda b, c, u: (b * dilation + c, u, which))

    def prev(which):
        return pl.BlockSpec((1, w, width),
                            lambda b, c, u: (b * dilation + c, jnp.maximum(u - 1, 0), which))

    at = lambda b, c, u: (b * dilation + c, u, 0)
    limit = _vmem_limit(
        5 * _nbytes((w, width), BF16) + _nbytes((DIL_HEADS, w, 2 * w), F32)
        + _nbytes((w, width), BF16) + _nbytes((w, LANES), F32),
        32 * _nbytes((w, 2 * w), F32))
    o, lse = pl.pallas_call(
        _dil_kernel,
        out_shape=(jax.ShapeDtypeStruct((batch * dilation, length, width), BF16),
                   jax.ShapeDtypeStruct((batch * dilation, length, LANES), F32)),
        grid=(batch, dilation, length // w),
        in_specs=[cur(0), prev(1), cur(1), prev(2), cur(2),
                  pl.BlockSpec((DIL_HEADS, w, 2 * w), lambda b, c, u: (0, 0, 0))],
        out_specs=(pl.BlockSpec((1, w, width), at), pl.BlockSpec((1, w, LANES), at)),
        compiler_params=pltpu.CompilerParams(
            dimension_semantics=("parallel", "parallel", "parallel"), vmem_limit_bytes=limit),
        name=f"dil_attention_g{group}",
    )(qv, qv, qv, qv, qv, bias)
    return o.reshape(batch * seq, width), lse.reshape(batch * seq, LANES)


def _dil_token_order_kernel(o_ref, l_ref, on_ref, ln_ref, *, dilation):
    seq = o_ref.shape[0]
    run = PERM_BLOCK // dilation
    length = seq // dilation
    row = lax.broadcasted_iota(jnp.int32, (PERM_BLOCK, PERM_BLOCK), 0)
    col = lax.broadcasted_iota(jnp.int32, (PERM_BLOCK, PERM_BLOCK), 1)
    src = (row & (dilation - 1)) * run + lax.shift_right_logical(row, dilation.bit_length() - 1)
    perm = jnp.where(col == src, 1.0, 0.0).astype(BF16)
    for blk in range(seq // PERM_BLOCK):
        gather = lambda ref: jnp.concatenate(
            [ref[c * length + blk * run:c * length + (blk + 1) * run, :] for c in range(dilation)],
            axis=0)
        rows = slice(blk * PERM_BLOCK, (blk + 1) * PERM_BLOCK)
        on_ref[rows, :] = _dot(perm, gather(o_ref)).astype(on_ref.dtype)
        l = gather(l_ref)
        hi = l.astype(BF16)
        rest = l - hi.astype(F32)
        mid = rest.astype(BF16)
        lo = (rest - mid.astype(F32)).astype(BF16)
        ln_ref[rows, :] = (_dot(perm, hi) + _dot(perm, mid)) + _dot(perm, lo)


def _dil_token_order(o, lse, dilation, batch, seq, name):
    width = o.shape[1]
    assert seq % PERM_BLOCK == 0 and PERM_BLOCK % (16 * dilation) == 0
    spec = lambda n: pl.BlockSpec((seq, n), lambda b: (b, 0))
    limit = _vmem_limit(2 * _nbytes((seq, width), BF16) + 2 * _nbytes((seq, LANES), F32),
                        8 * _nbytes((PERM_BLOCK, width), F32))
    return pl.pallas_call(
        functools.partial(_dil_token_order_kernel, dilation=dilation),
        out_shape=(jax.ShapeDtypeStruct(o.shape, o.dtype), jax.ShapeDtypeStruct(lse.shape, lse.dtype)),
        grid=(batch,),
        in_specs=[spec(width), spec(LANES)],
        out_specs=(spec(width), spec(LANES)),
        compiler_params=pltpu.CompilerParams(dimension_semantics=("parallel",),
                                             vmem_limit_bytes=limit),
        name=name,
    )(o, lse)


def _dil_merge_kernel(o0_ref, o1_ref, o2_ref, l0_ref, l1_ref, l2_ref, w_ref, x_ref, g_ref, out_ref):
    l0, l1, l2 = l0_ref[...], l1_ref[...], l2_ref[...]
    m = jnp.maximum(jnp.maximum(l0, l1), l2)
    e0, e1, e2 = jnp.exp(l0 - m), jnp.exp(l1 - m), jnp.exp(l2 - m)
    inv = 1.0 / (e0 + e1 + e2)
    width = o0_ref.shape[1]
    head = lax.broadcasted_iota(jnp.int32, (LANES, width), 0)
    lane = lax.broadcasted_iota(jnp.int32, (LANES, width), 1)
    owner = lax.shift_right_logical(lane, HEAD_DIM.bit_length() - 1)
    spread = jnp.where(owner == head, 1.0, 0.0).astype(BF16)

    def per_lane(wt):
        hi = wt.astype(BF16)
        lo = (wt - hi.astype(F32)).astype(BF16)
        return _dot(hi, spread) + _dot(lo, spread)

    o = (o0_ref[...].astype(F32) * per_lane(e0 * inv) + o1_ref[...].astype(F32) * per_lane(e1 * inv)
         + o2_ref[...].astype(F32) * per_lane(e2 * inv))
    mo = _dot(o.astype(BF16), w_ref[...])
    out_ref[...] = x_ref[...] + _rms_scale(mo) * g_ref[...]


def _dil_merge_proj(outs, lses, w, x, g):
    t, k = outs[0].shape
    d = w.shape[1]
    row = lambda i: (i, 0)
    limit = _vmem_limit(
        3 * _nbytes((ROW_TILE, k), BF16) + 3 * _nbytes((ROW_TILE, LANES), F32)
        + _nbytes((k, d), BF16) + 2 * _nbytes((ROW_TILE, d), F32),
        8 * _nbytes((ROW_TILE, k), F32) + 2 * _nbytes((ROW_TILE, d), F32))
    return pl.pallas_call(
        _dil_merge_kernel,
        out_shape=jax.ShapeDtypeStruct((t, d), F32),
        grid=(t // ROW_TILE,),
        in_specs=[pl.BlockSpec((ROW_TILE, k), row)] * 3 + [pl.BlockSpec((ROW_TILE, LANES), row)] * 3
        + [pl.BlockSpec((k, d), lambda i: (0, 0)),
           pl.BlockSpec((ROW_TILE, d), row),
           pl.BlockSpec((1, d), lambda i: (0, 0))],
        out_specs=pl.BlockSpec((ROW_TILE, d), row),
        compiler_params=pltpu.CompilerParams(dimension_semantics=("parallel",),
                                             vmem_limit_bytes=limit),
        name="dil_merge_proj",
    )(*outs, *lses, w, x, g)


def _mla_pre_kernel(x_ref, g_ref, wcq_ref, wckv_ref, wkr_ref, wkr_rot_ref, qn_ref, kvn_ref,
                    wq_ref, wq_rot_ref, wk_ref, wv_ref, cos_ref, sin_ref,
                    q_ref, k_ref, vt_ref, *, q_scale):
    xn = (_rms_scale(x_ref[...]) * g_ref[...]).astype(BF16)
    cq = (_rms_scale(_dot(xn, wcq_ref[...])) * qn_ref[...]).astype(BF16)
    ckv = (_rms_scale(_dot(xn, wckv_ref[...])) * kvn_ref[...]).astype(BF16)
    cos, sin = cos_ref[...], sin_ref[...]
    k_rope = _dot(xn, wkr_ref[...]) * cos + _dot(xn, wkr_rot_ref[...]) * sin
    cos2, sin2, k_rope2 = (jnp.concatenate([a, a], axis=1) for a in (cos, sin, k_rope))
    for p in range(MLA_HEADS // 2):
        sl = slice(2 * p * LANES, 2 * (p + 1) * LANES)
        q = _dot(cq, wq_ref[:, sl]) * cos2 + _dot(cq, wq_rot_ref[:, sl]) * sin2
        q_ref[:, sl] = (q * q_scale).astype(q_ref.dtype)
        k_ref[:, sl] = (_dot(ckv, wk_ref[:, sl]) + k_rope2).astype(k_ref.dtype)
    vt_ref[0] = _dot_nt(wv_ref[...], ckv).astype(vt_ref.dtype)


def _rope_rotation(w):
    half = MLA_ROPE // 2
    return jnp.concatenate([-w[..., half:], w[..., :half]], axis=-1)


def _pad_head_slabs(nope, rope):
    k, h = nope.shape[0], nope.shape[1]
    pad = jnp.zeros((k, h, LANES - MLA_NOPE - MLA_ROPE), nope.dtype)
    return jnp.concatenate([nope, rope, pad], axis=-1).reshape(k, h * LANES)


def _mla_pre(x, g, positions, w_in, q_norm, w_qb, kv_norm, w_kvb):
    t, d = x.shape
    h = MLA_HEADS
    w_cq = w_in[:, :MLA_Q_RANK].astype(BF16)
    w_ckv = w_in[:, MLA_Q_RANK:MLA_Q_RANK + MLA_KV_RANK].astype(BF16)
    w_kr = w_in[:, MLA_Q_RANK + MLA_KV_RANK:]
    zeros_nope = jnp.zeros((d, 1, MLA_NOPE), F32)
    w_kr_pad = _pad_head_slabs(zeros_nope, w_kr[:, None, :]).astype(BF16)
    w_kr_rot = _pad_head_slabs(zeros_nope, _rope_rotation(w_kr)[:, None, :]).astype(BF16)

    wq = w_qb.reshape(MLA_Q_RANK, h, MLA_NOPE + MLA_ROPE)
    wq_pad = _pad_head_slabs(wq[..., :MLA_NOPE], wq[..., MLA_NOPE:]).astype(BF16)
    wq_rot = _pad_head_slabs(jnp.zeros_like(wq[..., :MLA_NOPE]),
                             _rope_rotation(wq[..., MLA_NOPE:])).astype(BF16)
    wkv = w_kvb.reshape(MLA_KV_RANK, h, MLA_NOPE + MLA_V)
    wk_pad = _pad_head_slabs(wkv[..., :MLA_NOPE],
                             jnp.zeros((MLA_KV_RANK, h, MLA_ROPE), F32)).astype(BF16)
    wv_t = wkv[..., MLA_NOPE:].reshape(MLA_KV_RANK, h * MLA_V).T.astype(BF16)

    half = MLA_ROPE // 2
    freqs = ROPE_THETA ** (-jnp.arange(half, dtype=F32) / half)
    ang = positions.astype(F32).reshape(t, 1) * freqs
    cos, sin = jnp.cos(ang), jnp.sin(ang)
    tail = jnp.zeros((t, LANES - MLA_NOPE - MLA_ROPE), F32)
    cos_tab = jnp.concatenate([jnp.ones((t, MLA_NOPE), F32), cos, cos, tail], axis=-1)
    sin_tab = jnp.concatenate([jnp.zeros((t, MLA_NOPE), F32), sin, sin, tail], axis=-1)

    tm = ROW_TILE // 2
    full = lambda a: pl.BlockSpec(a.shape, lambda i: (0,) * a.ndim)
    row = lambda n: pl.BlockSpec((tm, n), lambda i: (i, 0))
    weights = (w_cq, w_ckv, w_kr_pad, w_kr_rot, q_norm.reshape(1, -1), kv_norm.reshape(1, -1),
               wq_pad, wq_rot, wk_pad, wv_t)
    assert tm == ATT_TILE
    limit = _vmem_limit(
        _nbytes((tm, d), F32) + sum(_nbytes(a.shape, a.dtype) for a in weights)
        + 2 * _nbytes((tm, LANES), F32) + 2 * _nbytes((tm, h * LANES), BF16)
        + _nbytes((tm, h * MLA_V), BF16),
        4 * _nbytes((tm, d), F32))
    return pl.pallas_call(
        functools.partial(_mla_pre_kernel, q_scale=(MLA_NOPE + MLA_ROPE) ** -0.5),
        out_shape=(jax.ShapeDtypeStruct((t, h * LANES), BF16),
                   jax.ShapeDtypeStruct((t, h * LANES), BF16),
                   jax.ShapeDtypeStruct((t // tm, h * MLA_V, tm), BF16)),
        grid=(t // tm,),
        in_specs=[row(d), full(g)] + [full(a) for a in weights] + [row(LANES), row(LANES)],
        out_specs=(row(h * LANES), row(h * LANES),
                   pl.BlockSpec((1, h * MLA_V, tm), lambda i: (i, 0, 0))),
        compiler_params=pltpu.CompilerParams(dimension_semantics=("parallel",),
                                             vmem_limit_bytes=limit),
        name="mla_pre",
    )(x, g, *weights, cos_tab, sin_tab)


def _mla_kernel(q_ref, k_ref, vt_ref, o_ref):
    tq = q_ref.shape[1]
    qi = pl.program_id(2)
    key = lax.broadcasted_iota(jnp.int32, (tq, tq), 0)
    query = lax.broadcasted_iota(jnp.int32, (tq, tq), 1)
    causal = key <= query
    n_heads = q_ref.shape[2] // LANES
    heads = range(n_heads)
    q_heads = [q_ref[0, :, h * LANES:(h + 1) * LANES] for h in heads]

    def all_heads(kb, st, mask):
        kblk = k_ref[0, pl.ds(pl.multiple_of(kb * tq, tq), tq), :]
        vt = vt_ref[kb]
        ss = [_dot_nt(kblk[:, h * LANES:(h + 1) * LANES], q_heads[h]) for h in heads]
        probs, stats = [], []
        for h in heads:
            s, (m, den) = ss[h], st[3 * h:3 * h + 2]
            if mask is not None:
                s = jnp.where(mask, s, NEG)
            m_new = jnp.maximum(m, jnp.max(s, axis=0, keepdims=True))
            alpha = jnp.exp(m - m_new)
            e = jnp.exp(s - m_new)
            stats.append((m_new, alpha * den + jnp.sum(e, axis=0, keepdims=True), alpha))
            probs.append(e.astype(BF16))
        out = []
        for h in heads:
            pv = _dot(vt[h * MLA_V:(h + 1) * MLA_V, :], probs[h])
            out.extend((stats[h][0], stats[h][1], stats[h][2] * st[3 * h + 2] + pv))
        return tuple(out)

    init = (jnp.full((1, tq), NEG, F32), jnp.zeros((1, tq), F32),
            jnp.zeros((MLA_V, tq), F32)) * n_heads
    st = all_heads(qi, init, causal)
    st = lax.fori_loop(0, qi, lambda n, st: all_heads(qi - 1 - n, st, None), st)
    for p in range(n_heads // 2):
        a, b = 2 * p, 2 * p + 1
        pair_t = jnp.concatenate([st[3 * a + 2] / st[3 * a + 1], st[3 * b + 2] / st[3 * b + 1]],
                                 axis=0)
        o_ref[0, :, p * LANES:(p + 1) * LANES] = pair_t.T.astype(o_ref.dtype)


MLA_HEADS_PER_STEP = 16


def _mla_attention(q, k, vt, batch, seq):
    nh = MLA_HEADS_PER_STEP
    steps = MLA_HEADS // nh
    tq = ATT_TILE
    nb = seq // tq
    limit = _vmem_limit(
        _nbytes((tq, nh * LANES), BF16) + _nbytes((seq, nh * LANES), BF16)
        + _nbytes((seq, nh * MLA_V), BF16) + _nbytes((tq, nh * MLA_V), BF16),
        8 * nh * _nbytes((tq, tq), F32))
    return pl.pallas_call(
        _mla_kernel,
        out_shape=jax.ShapeDtypeStruct((batch, seq, MLA_HEADS * MLA_V), BF16),
        grid=(batch, steps, nb),
        in_specs=[pl.BlockSpec((1, tq, nh * LANES), lambda b, p, i: (b, i, p)),
                  pl.BlockSpec((1, seq, nh * LANES), lambda b, p, i: (b, 0, p)),
                  pl.BlockSpec((nb, nh * MLA_V, tq), lambda b, p, i: (b, p, 0))],
        out_specs=pl.BlockSpec((1, tq, nh * MLA_V), lambda b, p, i: (b, i, p)),
        compiler_params=pltpu.CompilerParams(
            dimension_semantics=("parallel", "parallel", "parallel"), vmem_limit_bytes=limit),
        name="mla_attention",
    )(q, k, vt)


def _ffn_kernel(x_ref, g_in_ref, wg_ref, wu_ref, cwg_ref, cwu_ref, cbg_ref, cbu_ref, wd_ref,
                g_out_ref, o_ref, tail_g_ref, tail_u_ref, acc_ref, *, tiles_per_seq):
    tm = x_ref.shape[0]

    @pl.when(pl.program_id(0) % tiles_per_seq == 0)
    def _():
        tail_g_ref[...] = jnp.zeros_like(tail_g_ref)
        tail_u_ref[...] = jnp.zeros_like(tail_u_ref)

    x = x_ref[...]
    xn = (_rms_scale(x) * g_in_ref[...]).astype(BF16)
    row = lax.broadcasted_iota(jnp.int32, (tm, FF_CHUNK), 0)

    def conv(h, tail_ref, cw_ref, cb_ref, sl):
        p1 = tail_ref[7:8, sl]
        p2 = tail_ref[6:7, sl]
        h1 = jnp.where(row == 0, p1, pltpu.roll(h, 1, 0))
        h2 = jnp.where(row == 0, p2, jnp.where(row == 1, p1, pltpu.roll(h, 2, 0)))
        tail_ref[:, sl] = h[tm - 8:, :]
        return cw_ref[0:1, sl] * h2 + cw_ref[1:2, sl] * h1 + cw_ref[2:3, sl] * h + cb_ref[:, sl]

    chunk = lambda c: slice(c * FF_CHUNK, (c + 1) * FF_CHUNK)
    n_chunks = wg_ref.shape[1] // FF_CHUNK
    pending = None
    for c in range(n_chunks):
        groups = [slice(r * tm // FFN_ROW_GROUPS, (r + 1) * tm // FFN_ROW_GROUPS)
                  for r in range(FFN_ROW_GROUPS)]
        hg = jnp.concatenate([_dot(xn[rs], wg_ref[:, chunk(c)]) for rs in groups], axis=0)
        hu = jnp.concatenate([_dot(xn[rs], wu_ref[:, chunk(c)]) for rs in groups], axis=0)
        if pending is not None:
            part = _dot(pending[1], wd_ref[chunk(pending[0]), :])
            if pending[0] == 0:
                acc_ref[...] = part
            else:
                acc_ref[...] += part
        gate = conv(hg, tail_g_ref, cwg_ref, cbg_ref, chunk(c))
        val = conv(hu, tail_u_ref, cwu_ref, cbu_ref, chunk(c))
        pending = (c, (gate * (1.0 / (1.0 + jnp.exp(-gate))) * val).astype(BF16))
    m = acc_ref[...] + _dot(pending[1], wd_ref[chunk(pending[0]), :])
    o_ref[...] = x + _rms_scale(m) * g_out_ref[...]


def _ffn(x, g_in, g_out, w_up, conv_w, conv_b, w_down, seq):
    t, d = x.shape
    wg, wu = w_up[:, :D_FF].astype(BF16), w_up[:, D_FF:].astype(BF16)
    wd = w_down.astype(BF16)
    cwg, cwu = conv_w[:, :D_FF], conv_w[:, D_FF:]
    cbg, cbu = conv_b[:D_FF].reshape(1, D_FF), conv_b[D_FF:].reshape(1, D_FF)
    tm = FFN_ROW_TILE
    assert seq % tm == 0 and D_FF % FF_CHUNK == 0
    full = lambda a: pl.BlockSpec(a.shape, lambda i: (0,) * a.ndim, pipeline_mode=pl.Buffered(1))
    row = pl.BlockSpec((tm, d), lambda i: (i, 0))
    limit = _vmem_limit(
        2 * _nbytes((tm, d), F32),
        3 * _nbytes((d, D_FF), BF16) + 8 * _nbytes((8, D_FF), F32)
        + 3 * _nbytes((tm, d), F32) + 16 * _nbytes((tm, FF_CHUNK), F32))
    return pl.pallas_call(
        functools.partial(_ffn_kernel, tiles_per_seq=seq // tm),
        out_shape=jax.ShapeDtypeStruct((t, d), F32),
        grid=(t // tm,),
        in_specs=[row, full(g_in), full(wg), full(wu), full(cwg), full(cwu), full(cbg), full(cbu),
                  full(wd), full(g_out)],
        out_specs=row,
        scratch_shapes=[pltpu.VMEM((8, D_FF), F32), pltpu.VMEM((8, D_FF), F32),
                        pltpu.VMEM((tm, d), F32)],
        compiler_params=pltpu.CompilerParams(dimension_semantics=("arbitrary",),
                                             vmem_limit_bytes=limit),
        name="conv_ffn",
    )(x, g_in, wg, wu, cwg, cwu, cbg, cbu, wd, g_out)


def kernel(x, positions, rel_bias, norm_gains, sb_w_qkv, sb_w_o, dil_w_qkv, dil_w_o, mla_w_in,
           mla_q_norm, mla_w_qb, mla_kv_norm, mla_w_kvb, mla_w_o, ffn_w_up, ffn_conv_w,
           ffn_conv_b, ffn_w_down):
    batch, seq, d = x.shape
    t = batch * seq
    x = x.reshape(t, d)
    for i in range(DEPTH):
        kind, j = i % N_MIXERS, i // N_MIXERS
        gain = lambda n: norm_gains[i, n].reshape(1, d)
        if kind == 0:
            qk, vt = _sb_proj(x, gain(0), sb_w_qkv[j])
            o = _sb_attention(qk.reshape(batch, seq, -1), vt, batch, seq).reshape(t, -1)
            x = _proj_post(o, sb_w_o[j].astype(BF16), x, gain(1), "sb_out_proj")
        elif kind == 1:
            width = DIL_HEADS * HEAD_DIM
            w_groups = dil_w_qkv[j].reshape(d, 3, len(DIL_GROUPS), width).astype(BF16)
            outs, lses = [], []
            for grp, (window, dilation) in enumerate(DIL_GROUPS):
                n_back = window // dilation
                assert n_back == DIL_TILE and (seq // dilation) % DIL_TILE == 0
                qkv = _dil_proj(x, gain(0), w_groups[:, :, grp].reshape(d, 3 * width), dilation,
                                batch, seq, f"dil_qkv_proj_g{grp}")
                bias = _dil_bias(rel_bias, grp, dilation, n_back)
                o, lse = _dil_group_attention(qkv, bias, grp, dilation, batch, seq)
                if dilation > 1:
                    o, lse = _dil_token_order(o, lse, dilation, batch, seq,
                                              f"dil_token_order_g{grp}")
                outs.append(o)
                lses.append(lse)
            x = _dil_merge_proj(outs, lses, dil_w_o[j].astype(BF16), x, gain(1))
        else:
            q, k, vt = _mla_pre(x, gain(0), positions, mla_w_in[j], mla_q_norm[j], mla_w_qb[j],
                                mla_kv_norm[j], mla_w_kvb[j])
            o = _mla_attention(q.reshape(batch, seq, -1), k.reshape(batch, seq, -1),
                               vt, batch, seq).reshape(t, -1)
            x = _proj_post(o, mla_w_o[j].astype(BF16), x, gain(1), "mla_out_proj")
        x = _ffn(x, gain(2), gain(3), ffn_w_up[i], ffn_conv_w[i], ffn_conv_b[i], ffn_w_down[i], seq)
    return x.reshape(batch, seq, d)
```

```python
import functools
import math

import jax
import jax.numpy as jnp
from jax import lax
from jax.experimental import pallas as pl
from jax.experimental.pallas import tpu as pltpu

F32 = jnp.float32
BF16 = jnp.bfloat16

D_MODEL = 1024
DEPTH = 4
N_MIXERS = 3
EPS = 1e-6
NEG = -1e30

SB_HEADS = 16
HEAD_DIM = 64

DIL_GROUPS = ((128, 1), (512, 4), (2048, 16))
DIL_HEADS = 8
N_BUCKETS = 32
BUCKET_MAX_DIST = 2048

MLA_HEADS = 16
MLA_Q_RANK = 384
MLA_KV_RANK = 256
MLA_NOPE = 64
MLA_ROPE = 32
MLA_V = 64
ROPE_THETA = 10000.0

D_FF = 2816

LANES = 128
V7X_VMEM_CAP_BYTES = 56 * 1024 * 1024

ROW_TILE = 512
ATT_TILE = 256
DIL_TILE = 128
FF_CHUNK = 256
FFN_ROW_TILE = 1024
FFN_ROW_GROUPS = 8


def _vmem_limit(pipelined_bytes, resident_bytes):
    return int(min(V7X_VMEM_CAP_BYTES, 2 * pipelined_bytes + resident_bytes))


def _nbytes(shape, dtype):
    return math.prod(shape) * jnp.dtype(dtype).itemsize


def _rms_scale(x):
    return x * lax.rsqrt(jnp.mean(x * x, axis=-1, keepdims=True) + EPS)


def _dot(a, b):
    return jnp.dot(a, b, preferred_element_type=F32)


def _neg_abs(x):
    bits = lax.bitcast_convert_type(x, jnp.uint32) | jnp.uint32(0x80000000)
    return lax.bitcast_convert_type(bits, F32)


def _dot_nt(a, b):
    return lax.dot_general(a, b, (((1,), (1,)), ((), ())), preferred_element_type=F32)


def _proj_post_kernel(a_ref, w_ref, x_ref, g_ref, o_ref):
    m = _dot(a_ref[...], w_ref[...])
    o_ref[...] = x_ref[...] + _rms_scale(m) * g_ref[...]


def _proj_post(a, w, x, g, name):
    t, k = a.shape
    d = w.shape[1]
    limit = _vmem_limit(
        _nbytes((ROW_TILE, k), BF16) + _nbytes((k, d), BF16) + 2 * _nbytes((ROW_TILE, d), F32),
        2 * _nbytes((ROW_TILE, d), F32))
    return pl.pallas_call(
        _proj_post_kernel,
        out_shape=jax.ShapeDtypeStruct((t, d), F32),
        grid=(t // ROW_TILE,),
        in_specs=[pl.BlockSpec((ROW_TILE, k), lambda i: (i, 0)),
                  pl.BlockSpec((k, d), lambda i: (0, 0)),
                  pl.BlockSpec((ROW_TILE, d), lambda i: (i, 0)),
                  pl.BlockSpec((1, d), lambda i: (0, 0))],
        out_specs=pl.BlockSpec((ROW_TILE, d), lambda i: (i, 0)),
        compiler_params=pltpu.CompilerParams(dimension_semantics=("parallel",),
                                             vmem_limit_bytes=limit),
        name=name,
    )(a, w, x, g)


def _sb_proj_kernel(x_ref, g_ref, wqk_ref, wvt_ref, qk_ref, vt_ref, *, n_chunk):
    xn = (_rms_scale(x_ref[...]) * g_ref[...]).astype(BF16)
    for c in range(qk_ref.shape[1] // n_chunk):
        sl = slice(c * n_chunk, (c + 1) * n_chunk)
        qk_ref[:, sl] = _dot(xn, wqk_ref[:, sl]).astype(qk_ref.dtype)
    tk = vt_ref.shape[2]
    for c in range(wvt_ref.shape[0] // n_chunk):
        sl = slice(c * n_chunk, (c + 1) * n_chunk)
        vt = _dot_nt(wvt_ref[sl, :], xn).astype(vt_ref.dtype)
        for j in range(vt_ref.shape[0]):
            vt_ref[j, sl, :] = vt[:, j * tk:(j + 1) * tk]


def _sb_proj(x, g, w_qkv):
    t, d = x.shape
    width = SB_HEADS * HEAD_DIM
    wqk = w_qkv[:, :2 * width].astype(BF16)
    wvt = w_qkv[:, 2 * width:].T.astype(BF16)
    n_chunk = 512
    tk = ATT_TILE
    assert t % ROW_TILE == 0 and ROW_TILE % tk == 0 and width % n_chunk == 0
    limit = _vmem_limit(
        _nbytes((ROW_TILE, d), F32) + _nbytes((d, 3 * width), BF16)
        + _nbytes((ROW_TILE, 3 * width), BF16),
        _nbytes((ROW_TILE, d), F32) + 2 * _nbytes((ROW_TILE, n_chunk), F32))
    return pl.pallas_call(
        functools.partial(_sb_proj_kernel, n_chunk=n_chunk),
        out_shape=(jax.ShapeDtypeStruct((t, 2 * width), BF16),
                   jax.ShapeDtypeStruct((t // tk, width, tk), BF16)),
        grid=(t // ROW_TILE,),
        in_specs=[pl.BlockSpec((ROW_TILE, d), lambda i: (i, 0)),
                  pl.BlockSpec((1, d), lambda i: (0, 0)),
                  pl.BlockSpec((d, 2 * width), lambda i: (0, 0)),
                  pl.BlockSpec((width, d), lambda i: (0, 0))],
        out_specs=(pl.BlockSpec((ROW_TILE, 2 * width), lambda i: (i, 0)),
                   pl.BlockSpec((ROW_TILE // tk, width, tk), lambda i: (i, 0, 0))),
        compiler_params=pltpu.CompilerParams(dimension_semantics=("parallel",),
                                             vmem_limit_bytes=limit),
        name="sb_qkv_proj",
    )(x, g, wqk, wvt)


SB_UNDERFLOW = 105.0
SB_HEADS_PER_STEP = 8


def _sb_kernel(q_ref, k_ref, vt_ref, o_ref):
    tq = q_ref.shape[1]
    n_pairs = q_ref.shape[2] // LANES
    qi = pl.program_id(2)
    lane = lax.broadcasted_iota(jnp.int32, (tq, LANES), 1)
    first = lane < HEAD_DIM
    q_heads = []
    for p in range(n_pairs):
        q2 = q_ref[0, :, p * LANES:(p + 1) * LANES] * jnp.asarray(HEAD_DIM ** -0.5, BF16)
        zero = jnp.zeros_like(q2)
        q_heads.append(jnp.where(first, q2, zero))
        q_heads.append(jnp.where(first, zero, q2))
    heads = range(len(q_heads))

    key = lax.broadcasted_iota(jnp.int32, (tq, tq), 0)
    other = lax.broadcasted_iota(jnp.int32, (tq, tq), 1)
    later = jnp.where(other > key, 1.0, 0.0).astype(BF16)
    causal = key < other

    def all_heads(kb, st, mask):
        kblk = k_ref[0, pl.ds(pl.multiple_of(kb * tq, tq), tq), :]
        vt = vt_ref[kb]
        pair = lambda h: slice((h // 2) * LANES, (h // 2 + 1) * LANES)
        zs = [_dot_nt(kblk[:, pair(h)], q_heads[h]) for h in heads]
        mids = []
        for h in heads:
            z = zs[h]
            softplus = jnp.maximum(z, 0.0) + jnp.log(1.0 + jnp.exp(_neg_abs(z)))
            base = (z - softplus) - st[2 * h]
            if mask is not None:
                softplus = jnp.where(mask, softplus, 0.0)
            mids.append((softplus.astype(BF16), base, jnp.sum(softplus, axis=0, keepdims=True)))
        betweens = [_dot(later, sp) for sp, _, _ in mids]
        weights = []
        for h in heads:
            a = jnp.exp(mids[h][1] - betweens[h])
            if mask is not None:
                a = jnp.where(mask, a, 0.0)
            weights.append(a.astype(BF16))
        out = []
        for h in heads:
            out.append(st[2 * h] + mids[h][2])
            out.append(st[2 * h + 1] + _dot(vt[h * HEAD_DIM:(h + 1) * HEAD_DIM, :], weights[h]))
        return tuple(out)

    def smallest_carry(st):
        return functools.reduce(jnp.minimum, [jnp.min(c) for c in st[0::2]])

    init = (jnp.zeros((1, tq), F32), jnp.zeros((HEAD_DIM, tq), F32)) * len(q_heads)
    st = all_heads(qi, init, causal)

    def more(carry):
        n, low, _ = carry
        return jnp.logical_and(n < qi, low < SB_UNDERFLOW)

    def body(carry):
        n, _, st = carry
        st = all_heads(qi - 1 - n, st, None)
        return n + 1, smallest_carry(st), st

    _, _, st = lax.while_loop(more, body, (jnp.int32(0), smallest_carry(st), st))
    for p in range(n_pairs):
        pair_t = jnp.concatenate([st[4 * p + 1], st[4 * p + 3]], axis=0)
        o_ref[0, :, p * LANES:(p + 1) * LANES] = pair_t.T.astype(o_ref.dtype)


def _sb_attention(qk, vt, batch, seq):
    width = SB_HEADS_PER_STEP * HEAD_DIM
    groups = SB_HEADS // SB_HEADS_PER_STEP
    tq = ATT_TILE
    nb = seq // tq
    limit = _vmem_limit(
        2 * _nbytes((tq, width), BF16) + 2 * _nbytes((seq, width), BF16),
        12 * SB_HEADS_PER_STEP * _nbytes((tq, tq), F32))
    return pl.pallas_call(
        _sb_kernel,
        out_shape=jax.ShapeDtypeStruct((batch, seq, SB_HEADS * HEAD_DIM), BF16),
        grid=(batch, groups, nb),
        in_specs=[pl.BlockSpec((1, tq, width), lambda b, p, i: (b, i, p)),
                  pl.BlockSpec((1, seq, width), lambda b, p, i: (b, 0, groups + p)),
                  pl.BlockSpec((nb, width, tq), lambda b, p, i: (b, p, 0))],
        out_specs=pl.BlockSpec((1, tq, width), lambda b, p, i: (b, i, p)),
        compiler_params=pltpu.CompilerParams(
            dimension_semantics=("parallel", "parallel", "parallel"), vmem_limit_bytes=limit),
        name="sb_attention",
    )(qk, qk, vt)


PERM_BLOCK = 256


def _dil_proj_kernel(x_ref, g_ref, w_ref, o_ref, xp_ref, *, dilation, n_chunk):
    seq = x_ref.shape[0]
    run = PERM_BLOCK // dilation
    length = seq // dilation
    if dilation > 1:
        row = lax.broadcasted_iota(jnp.int32, (PERM_BLOCK, PERM_BLOCK), 0)
        col = lax.broadcasted_iota(jnp.int32, (PERM_BLOCK, PERM_BLOCK), 1)
        src = (row & (run - 1)) * dilation + lax.shift_right_logical(row, run.bit_length() - 1)
        perm = jnp.where(col == src, 1.0, 0.0).astype(BF16)
    for blk in range(seq // PERM_BLOCK):
        rows = slice(blk * PERM_BLOCK, (blk + 1) * PERM_BLOCK)
        xn = (_rms_scale(x_ref[rows, :]) * g_ref[...]).astype(BF16)
        if dilation == 1:
            xp_ref[rows, :] = xn
        else:
            moved = _dot(perm, xn).astype(BF16)
            for c in range(dilation):
                dst = c * length + blk * run
                xp_ref[dst:dst + run, :] = moved[c * run:(c + 1) * run, :]
    for c in range(w_ref.shape[1] // n_chunk):
        sl = slice(c * n_chunk, (c + 1) * n_chunk)
        for m in range(seq // ROW_TILE):
            rows = slice(m * ROW_TILE, (m + 1) * ROW_TILE)
            o_ref[rows, sl] = _dot(xp_ref[rows, :], w_ref[:, sl]).astype(o_ref.dtype)


def _dil_proj(x, g, w, dilation, batch, seq, name):
    d = x.shape[1]
    n = w.shape[1]
    n_chunk = 512
    assert seq % PERM_BLOCK == 0 and PERM_BLOCK % (16 * dilation) == 0 and n % n_chunk == 0
    limit = _vmem_limit(
        _nbytes((seq, d), F32) + _nbytes((d, n), BF16) + _nbytes((seq, n), BF16),
        _nbytes((seq, d), BF16) + 4 * _nbytes((ROW_TILE, d), F32))
    return pl.pallas_call(
        functools.partial(_dil_proj_kernel, dilation=dilation, n_chunk=n_chunk),
        out_shape=jax.ShapeDtypeStruct((batch * seq, n), BF16),
        grid=(batch,),
        in_specs=[pl.BlockSpec((seq, d), lambda b: (b, 0)),
                  pl.BlockSpec((1, d), lambda b: (0, 0)),
                  pl.BlockSpec((d, n), lambda b: (0, 0))],
        out_specs=pl.BlockSpec((seq, n), lambda b: (b, 0)),
        scratch_shapes=[pltpu.VMEM((seq, d), BF16)],
        compiler_params=pltpu.CompilerParams(dimension_semantics=("parallel",),
                                             vmem_limit_bytes=limit),
        name=name,
    )(x, g, w)


def _dil_kernel(q_ref, kp_ref, kc_ref, vp_ref, vc_ref, bias_ref, o_ref, lse_ref):
    w = q_ref.shape[1]
    ut = pl.program_id(2)
    lane = lax.broadcasted_iota(jnp.int32, (w, LANES), 1)
    first = lane < HEAD_DIM
    col = lax.broadcasted_iota(jnp.int32, (w, 2 * w), 1)
    key_ok = jnp.logical_or(col >= w, ut > 0)
    n_pairs = q_ref.shape[2] // LANES
    slab = lambda p: slice(p * LANES, (p + 1) * LANES)
    q_heads, k2, v2 = [], [], []
    for p in range(n_pairs):
        q2 = q_ref[0, :, slab(p)] * jnp.asarray(HEAD_DIM ** -0.5, BF16)
        zero = jnp.zeros_like(q2)
        q_heads += [jnp.where(first, q2, zero), jnp.where(first, zero, q2)]
        k2.append(jnp.concatenate([kp_ref[0, :, slab(p)], kc_ref[0, :, slab(p)]], axis=0))
        v2.append(jnp.concatenate([vp_ref[0, :, slab(p)], vc_ref[0, :, slab(p)]], axis=0))
    heads = range(2 * n_pairs)
    scores = [_dot_nt(q_heads[h], k2[h // 2]) for h in heads]
    probs, dens, lses = [], [], []
    for h in heads:
        s = jnp.where(key_ok, scores[h] + bias_ref[h], NEG)
        m = jnp.max(s, axis=-1, keepdims=True)
        e = jnp.exp(s - m)
        den = jnp.sum(e, axis=-1, keepdims=True)
        probs.append(e.astype(BF16))
        dens.append(den)
        lses.append(m + jnp.log(den))
    outs = [_dot(probs[h], v2[h // 2]) / dens[h] for h in heads]
    for p in range(n_pairs):
        o_ref[0, :, slab(p)] = jnp.where(first, outs[2 * p], outs[2 * p + 1]).astype(o_ref.dtype)
    lse = jnp.zeros((w, LANES), F32)
    for h in heads:
        lse = jnp.where(lane == h, lses[h], lse)
    lse_ref[0] = lse


def _dil_bias(rel_bias, group, dilation, n_back):
    w = n_back
    dist = jnp.arange(w + 1) * dilation
    max_exact = N_BUCKETS // 2
    d = jnp.maximum(dist.astype(F32), 1.0)
    large = max_exact + (jnp.log(d / max_exact) / math.log(BUCKET_MAX_DIST / max_exact)
                         * (N_BUCKETS - max_exact)).astype(jnp.int32)
    bucket = jnp.where(dist < max_exact, dist, jnp.minimum(large, N_BUCKETS - 1))
    per_m = rel_bias[:, group * DIL_HEADS:(group + 1) * DIL_HEADS][bucket].astype(F32).T
    period = jnp.concatenate(
        [per_m[:, ::-1], jnp.full((DIL_HEADS, w), NEG, F32)], axis=1)
    tiled = jnp.tile(period, (1, w))[:, :w * 2 * w]
    return tiled.reshape(DIL_HEADS, w, 2 * w)


def _dil_group_attention(qkv, bias, group, dilation, batch, seq):
    width = DIL_HEADS * HEAD_DIM
    length = seq // dilation
    w = DIL_TILE
    qv = qkv.reshape(batch * dilation, length, 3 * width)

    def cur(which):
        return pl.BlockSpec((1, w, width), lambda b, c, u: (b * dilation + c, u, which))

    def prev(which):
        return pl.BlockSpec((1, w, width),
                            lambda b, c, u: (b * dilation + c, jnp.maximum(u - 1, 0), which))

    at = lambda b, c, u: (b * dilation + c, u, 0)
    limit = _vmem_limit(
        5 * _nbytes((w, width), BF16) + _nbytes((DIL_HEADS, w, 2 * w), F32)
        + _nbytes((w, width), BF16) + _nbytes((w, LANES), F32),
        32 * _nbytes((w, 2 * w), F32))
    o, lse = pl.pallas_call(
        _dil_kernel,
        out_shape=(jax.ShapeDtypeStruct((batch * dilation, length, width), BF16),
                   jax.ShapeDtypeStruct((batch * dilation, length, LANES), F32)),
        grid=(batch, dilation, length // w),
        in_specs=[cur(0), prev(1), cur(1), prev(2), cur(2),
                  pl.BlockSpec((DIL_HEADS, w, 2 * w), lambda b, c, u: (0, 0, 0))],
        out_specs=(pl.BlockSpec((1, w, width), at), pl.BlockSpec((1, w, LANES), at)),
        compiler_params=pltpu.CompilerParams(
            dimension_semantics=("parallel", "parallel", "parallel"), vmem_limit_bytes=limit),
        name=f"dil_attention_g{group}",
    )(qv, qv, qv, qv, qv, bias)
    return o.reshape(batch * seq, width), lse.reshape(batch * seq, LANES)


def _dil_token_order_kernel(o_ref, l_ref, on_ref, ln_ref, *, dilation):
    seq = o_ref.shape[0]
    run = PERM_BLOCK // dilation
    length = seq // dilation
    row = lax.broadcasted_iota(jnp.int32, (PERM_BLOCK, PERM_BLOCK), 0)
    col = lax.broadcasted_iota(jnp.int32, (PERM_BLOCK, PERM_BLOCK), 1)
    src = (row & (dilation - 1)) * run + lax.shift_right_logical(row, dilation.bit_length() - 1)
    perm = jnp.where(col == src, 1.0, 0.0).astype(BF16)
    for blk in range(seq // PERM_BLOCK):
        gather = lambda ref: jnp.concatenate(
            [ref[c * length + blk * run:c * length + (blk + 1) * run, :] for c in range(dilation)],
            axis=0)
        rows = slice(blk * PERM_BLOCK, (blk + 1) * PERM_BLOCK)
        on_ref[rows, :] = _dot(perm, gather(o_ref)).astype(on_ref.dtype)
        l = gather(l_ref)
        hi = l.astype(BF16)
        rest = l - hi.astype(F32)
        mid = rest.astype(BF16)
        lo = (rest - mid.astype(F32)).astype(BF16)
        ln_ref[rows, :] = (_dot(perm, hi) + _dot(perm, mid)) + _dot(perm, lo)


def _dil_token_order(o, lse, dilation, batch, seq, name):
    width = o.shape[1]
    assert seq % PERM_BLOCK == 0 and PERM_BLOCK % (16 * dilation) == 0
    spec = lambda n: pl.BlockSpec((seq, n), lambda b: (b, 0))
    limit = _vmem_limit(2 * _nbytes((seq, width), BF16) + 2 * _nbytes((seq, LANES), F32),
                        8 * _nbytes((PERM_BLOCK, width), F32))
    return pl.pallas_call(
        functools.partial(_dil_token_order_kernel, dilation=dilation),
        out_shape=(jax.ShapeDtypeStruct(o.shape, o.dtype), jax.ShapeDtypeStruct(lse.shape, lse.dtype)),
        grid=(batch,),
        in_specs=[spec(width), spec(LANES)],
        out_specs=(spec(width), spec(LANES)),
        compiler_params=pltpu.CompilerParams(dimension_semantics=("parallel",),
                                             vmem_limit_bytes=limit),
        name=name,
    )(o, lse)


def _dil_merge_kernel(o0_ref, o1_ref, o2_ref, l0_ref, l1_ref, l2_ref, w_ref, x_ref, g_ref, out_ref):
    l0, l1, l2 = l0_ref[...], l1_ref[...], l2_ref[...]
    m = jnp.maximum(jnp.maximum(l0, l1), l2)
    e0, e1, e2 = jnp.exp(l0 - m), jnp.exp(l1 - m), jnp.exp(l2 - m)
    inv = 1.0 / (e0 + e1 + e2)
    width = o0_ref.shape[1]
    head = lax.broadcasted_iota(jnp.int32, (LANES, width), 0)
    lane = lax.broadcasted_iota(jnp.int32, (LANES, width), 1)
    owner = lax.shift_right_logical(lane, HEAD_DIM.bit_length() - 1)
    spread = jnp.where(owner == head, 1.0, 0.0).astype(BF16)

    def per_lane(wt):
        hi = wt.astype(BF16)
        lo = (wt - hi.astype(F32)).astype(BF16)
        return _dot(hi, spread) + _dot(lo, spread)

    o = (o0_ref[...].astype(F32) * per_lane(e0 * inv) + o1_ref[...].astype(F32) * per_lane(e1 * inv)
         + o2_ref[...].astype(F32) * per_lane(e2 * inv))
    mo = _dot(o.astype(BF16), w_ref[...])
    out_ref[...] = x_ref[...] + _rms_scale(mo) * g_ref[...]


def _dil_merge_proj(outs, lses, w, x, g):
    t, k = outs[0].shape
    d = w.shape[1]
    row = lambda i: (i, 0)
    limit = _vmem_limit(
        3 * _nbytes((ROW_TILE, k), BF16) + 3 * _nbytes((ROW_TILE, LANES), F32)
        + _nbytes((k, d), BF16) + 2 * _nbytes((ROW_TILE, d), F32),
        8 * _nbytes((ROW_TILE, k), F32) + 2 * _nbytes((ROW_TILE, d), F32))
    return pl.pallas_call(
        _dil_merge_kernel,
        out_shape=jax.ShapeDtypeStruct((t, d), F32),
        grid=(t // ROW_TILE,),
        in_specs=[pl.BlockSpec((ROW_TILE, k), row)] * 3 + [pl.BlockSpec((ROW_TILE, LANES), row)] * 3
        + [pl.BlockSpec((k, d), lambda i: (0, 0)),
           pl.BlockSpec((ROW_TILE, d), row),
           pl.BlockSpec((1, d), lambda i: (0, 0))],
        out_specs=pl.BlockSpec((ROW_TILE, d), row),
        compiler_params=pltpu.CompilerParams(dimension_semantics=("parallel",),
                                             vmem_limit_bytes=limit),
        name="dil_merge_proj",
    )(*outs, *lses, w, x, g)


def _mla_pre_kernel(x_ref, g_ref, wcq_ref, wckv_ref, wkr_ref, wkr_rot_ref, qn_ref, kvn_ref,
                    wq_ref, wq_rot_ref, wk_ref, wv_ref, cos_ref, sin_ref,
                    q_ref, k_ref, vt_ref, *, q_scale):
    xn = (_rms_scale(x_ref[...]) * g_ref[...]).astype(BF16)
    cq = (_rms_scale(_dot(xn, wcq_ref[...])) * qn_ref[...]).astype(BF16)
    ckv = (_rms_scale(_dot(xn, wckv_ref[...])) * kvn_ref[...]).astype(BF16)
    cos, sin = cos_ref[...], sin_ref[...]
    k_rope = _dot(xn, wkr_ref[...]) * cos + _dot(xn, wkr_rot_ref[...]) * sin
    cos2, sin2, k_rope2 = (jnp.concatenate([a, a], axis=1) for a in (cos, sin, k_rope))
    for p in range(MLA_HEADS // 2):
        sl = slice(2 * p * LANES, 2 * (p + 1) * LANES)
        q = _dot(cq, wq_ref[:, sl]) * cos2 + _dot(cq, wq_rot_ref[:, sl]) * sin2
        q_ref[:, sl] = (q * q_scale).astype(q_ref.dtype)
        k_ref[:, sl] = (_dot(ckv, wk_ref[:, sl]) + k_rope2).astype(k_ref.dtype)
    vt_ref[0] = _dot_nt(wv_ref[...], ckv).astype(vt_ref.dtype)


def _rope_rotation(w):
    half = MLA_ROPE // 2
    return jnp.concatenate([-w[..., half:], w[..., :half]], axis=-1)


def _pad_head_slabs(nope, rope):
    k, h = nope.shape[0], nope.shape[1]
    pad = jnp.zeros((k, h, LANES - MLA_NOPE - MLA_ROPE), nope.dtype)
    return jnp.concatenate([nope, rope, pad], axis=-1).reshape(k, h * LANES)


def _mla_pre(x, g, positions, w_in, q_norm, w_qb, kv_norm, w_kvb):
    t, d = x.shape
    h = MLA_HEADS
    w_cq = w_in[:, :MLA_Q_RANK].astype(BF16)
    w_ckv = w_in[:, MLA_Q_RANK:MLA_Q_RANK + MLA_KV_RANK].astype(BF16)
    w_kr = w_in[:, MLA_Q_RANK + MLA_KV_RANK:]
    zeros_nope = jnp.zeros((d, 1, MLA_NOPE), F32)
    w_kr_pad = _pad_head_slabs(zeros_nope, w_kr[:, None, :]).astype(BF16)
    w_kr_rot = _pad_head_slabs(zeros_nope, _rope_rotation(w_kr)[:, None, :]).astype(BF16)

    wq = w_qb.reshape(MLA_Q_RANK, h, MLA_NOPE + MLA_ROPE)
    wq_pad = _pad_head_slabs(wq[..., :MLA_NOPE], wq[..., MLA_NOPE:]).astype(BF16)
    wq_rot = _pad_head_slabs(jnp.zeros_like(wq[..., :MLA_NOPE]),
                             _rope_rotation(wq[..., MLA_NOPE:])).astype(BF16)
    wkv = w_kvb.reshape(MLA_KV_RANK, h, MLA_NOPE + MLA_V)
    wk_pad = _pad_head_slabs(wkv[..., :MLA_NOPE],
                             jnp.zeros((MLA_KV_RANK, h, MLA_ROPE), F32)).astype(BF16)
    wv_t = wkv[..., MLA_NOPE:].reshape(MLA_KV_RANK, h * MLA_V).T.astype(BF16)

    half = MLA_ROPE // 2
    freqs = ROPE_THETA ** (-jnp.arange(half, dtype=F32) / half)
    ang = positions.astype(F32).reshape(t, 1) * freqs
    cos, sin = jnp.cos(ang), jnp.sin(ang)
    tail = jnp.zeros((t, LANES - MLA_NOPE - MLA_ROPE), F32)
    cos_tab = jnp.concatenate([jnp.ones((t, MLA_NOPE), F32), cos, cos, tail], axis=-1)
    sin_tab = jnp.concatenate([jnp.zeros((t, MLA_NOPE), F32), sin, sin, tail], axis=-1)

    tm = ROW_TILE // 2
    full = lambda a: pl.BlockSpec(a.shape, lambda i: (0,) * a.ndim)
    row = lambda n: pl.BlockSpec((tm, n), lambda i: (i, 0))
    weights = (w_cq, w_ckv, w_kr_pad, w_kr_rot, q_norm.reshape(1, -1), kv_norm.reshape(1, -1),
               wq_pad, wq_rot, wk_pad, wv_t)
    assert tm == ATT_TILE
    limit = _vmem_limit(
        _nbytes((tm, d), F32) + sum(_nbytes(a.shape, a.dtype) for a in weights)
        + 2 * _nbytes((tm, LANES), F32) + 2 * _nbytes((tm, h * LANES), BF16)
        + _nbytes((tm, h * MLA_V), BF16),
        4 * _nbytes((tm, d), F32))
    return pl.pallas_call(
        functools.partial(_mla_pre_kernel, q_scale=(MLA_NOPE + MLA_ROPE) ** -0.5),
        out_shape=(jax.ShapeDtypeStruct((t, h * LANES), BF16),
                   jax.ShapeDtypeStruct((t, h * LANES), BF16),
                   jax.ShapeDtypeStruct((t // tm, h * MLA_V, tm), BF16)),
        grid=(t // tm,),
        in_specs=[row(d), full(g)] + [full(a) for a in weights] + [row(LANES), row(LANES)],
        out_specs=(row(h * LANES), row(h * LANES),
                   pl.BlockSpec((1, h * MLA_V, tm), lambda i: (i, 0, 0))),
        compiler_params=pltpu.CompilerParams(dimension_semantics=("parallel",),
                                             vmem_limit_bytes=limit),
        name="mla_pre",
    )(x, g, *weights, cos_tab, sin_tab)


def _mla_kernel(q_ref, k_ref, vt_ref, o_ref):
    tq = q_ref.shape[1]
    qi = pl.program_id(2)
    key = lax.broadcasted_iota(jnp.int32, (tq, tq), 0)
    query = lax.broadcasted_iota(jnp.int32, (tq, tq), 1)
    causal = key <= query
    n_heads = q_ref.shape[2] // LANES
    heads = range(n_heads)
    q_heads = [q_ref[0, :, h * LANES:(h + 1) * LANES] for h in heads]

    def all_heads(kb, st, mask):
        kblk = k_ref[0, pl.ds(pl.multiple_of(kb * tq, tq), tq), :]
        vt = vt_ref[kb]
        ss = [_dot_nt(kblk[:, h * LANES:(h + 1) * LANES], q_heads[h]) for h in heads]
        probs, stats = [], []
        for h in heads:
            s, (m, den) = ss[h], st[3 * h:3 * h + 2]
            if mask is not None:
                s = jnp.where(mask, s, NEG)
            m_new = jnp.maximum(m, jnp.max(s, axis=0, keepdims=True))
            alpha = jnp.exp(m - m_new)
            e = jnp.exp(s - m_new)
            stats.append((m_new, alpha * den + jnp.sum(e, axis=0, keepdims=True), alpha))
            probs.append(e.astype(BF16))
        out = []
        for h in heads:
            pv = _dot(vt[h * MLA_V:(h + 1) * MLA_V, :], probs[h])
            out.extend((stats[h][0], stats[h][1], stats[h][2] * st[3 * h + 2] + pv))
        return tuple(out)

    init = (jnp.full((1, tq), NEG, F32), jnp.zeros((1, tq), F32),
            jnp.zeros((MLA_V, tq), F32)) * n_heads
    st = all_heads(qi, init, causal)
    st = lax.fori_loop(0, qi, lambda n, st: all_heads(qi - 1 - n, st, None), st)
    for p in range(n_heads // 2):
        a, b = 2 * p, 2 * p + 1
        pair_t = jnp.concatenate([st[3 * a + 2] / st[3 * a + 1], st[3 * b + 2] / st[3 * b + 1]],
                                 axis=0)
        o_ref[0, :, p * LANES:(p + 1) * LANES] = pair_t.T.astype(o_ref.dtype)


MLA_HEADS_PER_STEP = 16


def _mla_attention(q, k, vt, batch, seq):
    nh = MLA_HEADS_PER_STEP
    steps = MLA_HEADS // nh
    tq = ATT_TILE
    nb = seq // tq
    limit = _vmem_limit(
        _nbytes((tq, nh * LANES), BF16) + _nbytes((seq, nh * LANES), BF16)
        + _nbytes((seq, nh * MLA_V), BF16) + _nbytes((tq, nh * MLA_V), BF16),
        8 * nh * _nbytes((tq, tq), F32))
    return pl.pallas_call(
        _mla_kernel,
        out_shape=jax.ShapeDtypeStruct((batch, seq, MLA_HEADS * MLA_V), BF16),
        grid=(batch, steps, nb),
        in_specs=[pl.BlockSpec((1, tq, nh * LANES), lambda b, p, i: (b, i, p)),
                  pl.BlockSpec((1, seq, nh * LANES), lambda b, p, i: (b, 0, p)),
                  pl.BlockSpec((nb, nh * MLA_V, tq), lambda b, p, i: (b, p, 0))],
        out_specs=pl.BlockSpec((1, tq, nh * MLA_V), lambda b, p, i: (b, i, p)),
        compiler_params=pltpu.CompilerParams(
            dimension_semantics=("parallel", "parallel", "parallel"), vmem_limit_bytes=limit),
        name="mla_attention",
    )(q, k, vt)


def _ffn_kernel(layer_ref, x_ref, g_in_ref, wg_ref, wu_ref, cwg_ref, cwu_ref, cbg_ref, cbu_ref,
                wd_ref, g_out_ref, o_ref, tail_g_ref, tail_u_ref, acc_ref, *, tiles_per_seq):
    del layer_ref
    tm = x_ref.shape[0]

    @pl.when(pl.program_id(0) % tiles_per_seq == 0)
    def _():
        tail_g_ref[...] = jnp.zeros_like(tail_g_ref)
        tail_u_ref[...] = jnp.zeros_like(tail_u_ref)

    x = x_ref[...]
    xn = (_rms_scale(x) * g_in_ref[...]).astype(BF16)
    row = lax.broadcasted_iota(jnp.int32, (tm, FF_CHUNK), 0)

    def conv(h, tail_ref, cw_ref, cb_ref, sl):
        p1 = tail_ref[7:8, sl]
        p2 = tail_ref[6:7, sl]
        h1 = jnp.where(row == 0, p1, pltpu.roll(h, 1, 0))
        h2 = jnp.where(row == 0, p2, jnp.where(row == 1, p1, pltpu.roll(h, 2, 0)))
        tail_ref[:, sl] = h[tm - 8:, :]
        return cw_ref[0:1, sl] * h2 + cw_ref[1:2, sl] * h1 + cw_ref[2:3, sl] * h + cb_ref[:, sl]

    chunk = lambda c: slice(c * FF_CHUNK, (c + 1) * FF_CHUNK)
    n_chunks = wg_ref.shape[1] // FF_CHUNK
    pending = None
    for c in range(n_chunks):
        groups = [slice(r * tm // FFN_ROW_GROUPS, (r + 1) * tm // FFN_ROW_GROUPS)
                  for r in range(FFN_ROW_GROUPS)]
        hg = jnp.concatenate([_dot(xn[rs], wg_ref[:, chunk(c)]) for rs in groups], axis=0)
        hu = jnp.concatenate([_dot(xn[rs], wu_ref[:, chunk(c)]) for rs in groups], axis=0)
        if pending is not None:
            part = _dot(pending[1], wd_ref[chunk(pending[0]), :])
            if pending[0] == 0:
                acc_ref[...] = part
            else:
                acc_ref[...] += part
        gate = conv(hg, tail_g_ref, cwg_ref, cbg_ref, chunk(c))
        val = conv(hu, tail_u_ref, cwu_ref, cbu_ref, chunk(c))
        pending = (c, (gate * (1.0 / (1.0 + jnp.exp(-gate))) * val).astype(BF16))
    m = acc_ref[...] + _dot(pending[1], wd_ref[chunk(pending[0]), :])
    o_ref[...] = x + _rms_scale(m) * g_out_ref[...]


def _ffn(x, layer, gains, w_up, conv_w, conv_b, w_down, seq):
    t, d = x.shape
    tm = FFN_ROW_TILE
    assert seq % tm == 0 and D_FF % FF_CHUNK == 0

    def resident(rows, cols, at):
        return pl.BlockSpec((None, rows, cols), lambda i, layer_ref: at(layer_ref[0]),
                            pipeline_mode=pl.Buffered(1))

    gain = lambda n: resident(1, d, lambda l: (4 * l + n, 0, 0))
    halves = lambda rows: [resident(rows, D_FF, lambda l, h=h: (l, 0, h)) for h in (0, 1)]
    row = pl.BlockSpec((tm, d), lambda i, layer_ref: (i, 0))
    limit = _vmem_limit(
        2 * _nbytes((tm, d), F32),
        3 * _nbytes((d, D_FF), BF16) + 8 * _nbytes((8, D_FF), F32)
        + 3 * _nbytes((tm, d), F32) + 16 * _nbytes((tm, FF_CHUNK), F32))
    return pl.pallas_call(
        functools.partial(_ffn_kernel, tiles_per_seq=seq // tm),
        out_shape=jax.ShapeDtypeStruct((t, d), F32),
        grid_spec=pltpu.PrefetchScalarGridSpec(
            num_scalar_prefetch=1,
            grid=(t // tm,),
            in_specs=[row, gain(2)] + halves(d) + halves(3) + halves(1)
            + [resident(D_FF, d, lambda l: (l, 0, 0)), gain(3)],
            out_specs=row,
            scratch_shapes=[pltpu.VMEM((8, D_FF), F32), pltpu.VMEM((8, D_FF), F32),
                            pltpu.VMEM((tm, d), F32)]),
        compiler_params=pltpu.CompilerParams(dimension_semantics=("arbitrary",),
                                             vmem_limit_bytes=limit),
        name="conv_ffn",
    )(jnp.full((1,), layer, jnp.int32), x, gains, w_up, w_up, conv_w, conv_w, conv_b, conv_b,
      w_down, gains)


def kernel(x, positions, rel_bias, norm_gains, sb_w_qkv, sb_w_o, dil_w_qkv, dil_w_o, mla_w_in,
           mla_q_norm, mla_w_qb, mla_kv_norm, mla_w_kvb, mla_w_o, ffn_w_up, ffn_conv_w,
           ffn_conv_b, ffn_w_down):
    batch, seq, d = x.shape
    t = batch * seq
    x = x.reshape(t, d)
    gains3 = norm_gains.reshape(4 * DEPTH, 1, d)
    w_up_bf16, w_down_bf16 = ffn_w_up.astype(BF16), ffn_w_down.astype(BF16)
    conv_b3 = ffn_conv_b.reshape(DEPTH, 1, 2 * D_FF)
    for i in range(DEPTH):
        kind, j = i % N_MIXERS, i // N_MIXERS
        gain = lambda n: norm_gains[i, n].reshape(1, d)
        if kind == 0:
            qk, vt = _sb_proj(x, gain(0), sb_w_qkv[j])
            o = _sb_attention(qk.reshape(batch, seq, -1), vt, batch, seq).reshape(t, -1)
            x = _proj_post(o, sb_w_o[j].astype(BF16), x, gain(1), "sb_out_proj")
        elif kind == 1:
            width = DIL_HEADS * HEAD_DIM
            w_groups = dil_w_qkv[j].reshape(d, 3, len(DIL_GROUPS), width).astype(BF16)
            outs, lses = [], []
            for grp, (window, dilation) in enumerate(DIL_GROUPS):
                n_back = window // dilation
                assert n_back == DIL_TILE and (seq // dilation) % DIL_TILE == 0
                qkv = _dil_proj(x, gain(0), w_groups[:, :, grp].reshape(d, 3 * width), dilation,
                                batch, seq, f"dil_qkv_proj_g{grp}")
                bias = _dil_bias(rel_bias, grp, dilation, n_back)
                o, lse = _dil_group_attention(qkv, bias, grp, dilation, batch, seq)
                if dilation > 1:
                    o, lse = _dil_token_order(o, lse, dilation, batch, seq,
                                              f"dil_token_order_g{grp}")
                outs.append(o)
                lses.append(lse)
            x = _dil_merge_proj(outs, lses, dil_w_o[j].astype(BF16), x, gain(1))
        else:
            q, k, vt = _mla_pre(x, gain(0), positions, mla_w_in[j], mla_q_norm[j], mla_w_qb[j],
                                mla_kv_norm[j], mla_w_kvb[j])
            o = _mla_attention(q.reshape(batch, seq, -1), k.reshape(batch, seq, -1),
                               vt, batch, seq).reshape(t, -1)
            x = _proj_post(o, mla_w_o[j].astype(BF16), x, gain(1), "mla_out_proj")
        x = _ffn(x, i, gains3, w_up_bf16, ffn_conv_w, conv_b3, w_down_bf16, seq)
    return x.reshape(batch, seq, d)
```

```python
import functools
import math

import jax
import jax.numpy as jnp
from jax import lax
from jax.experimental import pallas as pl
from jax.experimental.pallas import tpu as pltpu

F32 = jnp.float32
BF16 = jnp.bfloat16

D_MODEL = 1024
DEPTH = 4
N_MIXERS = 3
EPS = 1e-6
NEG = -1e30

SB_HEADS = 16
HEAD_DIM = 64

DIL_GROUPS = ((128, 1), (512, 4), (2048, 16))
DIL_HEADS = 8
N_BUCKETS = 32
BUCKET_MAX_DIST = 2048

MLA_HEADS = 16
MLA_Q_RANK = 384
MLA_KV_RANK = 256
MLA_NOPE = 64
MLA_ROPE = 32
MLA_V = 64
ROPE_THETA = 10000.0

D_FF = 2816

LANES = 128
V7X_VMEM_CAP_BYTES = 56 * 1024 * 1024

ROW_TILE = 512
ATT_TILE = 256
DIL_TILE = 128
FF_CHUNK = 256
FFN_ROW_TILE = 1024
FFN_ROW_GROUPS = 8


def _vmem_limit(pipelined_bytes, resident_bytes):
    return int(min(V7X_VMEM_CAP_BYTES, 2 * pipelined_bytes + resident_bytes))


def _nbytes(shape, dtype):
    return math.prod(shape) * jnp.dtype(dtype).itemsize


def _rms_scale(x):
    return x * lax.rsqrt(jnp.mean(x * x, axis=-1, keepdims=True) + EPS)


def _dot(a, b):
    return jnp.dot(a, b, preferred_element_type=F32)


def _neg_abs(x):
    bits = lax.bitcast_convert_type(x, jnp.uint32) | jnp.uint32(0x80000000)
    return lax.bitcast_convert_type(bits, F32)


def _dot_nt(a, b):
    return lax.dot_general(a, b, (((1,), (1,)), ((), ())), preferred_element_type=F32)


PROJ_ROW_GROUPS = 4


def _proj_post_kernel(a_ref, w_ref, x_ref, g_ref, o_ref):
    tm = a_ref.shape[0]
    for r in range(PROJ_ROW_GROUPS):
        rows = slice(r * tm // PROJ_ROW_GROUPS, (r + 1) * tm // PROJ_ROW_GROUPS)
        m = _dot(a_ref[rows, :], w_ref[...])
        o_ref[rows, :] = x_ref[rows, :] + _rms_scale(m) * g_ref[...]


def _proj_post(a, w, x, g, name):
    t, k = a.shape
    d = w.shape[1]
    limit = _vmem_limit(
        _nbytes((ROW_TILE, k), BF16) + _nbytes((k, d), BF16) + 2 * _nbytes((ROW_TILE, d), F32),
        2 * _nbytes((ROW_TILE, d), F32))
    return pl.pallas_call(
        _proj_post_kernel,
        out_shape=jax.ShapeDtypeStruct((t, d), F32),
        grid=(t // ROW_TILE,),
        in_specs=[pl.BlockSpec((ROW_TILE, k), lambda i: (i, 0)),
                  pl.BlockSpec((k, d), lambda i: (0, 0)),
                  pl.BlockSpec((ROW_TILE, d), lambda i: (i, 0)),
                  pl.BlockSpec((1, d), lambda i: (0, 0))],
        out_specs=pl.BlockSpec((ROW_TILE, d), lambda i: (i, 0)),
        compiler_params=pltpu.CompilerParams(dimension_semantics=("parallel",),
                                             vmem_limit_bytes=limit),
        name=name,
    )(a, w, x, g)


def _sb_proj_kernel(x_ref, g_ref, wqk_ref, wvt_ref, qk_ref, vt_ref, *, n_chunk):
    xn = (_rms_scale(x_ref[...]) * g_ref[...]).astype(BF16)
    for c in range(qk_ref.shape[1] // n_chunk):
        sl = slice(c * n_chunk, (c + 1) * n_chunk)
        qk_ref[:, sl] = _dot(xn, wqk_ref[:, sl]).astype(qk_ref.dtype)
    tk = vt_ref.shape[2]
    for c in range(wvt_ref.shape[0] // n_chunk):
        sl = slice(c * n_chunk, (c + 1) * n_chunk)
        vt = _dot_nt(wvt_ref[sl, :], xn).astype(vt_ref.dtype)
        for j in range(vt_ref.shape[0]):
            vt_ref[j, sl, :] = vt[:, j * tk:(j + 1) * tk]


def _sb_proj(x, g, w_qkv):
    t, d = x.shape
    width = SB_HEADS * HEAD_DIM
    wqk = w_qkv[:, :2 * width].astype(BF16)
    wvt = w_qkv[:, 2 * width:].T.astype(BF16)
    n_chunk = 512
    tk = SB_KEY_BLOCK
    assert t % ROW_TILE == 0 and ROW_TILE % tk == 0 and width % n_chunk == 0
    limit = _vmem_limit(
        _nbytes((ROW_TILE, d), F32) + _nbytes((d, 3 * width), BF16)
        + _nbytes((ROW_TILE, 3 * width), BF16),
        _nbytes((ROW_TILE, d), F32) + 2 * _nbytes((ROW_TILE, n_chunk), F32))
    return pl.pallas_call(
        functools.partial(_sb_proj_kernel, n_chunk=n_chunk),
        out_shape=(jax.ShapeDtypeStruct((t, 2 * width), BF16),
                   jax.ShapeDtypeStruct((t // tk, width, tk), BF16)),
        grid=(t // ROW_TILE,),
        in_specs=[pl.BlockSpec((ROW_TILE, d), lambda i: (i, 0)),
                  pl.BlockSpec((1, d), lambda i: (0, 0)),
                  pl.BlockSpec((d, 2 * width), lambda i: (0, 0)),
                  pl.BlockSpec((width, d), lambda i: (0, 0))],
        out_specs=(pl.BlockSpec((ROW_TILE, 2 * width), lambda i: (i, 0)),
                   pl.BlockSpec((ROW_TILE // tk, width, tk), lambda i: (i, 0, 0))),
        compiler_params=pltpu.CompilerParams(dimension_semantics=("parallel",),
                                             vmem_limit_bytes=limit),
        name="sb_qkv_proj",
    )(x, g, wqk, wvt)


SB_UNDERFLOW = 105.0
SB_HEADS_PER_STEP = 8
SB_KEY_BLOCK = 256


def _sb_kernel(q_ref, k_ref, vt_ref, o_ref):
    tq = q_ref.shape[1]
    n_pairs = q_ref.shape[2] // LANES
    qi = pl.program_id(2)
    lane = lax.broadcasted_iota(jnp.int32, (tq, LANES), 1)
    first = lane < HEAD_DIM
    q_heads = []
    for p in range(n_pairs):
        q2 = q_ref[0, :, p * LANES:(p + 1) * LANES] * jnp.asarray(HEAD_DIM ** -0.5, BF16)
        zero = jnp.zeros_like(q2)
        q_heads.append(jnp.where(first, q2, zero))
        q_heads.append(jnp.where(first, zero, q2))
    heads = range(len(q_heads))

    tk = vt_ref.shape[2]
    per_tile = tq // tk
    key = lax.broadcasted_iota(jnp.int32, (tk, tk), 0)
    other = lax.broadcasted_iota(jnp.int32, (tk, tk), 1)
    later = jnp.where(other > key, 1.0, 0.0).astype(BF16)
    key = lax.broadcasted_iota(jnp.int32, (tk, tq), 0)
    query = lax.broadcasted_iota(jnp.int32, (tk, tq), 1)

    def all_heads(kb, st, mask):
        kblk = k_ref[0, pl.ds(pl.multiple_of(kb * tk, tk), tk), :]
        vt = vt_ref[kb]
        pair = lambda h: slice((h // 2) * LANES, (h // 2 + 1) * LANES)
        zs = [_dot_nt(kblk[:, pair(h)], q_heads[h]) for h in heads]
        mids = []
        for h in heads:
            z = zs[h]
            softplus = jnp.maximum(z, 0.0) + jnp.log(1.0 + jnp.exp(_neg_abs(z)))
            base = (z - softplus) - st[2 * h]
            if mask is not None:
                softplus = jnp.where(mask, softplus, 0.0)
            mids.append((softplus.astype(BF16), base, jnp.sum(softplus, axis=0, keepdims=True)))
        betweens = [_dot(later, sp) for sp, _, _ in mids]
        weights = []
        for h in heads:
            a = jnp.exp(mids[h][1] - betweens[h])
            if mask is not None:
                a = jnp.where(mask, a, 0.0)
            weights.append(a.astype(BF16))
        out = []
        for h in heads:
            out.append(st[2 * h] + mids[h][2])
            out.append(st[2 * h + 1] + _dot(vt[h * HEAD_DIM:(h + 1) * HEAD_DIM, :], weights[h]))
        return tuple(out)

    def smallest_carry(st):
        return functools.reduce(jnp.minimum, [jnp.min(c) for c in st[0::2]])

    st = (jnp.zeros((1, tq), F32), jnp.zeros((HEAD_DIM, tq), F32)) * len(q_heads)
    for j in reversed(range(per_tile)):
        st = all_heads(qi * per_tile + j, st, key + j * tk < query)
    n_before = qi * per_tile

    def more(carry):
        n, low, _ = carry
        return jnp.logical_and(n < n_before, low < SB_UNDERFLOW)

    def body(carry):
        n, _, st = carry
        st = all_heads(n_before - 1 - n, st, None)
        return n + 1, smallest_carry(st), st

    _, _, st = lax.while_loop(more, body, (jnp.int32(0), smallest_carry(st), st))
    for p in range(n_pairs):
        pair_t = jnp.concatenate([st[4 * p + 1], st[4 * p + 3]], axis=0)
        o_ref[0, :, p * LANES:(p + 1) * LANES] = pair_t.T.astype(o_ref.dtype)


def _sb_attention(qk, vt, batch, seq):
    width = SB_HEADS_PER_STEP * HEAD_DIM
    groups = SB_HEADS // SB_HEADS_PER_STEP
    tq, tk = ATT_TILE, SB_KEY_BLOCK
    assert tq % tk == 0 and seq % tq == 0
    limit = _vmem_limit(
        2 * _nbytes((tq, width), BF16) + 2 * _nbytes((seq, width), BF16),
        12 * SB_HEADS_PER_STEP * _nbytes((tq, tq), F32))
    return pl.pallas_call(
        _sb_kernel,
        out_shape=jax.ShapeDtypeStruct((batch, seq, SB_HEADS * HEAD_DIM), BF16),
        grid=(batch, groups, seq // tq),
        in_specs=[pl.BlockSpec((1, tq, width), lambda b, p, i: (b, i, p)),
                  pl.BlockSpec((1, seq, width), lambda b, p, i: (b, 0, groups + p)),
                  pl.BlockSpec((seq // tk, width, tk), lambda b, p, i: (b, p, 0))],
        out_specs=pl.BlockSpec((1, tq, width), lambda b, p, i: (b, i, p)),
        compiler_params=pltpu.CompilerParams(
            dimension_semantics=("parallel", "parallel", "parallel"), vmem_limit_bytes=limit),
        name="sb_attention",
    )(qk, qk, vt)


PERM_BLOCK = 256


def _dil_proj_kernel(x_ref, g_ref, w_ref, o_ref, xp_ref, *, dilation, n_chunk):
    seq = x_ref.shape[0]
    run = PERM_BLOCK // dilation
    length = seq // dilation
    if dilation > 1:
        row = lax.broadcasted_iota(jnp.int32, (PERM_BLOCK, PERM_BLOCK), 0)
        col = lax.broadcasted_iota(jnp.int32, (PERM_BLOCK, PERM_BLOCK), 1)
        src = (row & (run - 1)) * dilation + lax.shift_right_logical(row, run.bit_length() - 1)
        perm = jnp.where(col == src, 1.0, 0.0).astype(BF16)
    for blk in range(seq // PERM_BLOCK):
        rows = slice(blk * PERM_BLOCK, (blk + 1) * PERM_BLOCK)
        xn = (_rms_scale(x_ref[rows, :]) * g_ref[...]).astype(BF16)
        if dilation == 1:
            xp_ref[rows, :] = xn
        else:
            moved = _dot(perm, xn).astype(BF16)
            for c in range(dilation):
                dst = c * length + blk * run
                xp_ref[dst:dst + run, :] = moved[c * run:(c + 1) * run, :]
    for c in range(w_ref.shape[1] // n_chunk):
        sl = slice(c * n_chunk, (c + 1) * n_chunk)
        for m in range(seq // ROW_TILE):
            rows = slice(m * ROW_TILE, (m + 1) * ROW_TILE)
            o_ref[rows, sl] = _dot(xp_ref[rows, :], w_ref[:, sl]).astype(o_ref.dtype)


def _dil_proj(x, g, w, dilation, batch, seq, name):
    d = x.shape[1]
    n = w.shape[1]
    n_chunk = 512
    assert seq % PERM_BLOCK == 0 and PERM_BLOCK % (16 * dilation) == 0 and n % n_chunk == 0
    limit = _vmem_limit(
        _nbytes((seq, d), F32) + _nbytes((d, n), BF16) + _nbytes((seq, n), BF16),
        _nbytes((seq, d), BF16) + 4 * _nbytes((ROW_TILE, d), F32))
    return pl.pallas_call(
        functools.partial(_dil_proj_kernel, dilation=dilation, n_chunk=n_chunk),
        out_shape=jax.ShapeDtypeStruct((batch * seq, n), BF16),
        grid=(batch,),
        in_specs=[pl.BlockSpec((seq, d), lambda b: (b, 0)),
                  pl.BlockSpec((1, d), lambda b: (0, 0)),
                  pl.BlockSpec((d, n), lambda b: (0, 0))],
        out_specs=pl.BlockSpec((seq, n), lambda b: (b, 0)),
        scratch_shapes=[pltpu.VMEM((seq, d), BF16)],
        compiler_params=pltpu.CompilerParams(dimension_semantics=("parallel",),
                                             vmem_limit_bytes=limit),
        name=name,
    )(x, g, w)


def _dil_kernel(q_ref, kp_ref, kc_ref, vp_ref, vc_ref, bias_ref, o_ref, lse_ref):
    w = q_ref.shape[1]
    ut = pl.program_id(2)
    lane = lax.broadcasted_iota(jnp.int32, (w, LANES), 1)
    first = lane < HEAD_DIM
    col = lax.broadcasted_iota(jnp.int32, (w, 2 * w), 1)
    key_ok = jnp.logical_or(col >= w, ut > 0)
    n_pairs = q_ref.shape[2] // LANES
    slab = lambda p: slice(p * LANES, (p + 1) * LANES)
    q_heads, k2, v2 = [], [], []
    for p in range(n_pairs):
        q2 = q_ref[0, :, slab(p)] * jnp.asarray(HEAD_DIM ** -0.5, BF16)
        zero = jnp.zeros_like(q2)
        q_heads += [jnp.where(first, q2, zero), jnp.where(first, zero, q2)]
        k2.append(jnp.concatenate([kp_ref[0, :, slab(p)], kc_ref[0, :, slab(p)]], axis=0))
        v2.append(jnp.concatenate([vp_ref[0, :, slab(p)], vc_ref[0, :, slab(p)]], axis=0))
    heads = range(2 * n_pairs)
    scores = [_dot_nt(q_heads[h], k2[h // 2]) for h in heads]
    probs, dens, lses = [], [], []
    for h in heads:
        s = jnp.where(key_ok, scores[h] + bias_ref[h], NEG)
        m = jnp.max(s, axis=-1, keepdims=True)
        e = jnp.exp(s - m)
        den = jnp.sum(e, axis=-1, keepdims=True)
        probs.append(e.astype(BF16))
        dens.append(den)
        lses.append(m + jnp.log(den))
    outs = [_dot(probs[h], v2[h // 2]) / dens[h] for h in heads]
    for p in range(n_pairs):
        o_ref[0, :, slab(p)] = jnp.where(first, outs[2 * p], outs[2 * p + 1]).astype(o_ref.dtype)
    lse = jnp.zeros((w, LANES), F32)
    for h in heads:
        lse = jnp.where(lane == h, lses[h], lse)
    lse_ref[0] = lse


def _dil_bias(rel_bias, group, dilation, n_back):
    w = n_back
    dist = jnp.arange(w + 1) * dilation
    max_exact = N_BUCKETS // 2
    d = jnp.maximum(dist.astype(F32), 1.0)
    large = max_exact + (jnp.log(d / max_exact) / math.log(BUCKET_MAX_DIST / max_exact)
                         * (N_BUCKETS - max_exact)).astype(jnp.int32)
    bucket = jnp.where(dist < max_exact, dist, jnp.minimum(large, N_BUCKETS - 1))
    per_m = rel_bias[:, group * DIL_HEADS:(group + 1) * DIL_HEADS][bucket].astype(F32).T
    period = jnp.concatenate(
        [per_m[:, ::-1], jnp.full((DIL_HEADS, w), NEG, F32)], axis=1)
    tiled = jnp.tile(period, (1, w))[:, :w * 2 * w]
    return tiled.reshape(DIL_HEADS, w, 2 * w)


def _dil_group_attention(qkv, bias, group, dilation, batch, seq):
    width = DIL_HEADS * HEAD_DIM
    length = seq // dilation
    w = DIL_TILE
    qv = qkv.reshape(batch * dilation, length, 3 * width)

    def cur(which):
        return pl.BlockSpec((1, w, width), lambda b, c, u: (b * dilation + c, u, which))

    def prev(which):
        return pl.BlockSpec((1, w, width),
                            lambda b, c, u: (b * dilation + c, jnp.maximum(u - 1, 0), which))

    at = lambda b, c, u: (b * dilation + c, u, 0)
    limit = _vmem_limit(
        5 * _nbytes((w, width), BF16) + _nbytes((DIL_HEADS, w, 2 * w), F32)
        + _nbytes((w, width), BF16) + _nbytes((w, LANES), F32),
        32 * _nbytes((w, 2 * w), F32))
    o, lse = pl.pallas_call(
        _dil_kernel,
        out_shape=(jax.ShapeDtypeStruct((batch * dilation, length, width), BF16),
                   jax.ShapeDtypeStruct((batch * dilation, length, LANES), F32)),
        grid=(batch, dilation, length // w),
        in_specs=[cur(0), prev(1), cur(1), prev(2), cur(2),
                  pl.BlockSpec((DIL_HEADS, w, 2 * w), lambda b, c, u: (0, 0, 0))],
        out_specs=(pl.BlockSpec((1, w, width), at), pl.BlockSpec((1, w, LANES), at)),
        compiler_params=pltpu.CompilerParams(
            dimension_semantics=("parallel", "parallel", "parallel"), vmem_limit_bytes=limit),
        name=f"dil_attention_g{group}",
    )(qv, qv, qv, qv, qv, bias)
    return o.reshape(batch * seq, width), lse.reshape(batch * seq, LANES)


def _dil_token_order_kernel(o_ref, l_ref, on_ref, ln_ref, *, dilation):
    seq = o_ref.shape[0]
    run = PERM_BLOCK // dilation
    length = seq // dilation
    row = lax.broadcasted_iota(jnp.int32, (PERM_BLOCK, PERM_BLOCK), 0)
    col = lax.broadcasted_iota(jnp.int32, (PERM_BLOCK, PERM_BLOCK), 1)
    src = (row & (dilation - 1)) * run + lax.shift_right_logical(row, dilation.bit_length() - 1)
    perm = jnp.where(col == src, 1.0, 0.0).astype(BF16)
    for blk in range(seq // PERM_BLOCK):
        gather = lambda ref: jnp.concatenate(
            [ref[c * length + blk * run:c * length + (blk + 1) * run, :] for c in range(dilation)],
            axis=0)
        rows = slice(blk * PERM_BLOCK, (blk + 1) * PERM_BLOCK)
        on_ref[rows, :] = _dot(perm, gather(o_ref)).astype(on_ref.dtype)
        l = gather(l_ref)
        hi = l.astype(BF16)
        rest = l - hi.astype(F32)
        mid = rest.astype(BF16)
        lo = (rest - mid.astype(F32)).astype(BF16)
        ln_ref[rows, :] = (_dot(perm, hi) + _dot(perm, mid)) + _dot(perm, lo)


def _dil_token_order(o, lse, dilation, batch, seq, name):
    width = o.shape[1]
    assert seq % PERM_BLOCK == 0 and PERM_BLOCK % (16 * dilation) == 0
    spec = lambda n: pl.BlockSpec((seq, n), lambda b: (b, 0))
    limit = _vmem_limit(2 * _nbytes((seq, width), BF16) + 2 * _nbytes((seq, LANES), F32),
                        8 * _nbytes((PERM_BLOCK, width), F32))
    return pl.pallas_call(
        functools.partial(_dil_token_order_kernel, dilation=dilation),
        out_shape=(jax.ShapeDtypeStruct(o.shape, o.dtype), jax.ShapeDtypeStruct(lse.shape, lse.dtype)),
        grid=(batch,),
        in_specs=[spec(width), spec(LANES)],
        out_specs=(spec(width), spec(LANES)),
        compiler_params=pltpu.CompilerParams(dimension_semantics=("parallel",),
                                             vmem_limit_bytes=limit),
        name=name,
    )(o, lse)


def _dil_merge_kernel(o0_ref, o1_ref, o2_ref, l0_ref, l1_ref, l2_ref, w_ref, x_ref, g_ref, out_ref):
    l0, l1, l2 = l0_ref[...], l1_ref[...], l2_ref[...]
    m = jnp.maximum(jnp.maximum(l0, l1), l2)
    e0, e1, e2 = jnp.exp(l0 - m), jnp.exp(l1 - m), jnp.exp(l2 - m)
    inv = 1.0 / (e0 + e1 + e2)
    width = o0_ref.shape[1]
    head = lax.broadcasted_iota(jnp.int32, (LANES, width), 0)
    lane = lax.broadcasted_iota(jnp.int32, (LANES, width), 1)
    owner = lax.shift_right_logical(lane, HEAD_DIM.bit_length() - 1)
    spread = jnp.where(owner == head, 1.0, 0.0).astype(BF16)

    def per_lane(wt):
        hi = wt.astype(BF16)
        lo = (wt - hi.astype(F32)).astype(BF16)
        return _dot(hi, spread) + _dot(lo, spread)

    o = (o0_ref[...].astype(F32) * per_lane(e0 * inv) + o1_ref[...].astype(F32) * per_lane(e1 * inv)
         + o2_ref[...].astype(F32) * per_lane(e2 * inv)).astype(BF16)
    tm = o.shape[0]
    for r in range(PROJ_ROW_GROUPS):
        rows = slice(r * tm // PROJ_ROW_GROUPS, (r + 1) * tm // PROJ_ROW_GROUPS)
        mo = _dot(o[rows], w_ref[...])
        out_ref[rows, :] = x_ref[rows, :] + _rms_scale(mo) * g_ref[...]


def _dil_merge_proj(outs, lses, w, x, g):
    t, k = outs[0].shape
    d = w.shape[1]
    row = lambda i: (i, 0)
    limit = _vmem_limit(
        3 * _nbytes((ROW_TILE, k), BF16) + 3 * _nbytes((ROW_TILE, LANES), F32)
        + _nbytes((k, d), BF16) + 2 * _nbytes((ROW_TILE, d), F32),
        8 * _nbytes((ROW_TILE, k), F32) + 2 * _nbytes((ROW_TILE, d), F32))
    return pl.pallas_call(
        _dil_merge_kernel,
        out_shape=jax.ShapeDtypeStruct((t, d), F32),
        grid=(t // ROW_TILE,),
        in_specs=[pl.BlockSpec((ROW_TILE, k), row)] * 3 + [pl.BlockSpec((ROW_TILE, LANES), row)] * 3
        + [pl.BlockSpec((k, d), lambda i: (0, 0)),
           pl.BlockSpec((ROW_TILE, d), row),
           pl.BlockSpec((1, d), lambda i: (0, 0))],
        out_specs=pl.BlockSpec((ROW_TILE, d), row),
        compiler_params=pltpu.CompilerParams(dimension_semantics=("parallel",),
                                             vmem_limit_bytes=limit),
        name="dil_merge_proj",
    )(*outs, *lses, w, x, g)


def _mla_pre_kernel(x_ref, g_ref, wcq_ref, wckv_ref, wkr_ref, wkr_rot_ref, qn_ref, kvn_ref,
                    wq_ref, wq_rot_ref, wk_ref, wv_ref, cos_ref, sin_ref,
                    q_ref, k_ref, vt_ref, *, q_scale):
    xn = (_rms_scale(x_ref[...]) * g_ref[...]).astype(BF16)
    cq = (_rms_scale(_dot(xn, wcq_ref[...])) * qn_ref[...]).astype(BF16)
    ckv = (_rms_scale(_dot(xn, wckv_ref[...])) * kvn_ref[...]).astype(BF16)
    cos, sin = cos_ref[...], sin_ref[...]
    k_rope = _dot(xn, wkr_ref[...]) * cos + _dot(xn, wkr_rot_ref[...]) * sin
    cos2, sin2, k_rope2 = (jnp.concatenate([a, a], axis=1) for a in (cos, sin, k_rope))
    for p in range(MLA_HEADS // 2):
        sl = slice(2 * p * LANES, 2 * (p + 1) * LANES)
        q = _dot(cq, wq_ref[:, sl]) * cos2 + _dot(cq, wq_rot_ref[:, sl]) * sin2
        q_ref[:, sl] = (q * q_scale).astype(q_ref.dtype)
        k_ref[:, sl] = (_dot(ckv, wk_ref[:, sl]) + k_rope2).astype(k_ref.dtype)
    vt_ref[0] = _dot_nt(wv_ref[...], ckv).astype(vt_ref.dtype)


def _rope_rotation(w):
    half = MLA_ROPE // 2
    return jnp.concatenate([-w[..., half:], w[..., :half]], axis=-1)


def _pad_head_slabs(nope, rope):
    k, h = nope.shape[0], nope.shape[1]
    pad = jnp.zeros((k, h, LANES - MLA_NOPE - MLA_ROPE), nope.dtype)
    return jnp.concatenate([nope, rope, pad], axis=-1).reshape(k, h * LANES)


def _mla_pre(x, g, positions, w_in, q_norm, w_qb, kv_norm, w_kvb):
    t, d = x.shape
    h = MLA_HEADS
    w_cq = w_in[:, :MLA_Q_RANK].astype(BF16)
    w_ckv = w_in[:, MLA_Q_RANK:MLA_Q_RANK + MLA_KV_RANK].astype(BF16)
    w_kr = w_in[:, MLA_Q_RANK + MLA_KV_RANK:]
    zeros_nope = jnp.zeros((d, 1, MLA_NOPE), F32)
    w_kr_pad = _pad_head_slabs(zeros_nope, w_kr[:, None, :]).astype(BF16)
    w_kr_rot = _pad_head_slabs(zeros_nope, _rope_rotation(w_kr)[:, None, :]).astype(BF16)

    wq = w_qb.reshape(MLA_Q_RANK, h, MLA_NOPE + MLA_ROPE)
    wq_pad = _pad_head_slabs(wq[..., :MLA_NOPE], wq[..., MLA_NOPE:]).astype(BF16)
    wq_rot = _pad_head_slabs(jnp.zeros_like(wq[..., :MLA_NOPE]),
                             _rope_rotation(wq[..., MLA_NOPE:])).astype(BF16)
    wkv = w_kvb.reshape(MLA_KV_RANK, h, MLA_NOPE + MLA_V)
    wk_pad = _pad_head_slabs(wkv[..., :MLA_NOPE],
                             jnp.zeros((MLA_KV_RANK, h, MLA_ROPE), F32)).astype(BF16)
    wv_t = wkv[..., MLA_NOPE:].reshape(MLA_KV_RANK, h * MLA_V).T.astype(BF16)

    half = MLA_ROPE // 2
    freqs = ROPE_THETA ** (-jnp.arange(half, dtype=F32) / half)
    ang = positions.astype(F32).reshape(t, 1) * freqs
    dense = ang.reshape(t * half // LANES, LANES)
    cos, sin = jnp.cos(dense).reshape(t, half), jnp.sin(dense).reshape(t, half)
    tail = jnp.zeros((t, LANES - MLA_NOPE - MLA_ROPE), F32)
    cos_tab = jnp.concatenate([jnp.ones((t, MLA_NOPE), F32), cos, cos, tail], axis=-1)
    sin_tab = jnp.concatenate([jnp.zeros((t, MLA_NOPE), F32), sin, sin, tail], axis=-1)

    tm = ROW_TILE // 2
    full = lambda a: pl.BlockSpec(a.shape, lambda i: (0,) * a.ndim)
    row = lambda n: pl.BlockSpec((tm, n), lambda i: (i, 0))
    weights = (w_cq, w_ckv, w_kr_pad, w_kr_rot, q_norm.reshape(1, -1), kv_norm.reshape(1, -1),
               wq_pad, wq_rot, wk_pad, wv_t)
    assert tm == ATT_TILE
    limit = _vmem_limit(
        _nbytes((tm, d), F32) + sum(_nbytes(a.shape, a.dtype) for a in weights)
        + 2 * _nbytes((tm, LANES), F32) + 2 * _nbytes((tm, h * LANES), BF16)
        + _nbytes((tm, h * MLA_V), BF16),
        4 * _nbytes((tm, d), F32))
    return pl.pallas_call(
        functools.partial(_mla_pre_kernel, q_scale=(MLA_NOPE + MLA_ROPE) ** -0.5),
        out_shape=(jax.ShapeDtypeStruct((t, h * LANES), BF16),
                   jax.ShapeDtypeStruct((t, h * LANES), BF16),
                   jax.ShapeDtypeStruct((t // tm, h * MLA_V, tm), BF16)),
        grid=(t // tm,),
        in_specs=[row(d), full(g)] + [full(a) for a in weights] + [row(LANES), row(LANES)],
        out_specs=(row(h * LANES), row(h * LANES),
                   pl.BlockSpec((1, h * MLA_V, tm), lambda i: (i, 0, 0))),
        compiler_params=pltpu.CompilerParams(dimension_semantics=("parallel",),
                                             vmem_limit_bytes=limit),
        name="mla_pre",
    )(x, g, *weights, cos_tab, sin_tab)


def _mla_kernel(q_ref, k_ref, vt_ref, o_ref):
    tq = q_ref.shape[1]
    qi = pl.program_id(2)
    key = lax.broadcasted_iota(jnp.int32, (tq, tq), 0)
    query = lax.broadcasted_iota(jnp.int32, (tq, tq), 1)
    causal = key <= query
    n_heads = q_ref.shape[2] // LANES
    heads = range(n_heads)
    q_heads = [q_ref[0, :, h * LANES:(h + 1) * LANES] for h in heads]

    def all_heads(kb, st, mask):
        kblk = k_ref[0, pl.ds(pl.multiple_of(kb * tq, tq), tq), :]
        vt = vt_ref[kb]
        ss = [_dot_nt(kblk[:, h * LANES:(h + 1) * LANES], q_heads[h]) for h in heads]
        probs, stats = [], []
        for h in heads:
            s, (m, den) = ss[h], st[3 * h:3 * h + 2]
            if mask is not None:
                s = jnp.where(mask, s, NEG)
            m_new = jnp.maximum(m, jnp.max(s, axis=0, keepdims=True))
            alpha = jnp.exp(m - m_new)
            e = jnp.exp(s - m_new)
            stats.append((m_new, alpha * den + jnp.sum(e, axis=0, keepdims=True), alpha))
            probs.append(e.astype(BF16))
        out = []
        for h in heads:
            pv = _dot(vt[h * MLA_V:(h + 1) * MLA_V, :], probs[h])
            out.extend((stats[h][0], stats[h][1], stats[h][2] * st[3 * h + 2] + pv))
        return tuple(out)

    init = (jnp.full((1, tq), NEG, F32), jnp.zeros((1, tq), F32),
            jnp.zeros((MLA_V, tq), F32)) * n_heads
    st = all_heads(qi, init, causal)
    st = lax.fori_loop(0, qi, lambda n, st: all_heads(qi - 1 - n, st, None), st)
    for p in range(n_heads // 2):
        a, b = 2 * p, 2 * p + 1
        pair_t = jnp.concatenate([st[3 * a + 2] / st[3 * a + 1], st[3 * b + 2] / st[3 * b + 1]],
                                 axis=0)
        o_ref[0, :, p * LANES:(p + 1) * LANES] = pair_t.T.astype(o_ref.dtype)


MLA_HEADS_PER_STEP = 16


def _mla_attention(q, k, vt, batch, seq):
    nh = MLA_HEADS_PER_STEP
    steps = MLA_HEADS // nh
    tq = ATT_TILE
    nb = seq // tq
    limit = _vmem_limit(
        _nbytes((tq, nh * LANES), BF16) + _nbytes((seq, nh * LANES), BF16)
        + _nbytes((seq, nh * MLA_V), BF16) + _nbytes((tq, nh * MLA_V), BF16),
        8 * nh * _nbytes((tq, tq), F32))
    return pl.pallas_call(
        _mla_kernel,
        out_shape=jax.ShapeDtypeStruct((batch, seq, MLA_HEADS * MLA_V), BF16),
        grid=(batch, steps, nb),
        in_specs=[pl.BlockSpec((1, tq, nh * LANES), lambda b, p, i: (b, i, p)),
                  pl.BlockSpec((1, seq, nh * LANES), lambda b, p, i: (b, 0, p)),
                  pl.BlockSpec((nb, nh * MLA_V, tq), lambda b, p, i: (b, p, 0))],
        out_specs=pl.BlockSpec((1, tq, nh * MLA_V), lambda b, p, i: (b, i, p)),
        compiler_params=pltpu.CompilerParams(
            dimension_semantics=("parallel", "parallel", "parallel"), vmem_limit_bytes=limit),
        name="mla_attention",
    )(q, k, vt)


def _ffn_kernel(layer_ref, x_ref, g_in_ref, wg_ref, wu_ref, cwg_ref, cwu_ref, cbg_ref, cbu_ref,
                wd_ref, g_out_ref, o_ref, tail_g_ref, tail_u_ref, acc_ref, *, tiles_per_seq):
    del layer_ref
    tm = x_ref.shape[0]

    @pl.when(pl.program_id(0) % tiles_per_seq == 0)
    def _():
        tail_g_ref[...] = jnp.zeros_like(tail_g_ref)
        tail_u_ref[...] = jnp.zeros_like(tail_u_ref)

    x = x_ref[...]
    xn = (_rms_scale(x) * g_in_ref[...]).astype(BF16)
    row = lax.broadcasted_iota(jnp.int32, (tm, FF_CHUNK), 0)

    def conv(h, tail_ref, cw_ref, cb_ref, sl):
        p1 = tail_ref[7:8, sl]
        p2 = tail_ref[6:7, sl]
        h1 = jnp.where(row == 0, p1, pltpu.roll(h, 1, 0))
        h2 = jnp.where(row == 0, p2, jnp.where(row == 1, p1, pltpu.roll(h, 2, 0)))
        tail_ref[:, sl] = h[tm - 8:, :]
        return cw_ref[0:1, sl] * h2 + cw_ref[1:2, sl] * h1 + cw_ref[2:3, sl] * h + cb_ref[:, sl]

    chunk = lambda c: slice(c * FF_CHUNK, (c + 1) * FF_CHUNK)
    n_chunks = wg_ref.shape[1] // FF_CHUNK
    pending = None
    for c in range(n_chunks):
        groups = [slice(r * tm // FFN_ROW_GROUPS, (r + 1) * tm // FFN_ROW_GROUPS)
                  for r in range(FFN_ROW_GROUPS)]
        hg = jnp.concatenate([_dot(xn[rs], wg_ref[:, chunk(c)]) for rs in groups], axis=0)
        hu = jnp.concatenate([_dot(xn[rs], wu_ref[:, chunk(c)]) for rs in groups], axis=0)
        if pending is not None:
            part = _dot(pending[1], wd_ref[chunk(pending[0]), :])
            if pending[0] == 0:
                acc_ref[...] = part
            else:
                acc_ref[...] += part
        gate = conv(hg, tail_g_ref, cwg_ref, cbg_ref, chunk(c))
        val = conv(hu, tail_u_ref, cwu_ref, cbu_ref, chunk(c))
        pending = (c, (gate * (1.0 / (1.0 + jnp.exp(-gate))) * val).astype(BF16))
    m = acc_ref[...] + _dot(pending[1], wd_ref[chunk(pending[0]), :])
    o_ref[...] = x + _rms_scale(m) * g_out_ref[...]


def _ffn(x, layer, gains, w_up, conv_w, conv_b, w_down, seq):
    t, d = x.shape
    tm = FFN_ROW_TILE
    assert seq % tm == 0 and D_FF % FF_CHUNK == 0

    def resident(rows, cols, at):
        return pl.BlockSpec((None, rows, cols), lambda i, layer_ref: at(layer_ref[0]),
                            pipeline_mode=pl.Buffered(1))

    gain = lambda n: resident(1, d, lambda l: (4 * l + n, 0, 0))
    halves = lambda rows: [resident(rows, D_FF, lambda l, h=h: (l, 0, h)) for h in (0, 1)]
    row = pl.BlockSpec((tm, d), lambda i, layer_ref: (i, 0))
    limit = _vmem_limit(
        2 * _nbytes((tm, d), F32),
        3 * _nbytes((d, D_FF), BF16) + 8 * _nbytes((8, D_FF), F32)
        + 3 * _nbytes((tm, d), F32) + 16 * _nbytes((tm, FF_CHUNK), F32))
    return pl.pallas_call(
        functools.partial(_ffn_kernel, tiles_per_seq=seq // tm),
        out_shape=jax.ShapeDtypeStruct((t, d), F32),
        grid_spec=pltpu.PrefetchScalarGridSpec(
            num_scalar_prefetch=1,
            grid=(t // tm,),
            in_specs=[row, gain(2)] + halves(d) + halves(3) + halves(1)
            + [resident(D_FF, d, lambda l: (l, 0, 0)), gain(3)],
            out_specs=row,
            scratch_shapes=[pltpu.VMEM((8, D_FF), F32), pltpu.VMEM((8, D_FF), F32),
                            pltpu.VMEM((tm, d), F32)]),
        compiler_params=pltpu.CompilerParams(dimension_semantics=("arbitrary",),
                                             vmem_limit_bytes=limit),
        name="conv_ffn",
    )(jnp.full((1,), layer, jnp.int32), x, gains, w_up, w_up, conv_w, conv_w, conv_b, conv_b,
      w_down, gains)


def kernel(x, positions, rel_bias, norm_gains, sb_w_qkv, sb_w_o, dil_w_qkv, dil_w_o, mla_w_in,
           mla_q_norm, mla_w_qb, mla_kv_norm, mla_w_kvb, mla_w_o, ffn_w_up, ffn_conv_w,
           ffn_conv_b, ffn_w_down):
    batch, seq, d = x.shape
    t = batch * seq
    x = x.reshape(t, d)
    gains3 = norm_gains.reshape(4 * DEPTH, 1, d)
    w_up_bf16, w_down_bf16 = ffn_w_up.astype(BF16), ffn_w_down.astype(BF16)
    conv_b3 = ffn_conv_b.reshape(DEPTH, 1, 2 * D_FF)
    for i in range(DEPTH):
        kind, j = i % N_MIXERS, i // N_MIXERS
        gain = lambda n: norm_gains[i, n].reshape(1, d)
        if kind == 0:
            qk, vt = _sb_proj(x, gain(0), sb_w_qkv[j])
            o = _sb_attention(qk.reshape(batch, seq, -1), vt, batch, seq).reshape(t, -1)
            x = _proj_post(o, sb_w_o[j].astype(BF16), x, gain(1), "sb_out_proj")
        elif kind == 1:
            width = DIL_HEADS * HEAD_DIM
            w_groups = dil_w_qkv[j].reshape(d, 3, len(DIL_GROUPS), width).astype(BF16)
            outs, lses = [], []
            for grp, (window, dilation) in enumerate(DIL_GROUPS):
                n_back = window // dilation
                assert n_back == DIL_TILE and (seq // dilation) % DIL_TILE == 0
                qkv = _dil_proj(x, gain(0), w_groups[:, :, grp].reshape(d, 3 * width), dilation,
                                batch, seq, f"dil_qkv_proj_g{grp}")
                bias = _dil_bias(rel_bias, grp, dilation, n_back)
                o, lse = _dil_group_attention(qkv, bias, grp, dilation, batch, seq)
                if dilation > 1:
                    o, lse = _dil_token_order(o, lse, dilation, batch, seq,
                                              f"dil_token_order_g{grp}")
                outs.append(o)
                lses.append(lse)
            x = _dil_merge_proj(outs, lses, dil_w_o[j].astype(BF16), x, gain(1))
        else:
            q, k, vt = _mla_pre(x, gain(0), positions, mla_w_in[j], mla_q_norm[j], mla_w_qb[j],
                                mla_kv_norm[j], mla_w_kvb[j])
            o = _mla_attention(q.reshape(batch, seq, -1), k.reshape(batch, seq, -1),
                               vt, batch, seq).reshape(t, -1)
            x = _proj_post(o, mla_w_o[j].astype(BF16), x, gain(1), "mla_out_proj")
        x = _ffn(x, i, gains3, w_up_bf16, ffn_conv_w, conv_b3, w_down_bf16, seq)
    return x.reshape(batch, seq, d)
```

```python
import functools
import math

import jax
import jax.numpy as jnp
from jax import lax
from jax.experimental import pallas as pl
from jax.experimental.pallas import tpu as pltpu

F32 = jnp.float32
BF16 = jnp.bfloat16

D_MODEL = 1024
DEPTH = 4
N_MIXERS = 3
EPS = 1e-6
NEG = -1e30

SB_HEADS = 16
HEAD_DIM = 64

DIL_GROUPS = ((128, 1), (512, 4), (2048, 16))
DIL_HEADS = 8
N_BUCKETS = 32
BUCKET_MAX_DIST = 2048

MLA_HEADS = 16
MLA_Q_RANK = 384
MLA_KV_RANK = 256
MLA_NOPE = 64
MLA_ROPE = 32
MLA_V = 64
ROPE_THETA = 10000.0

D_FF = 2816

LANES = 128
V7X_VMEM_CAP_BYTES = 56 * 1024 * 1024

ROW_TILE = 512
ATT_TILE = 256
DIL_TILE = 128
FF_CHUNK = 256
FFN_ROW_TILE = 1024
FFN_ROW_GROUPS = 8


def _vmem_limit(pipelined_bytes, resident_bytes):
    return int(min(V7X_VMEM_CAP_BYTES, 2 * pipelined_bytes + resident_bytes))


def _nbytes(shape, dtype):
    return math.prod(shape) * jnp.dtype(dtype).itemsize


def _rms_scale(x):
    return x * lax.rsqrt(jnp.mean(x * x, axis=-1, keepdims=True) + EPS)


def _dot(a, b):
    return jnp.dot(a, b, preferred_element_type=F32)


def _neg_abs(x):
    bits = lax.bitcast_convert_type(x, jnp.uint32) | jnp.uint32(0x80000000)
    return lax.bitcast_convert_type(bits, F32)


def _dot_nt(a, b):
    return lax.dot_general(a, b, (((1,), (1,)), ((), ())), preferred_element_type=F32)


def _proj_post_kernel(a_ref, w_ref, x_ref, g_ref, o_ref):
    m = _dot(a_ref[...], w_ref[...])
    o_ref[...] = x_ref[...] + _rms_scale(m) * g_ref[...]


def _proj_post(a, w, x, g, name):
    t, k = a.shape
    d = w.shape[1]
    limit = _vmem_limit(
        _nbytes((ROW_TILE, k), BF16) + _nbytes((k, d), BF16) + 2 * _nbytes((ROW_TILE, d), F32),
        2 * _nbytes((ROW_TILE, d), F32))
    return pl.pallas_call(
        _proj_post_kernel,
        out_shape=jax.ShapeDtypeStruct((t, d), F32),
        grid=(t // ROW_TILE,),
        in_specs=[pl.BlockSpec((ROW_TILE, k), lambda i: (i, 0)),
                  pl.BlockSpec((k, d), lambda i: (0, 0)),
                  pl.BlockSpec((ROW_TILE, d), lambda i: (i, 0)),
                  pl.BlockSpec((1, d), lambda i: (0, 0))],
        out_specs=pl.BlockSpec((ROW_TILE, d), lambda i: (i, 0)),
        compiler_params=pltpu.CompilerParams(dimension_semantics=("parallel",),
                                             vmem_limit_bytes=limit),
        name=name,
    )(a, w, x, g)


def _sb_proj_kernel(x_ref, g_ref, wqk_ref, wvt_ref, qk_ref, vt_ref, *, n_chunk):
    xn = (_rms_scale(x_ref[...]) * g_ref[...]).astype(BF16)
    for c in range(qk_ref.shape[1] // n_chunk):
        sl = slice(c * n_chunk, (c + 1) * n_chunk)
        qk_ref[:, sl] = _dot(xn, wqk_ref[:, sl]).astype(qk_ref.dtype)
    tk = vt_ref.shape[2]
    for c in range(wvt_ref.shape[0] // n_chunk):
        sl = slice(c * n_chunk, (c + 1) * n_chunk)
        vt = _dot_nt(wvt_ref[sl, :], xn).astype(vt_ref.dtype)
        for j in range(vt_ref.shape[0]):
            vt_ref[j, sl, :] = vt[:, j * tk:(j + 1) * tk]


def _sb_proj(x, g, w_qkv):
    t, d = x.shape
    width = SB_HEADS * HEAD_DIM
    wqk = w_qkv[:, :2 * width].astype(BF16)
    wvt = w_qkv[:, 2 * width:].T.astype(BF16)
    n_chunk = 512
    tk = ATT_TILE
    assert t % ROW_TILE == 0 and ROW_TILE % tk == 0 and width % n_chunk == 0
    limit = _vmem_limit(
        _nbytes((ROW_TILE, d), F32) + _nbytes((d, 3 * width), BF16)
        + _nbytes((ROW_TILE, 3 * width), BF16),
        _nbytes((ROW_TILE, d), F32) + 2 * _nbytes((ROW_TILE, n_chunk), F32))
    return pl.pallas_call(
        functools.partial(_sb_proj_kernel, n_chunk=n_chunk),
        out_shape=(jax.ShapeDtypeStruct((t, 2 * width), BF16),
                   jax.ShapeDtypeStruct((t // tk, width, tk), BF16)),
        grid=(t // ROW_TILE,),
        in_specs=[pl.BlockSpec((ROW_TILE, d), lambda i: (i, 0)),
                  pl.BlockSpec((1, d), lambda i: (0, 0)),
                  pl.BlockSpec((d, 2 * width), lambda i: (0, 0)),
                  pl.BlockSpec((width, d), lambda i: (0, 0))],
        out_specs=(pl.BlockSpec((ROW_TILE, 2 * width), lambda i: (i, 0)),
                   pl.BlockSpec((ROW_TILE // tk, width, tk), lambda i: (i, 0, 0))),
        compiler_params=pltpu.CompilerParams(dimension_semantics=("parallel",),
                                             vmem_limit_bytes=limit),
        name="sb_qkv_proj",
    )(x, g, wqk, wvt)


SB_UNDERFLOW = 105.0
SB_HEADS_PER_STEP = 16


def _sb_kernel(q_ref, k_ref, vt_ref, o_ref):
    tq = q_ref.shape[1]
    n_pairs = q_ref.shape[2] // LANES
    qi = pl.program_id(2)
    lane = lax.broadcasted_iota(jnp.int32, (tq, LANES), 1)
    first = lane < HEAD_DIM
    q_heads = []
    for p in range(n_pairs):
        q2 = q_ref[0, :, p * LANES:(p + 1) * LANES] * jnp.asarray(HEAD_DIM ** -0.5, BF16)
        zero = jnp.zeros_like(q2)
        q_heads.append(jnp.where(first, q2, zero))
        q_heads.append(jnp.where(first, zero, q2))
    heads = range(len(q_heads))

    key = lax.broadcasted_iota(jnp.int32, (tq, tq), 0)
    other = lax.broadcasted_iota(jnp.int32, (tq, tq), 1)
    later = jnp.where(other > key, 1.0, 0.0).astype(BF16)
    causal = key < other

    def all_heads(kb, st, mask):
        kblk = k_ref[0, pl.ds(pl.multiple_of(kb * tq, tq), tq), :]
        vt = vt_ref[kb]
        pair = lambda h: slice((h // 2) * LANES, (h // 2 + 1) * LANES)
        zs = [_dot_nt(kblk[:, pair(h)], q_heads[h]) for h in heads]
        mids = []
        for h in heads:
            z = zs[h]
            softplus = jnp.maximum(z, 0.0) + jnp.log(1.0 + jnp.exp(_neg_abs(z)))
            base = (z - softplus) - st[2 * h]
            if mask is not None:
                softplus = jnp.where(mask, softplus, 0.0)
            mids.append((softplus.astype(BF16), base, jnp.sum(softplus, axis=0, keepdims=True)))
        betweens = [_dot(later, sp) for sp, _, _ in mids]
        weights = []
        for h in heads:
            a = jnp.exp(mids[h][1] - betweens[h])
            if mask is not None:
                a = jnp.where(mask, a, 0.0)
            weights.append(a.astype(BF16))
        out = []
        for h in heads:
            out.append(st[2 * h] + mids[h][2])
            out.append(st[2 * h + 1] + _dot(vt[h * HEAD_DIM:(h + 1) * HEAD_DIM, :], weights[h]))
        return tuple(out)

    def smallest_carry(st):
        return functools.reduce(jnp.minimum, [jnp.min(c) for c in st[0::2]])

    init = (jnp.zeros((1, tq), F32), jnp.zeros((HEAD_DIM, tq), F32)) * len(q_heads)
    st = all_heads(qi, init, causal)

    def more(carry):
        n, low, _ = carry
        return jnp.logical_and(n < qi, low < SB_UNDERFLOW)

    def body(carry):
        n, _, st = carry
        st = all_heads(qi - 1 - n, st, None)
        return n + 1, smallest_carry(st), st

    _, _, st = lax.while_loop(more, body, (jnp.int32(0), smallest_carry(st), st))
    for p in range(n_pairs):
        pair_t = jnp.concatenate([st[4 * p + 1], st[4 * p + 3]], axis=0)
        o_ref[0, :, p * LANES:(p + 1) * LANES] = pair_t.T.astype(o_ref.dtype)


def _sb_attention(qk, vt, batch, seq):
    width = SB_HEADS_PER_STEP * HEAD_DIM
    groups = SB_HEADS // SB_HEADS_PER_STEP
    tq = ATT_TILE
    nb = seq // tq
    limit = _vmem_limit(
        2 * _nbytes((tq, width), BF16) + 2 * _nbytes((seq, width), BF16),
        12 * SB_HEADS_PER_STEP * _nbytes((tq, tq), F32))
    return pl.pallas_call(
        _sb_kernel,
        out_shape=jax.ShapeDtypeStruct((batch, seq, SB_HEADS * HEAD_DIM), BF16),
        grid=(batch, groups, nb),
        in_specs=[pl.BlockSpec((1, tq, width), lambda b, p, i: (b, i, p)),
                  pl.BlockSpec((1, seq, width), lambda b, p, i: (b, 0, groups + p)),
                  pl.BlockSpec((nb, width, tq), lambda b, p, i: (b, p, 0))],
        out_specs=pl.BlockSpec((1, tq, width), lambda b, p, i: (b, i, p)),
        compiler_params=pltpu.CompilerParams(
            dimension_semantics=("parallel", "parallel", "parallel"), vmem_limit_bytes=limit),
        name="sb_attention",
    )(qk, qk, vt)


PERM_BLOCK = 256


def _dil_proj_kernel(x_ref, g_ref, w_ref, o_ref, xp_ref, *, dilation, n_chunk):
    seq = x_ref.shape[0]
    run = PERM_BLOCK // dilation
    length = seq // dilation
    if dilation > 1:
        row = lax.broadcasted_iota(jnp.int32, (PERM_BLOCK, PERM_BLOCK), 0)
        col = lax.broadcasted_iota(jnp.int32, (PERM_BLOCK, PERM_BLOCK), 1)
        src = (row & (run - 1)) * dilation + lax.shift_right_logical(row, run.bit_length() - 1)
        perm = jnp.where(col == src, 1.0, 0.0).astype(BF16)
    for blk in range(seq // PERM_BLOCK):
        rows = slice(blk * PERM_BLOCK, (blk + 1) * PERM_BLOCK)
        xn = (_rms_scale(x_ref[rows, :]) * g_ref[...]).astype(BF16)
        if dilation == 1:
            xp_ref[rows, :] = xn
        else:
            moved = _dot(perm, xn).astype(BF16)
            for c in range(dilation):
                dst = c * length + blk * run
                xp_ref[dst:dst + run, :] = moved[c * run:(c + 1) * run, :]
    for c in range(w_ref.shape[1] // n_chunk):
        sl = slice(c * n_chunk, (c + 1) * n_chunk)
        for m in range(seq // ROW_TILE):
            rows = slice(m * ROW_TILE, (m + 1) * ROW_TILE)
            o_ref[rows, sl] = _dot(xp_ref[rows, :], w_ref[:, sl]).astype(o_ref.dtype)


def _dil_proj(x, g, w, dilation, batch, seq, name):
    d = x.shape[1]
    n = w.shape[1]
    n_chunk = 512
    assert seq % PERM_BLOCK == 0 and PERM_BLOCK % (16 * dilation) == 0 and n % n_chunk == 0
    limit = _vmem_limit(
        _nbytes((seq, d), F32) + _nbytes((d, n), BF16) + _nbytes((seq, n), BF16),
        _nbytes((seq, d), BF16) + 4 * _nbytes((ROW_TILE, d), F32))
    return pl.pallas_call(
        functools.partial(_dil_proj_kernel, dilation=dilation, n_chunk=n_chunk),
        out_shape=jax.ShapeDtypeStruct((batch * seq, n), BF16),
        grid=(batch,),
        in_specs=[pl.BlockSpec((seq, d), lambda b: (b, 0)),
                  pl.BlockSpec((1, d), lambda b: (0, 0)),
                  pl.BlockSpec((d, n), lambda b: (0, 0))],
        out_specs=pl.BlockSpec((seq, n), lambda b: (b, 0)),
        scratch_shapes=[pltpu.VMEM((seq, d), BF16)],
        compiler_params=pltpu.CompilerParams(dimension_semantics=("parallel",),
                                             vmem_limit_bytes=limit),
        name=name,
    )(x, g, w)


def _dil_kernel(q_ref, kp_ref, kc_ref, vp_ref, vc_ref, bias_ref, o_ref, lse_ref):
    w = q_ref.shape[1]
    ut = pl.program_id(2)
    lane = lax.broadcasted_iota(jnp.int32, (w, LANES), 1)
    first = lane < HEAD_DIM
    col = lax.broadcasted_iota(jnp.int32, (w, 2 * w), 1)
    key_ok = jnp.logical_or(col >= w, ut > 0)
    n_pairs = q_ref.shape[2] // LANES
    slab = lambda p: slice(p * LANES, (p + 1) * LANES)
    q_heads, k2, v2 = [], [], []
    for p in range(n_pairs):
        q2 = q_ref[0, :, slab(p)] * jnp.asarray(HEAD_DIM ** -0.5, BF16)
        zero = jnp.zeros_like(q2)
        q_heads += [jnp.where(first, q2, zero), jnp.where(first, zero, q2)]
        k2.append(jnp.concatenate([kp_ref[0, :, slab(p)], kc_ref[0, :, slab(p)]], axis=0))
        v2.append(jnp.concatenate([vp_ref[0, :, slab(p)], vc_ref[0, :, slab(p)]], axis=0))
    heads = range(2 * n_pairs)
    scores = [_dot_nt(q_heads[h], k2[h // 2]) for h in heads]
    probs, dens, lses = [], [], []
    for h in heads:
        s = jnp.where(key_ok, scores[h] + bias_ref[h], NEG)
        m = jnp.max(s, axis=-1, keepdims=True)
        e = jnp.exp(s - m)
        den = jnp.sum(e, axis=-1, keepdims=True)
        probs.append(e.astype(BF16))
        dens.append(den)
        lses.append(m + jnp.log(den))
    outs = [_dot(probs[h], v2[h // 2]) / dens[h] for h in heads]
    for p in range(n_pairs):
        o_ref[0, :, slab(p)] = jnp.where(first, outs[2 * p], outs[2 * p + 1]).astype(o_ref.dtype)
    lse = jnp.zeros((w, LANES), F32)
    for h in heads:
        lse = jnp.where(lane == h, lses[h], lse)
    lse_ref[0] = lse


def _dil_bias(rel_bias, group, dilation, n_back):
    w = n_back
    dist = jnp.arange(w + 1) * dilation
    max_exact = N_BUCKETS // 2
    d = jnp.maximum(dist.astype(F32), 1.0)
    large = max_exact + (jnp.log(d / max_exact) / math.log(BUCKET_MAX_DIST / max_exact)
                         * (N_BUCKETS - max_exact)).astype(jnp.int32)
    bucket = jnp.where(dist < max_exact, dist, jnp.minimum(large, N_BUCKETS - 1))
    per_m = rel_bias[:, group * DIL_HEADS:(group + 1) * DIL_HEADS][bucket].astype(F32).T
    period = jnp.concatenate(
        [per_m[:, ::-1], jnp.full((DIL_HEADS, w), NEG, F32)], axis=1)
    tiled = jnp.tile(period, (1, w))[:, :w * 2 * w]
    return tiled.reshape(DIL_HEADS, w, 2 * w)


def _dil_group_attention(qkv, bias, group, dilation, batch, seq):
    width = DIL_HEADS * HEAD_DIM
    length = seq // dilation
    w = DIL_TILE
    qv = qkv.reshape(batch * dilation, length, 3 * width)

    def cur(which):
        return pl.BlockSpec((1, w, width), lambda b, c, u: (b * dilation + c, u, which))

    def prev(which):
        return pl.BlockSpec((1, w, width),
                            lambda b, c, u: (b * dilation + c, jnp.maximum(u - 1, 0), which))

    at = lambda b, c, u: (b * dilation + c, u, 0)
    limit = _vmem_limit(
        5 * _nbytes((w, width), BF16) + _nbytes((DIL_HEADS, w, 2 * w), F32)
        + _nbytes((w, width), BF16) + _nbytes((w, LANES), F32),
        32 * _nbytes((w, 2 * w), F32))
    o, lse = pl.pallas_call(
        _dil_kernel,
        out_shape=(jax.ShapeDtypeStruct((batch * dilation, length, width), BF16),
                   jax.ShapeDtypeStruct((batch * dilation, length, LANES), F32)),
        grid=(batch, dilation, length // w),
        in_specs=[cur(0), prev(1), cur(1), prev(2), cur(2),
                  pl.BlockSpec((DIL_HEADS, w, 2 * w), lambda b, c, u: (0, 0, 0))],
        out_specs=(pl.BlockSpec((1, w, width), at), pl.BlockSpec((1, w, LANES), at)),
        compiler_params=pltpu.CompilerParams(
            dimension_semantics=("parallel", "parallel", "parallel"), vmem_limit_bytes=limit),
        name=f"dil_attention_g{group}",
    )(qv, qv, qv, qv, qv, bias)
    return o.reshape(batch * seq, width), lse.reshape(batch * seq, LANES)


def _dil_token_order_kernel(o_ref, l_ref, on_ref, ln_ref, *, dilation):
    seq = o_ref.shape[0]
    run = PERM_BLOCK // dilation
    length = seq // dilation
    row = lax.broadcasted_iota(jnp.int32, (PERM_BLOCK, PERM_BLOCK), 0)
    col = lax.broadcasted_iota(jnp.int32, (PERM_BLOCK, PERM_BLOCK), 1)
    src = (row & (dilation - 1)) * run + lax.shift_right_logical(row, dilation.bit_length() - 1)
    perm = jnp.where(col == src, 1.0, 0.0).astype(BF16)
    for blk in range(seq // PERM_BLOCK):
        gather = lambda ref: jnp.concatenate(
            [ref[c * length + blk * run:c * length + (blk + 1) * run, :] for c in range(dilation)],
            axis=0)
        rows = slice(blk * PERM_BLOCK, (blk + 1) * PERM_BLOCK)
        on_ref[rows, :] = _dot(perm, gather(o_ref)).astype(on_ref.dtype)
        l = gather(l_ref)
        hi = l.astype(BF16)
        rest = l - hi.astype(F32)
        mid = rest.astype(BF16)
        lo = (rest - mid.astype(F32)).astype(BF16)
        ln_ref[rows, :] = (_dot(perm, hi) + _dot(perm, mid)) + _dot(perm, lo)


def _dil_token_order(o, lse, dilation, batch, seq, name):
    width = o.shape[1]
    assert seq % PERM_BLOCK == 0 and PERM_BLOCK % (16 * dilation) == 0
    spec = lambda n: pl.BlockSpec((seq, n), lambda b: (b, 0))
    limit = _vmem_limit(2 * _nbytes((seq, width), BF16) + 2 * _nbytes((seq, LANES), F32),
                        8 * _nbytes((PERM_BLOCK, width), F32))
    return pl.pallas_call(
        functools.partial(_dil_token_order_kernel, dilation=dilation),
        out_shape=(jax.ShapeDtypeStruct(o.shape, o.dtype), jax.ShapeDtypeStruct(lse.shape, lse.dtype)),
        grid=(batch,),
        in_specs=[spec(width), spec(LANES)],
        out_specs=(spec(width), spec(LANES)),
        compiler_params=pltpu.CompilerParams(dimension_semantics=("parallel",),
                                             vmem_limit_bytes=limit),
        name=name,
    )(o, lse)


def _dil_merge_kernel(o0_ref, o1_ref, o2_ref, l0_ref, l1_ref, l2_ref, w_ref, x_ref, g_ref, out_ref):
    l0, l1, l2 = l0_ref[...], l1_ref[...], l2_ref[...]
    m = jnp.maximum(jnp.maximum(l0, l1), l2)
    e0, e1, e2 = jnp.exp(l0 - m), jnp.exp(l1 - m), jnp.exp(l2 - m)
    inv = 1.0 / (e0 + e1 + e2)
    width = o0_ref.shape[1]
    head = lax.broadcasted_iota(jnp.int32, (LANES, width), 0)
    lane = lax.broadcasted_iota(jnp.int32, (LANES, width), 1)
    owner = lax.shift_right_logical(lane, HEAD_DIM.bit_length() - 1)
    spread = jnp.where(owner == head, 1.0, 0.0).astype(BF16)

    def per_lane(wt):
        hi = wt.astype(BF16)
        lo = (wt - hi.astype(F32)).astype(BF16)
        return _dot(hi, spread) + _dot(lo, spread)

    o = (o0_ref[...].astype(F32) * per_lane(e0 * inv) + o1_ref[...].astype(F32) * per_lane(e1 * inv)
         + o2_ref[...].astype(F32) * per_lane(e2 * inv))
    mo = _dot(o.astype(BF16), w_ref[...])
    out_ref[...] = x_ref[...] + _rms_scale(mo) * g_ref[...]


def _dil_merge_proj(outs, lses, w, x, g):
    t, k = outs[0].shape
    d = w.shape[1]
    row = lambda i: (i, 0)
    limit = _vmem_limit(
        3 * _nbytes((ROW_TILE, k), BF16) + 3 * _nbytes((ROW_TILE, LANES), F32)
        + _nbytes((k, d), BF16) + 2 * _nbytes((ROW_TILE, d), F32),
        8 * _nbytes((ROW_TILE, k), F32) + 2 * _nbytes((ROW_TILE, d), F32))
    return pl.pallas_call(
        _dil_merge_kernel,
        out_shape=jax.ShapeDtypeStruct((t, d), F32),
        grid=(t // ROW_TILE,),
        in_specs=[pl.BlockSpec((ROW_TILE, k), row)] * 3 + [pl.BlockSpec((ROW_TILE, LANES), row)] * 3
        + [pl.BlockSpec((k, d), lambda i: (0, 0)),
           pl.BlockSpec((ROW_TILE, d), row),
           pl.BlockSpec((1, d), lambda i: (0, 0))],
        out_specs=pl.BlockSpec((ROW_TILE, d), row),
        compiler_params=pltpu.CompilerParams(dimension_semantics=("parallel",),
                                             vmem_limit_bytes=limit),
        name="dil_merge_proj",
    )(*outs, *lses, w, x, g)


def _mla_pre_kernel(x_ref, g_ref, wcq_ref, wckv_ref, wkr_ref, wkr_rot_ref, qn_ref, kvn_ref,
                    wq_ref, wq_rot_ref, wk_ref, wv_ref, cos_ref, sin_ref,
                    q_ref, k_ref, vt_ref, *, q_scale):
    xn = (_rms_scale(x_ref[...]) * g_ref[...]).astype(BF16)
    cq = (_rms_scale(_dot(xn, wcq_ref[...])) * qn_ref[...]).astype(BF16)
    ckv = (_rms_scale(_dot(xn, wckv_ref[...])) * kvn_ref[...]).astype(BF16)
    cos, sin = cos_ref[...], sin_ref[...]
    k_rope = _dot(xn, wkr_ref[...]) * cos + _dot(xn, wkr_rot_ref[...]) * sin
    cos2, sin2, k_rope2 = (jnp.concatenate([a, a], axis=1) for a in (cos, sin, k_rope))
    for p in range(MLA_HEADS // 2):
        sl = slice(2 * p * LANES, 2 * (p + 1) * LANES)
        q = _dot(cq, wq_ref[:, sl]) * cos2 + _dot(cq, wq_rot_ref[:, sl]) * sin2
        q_ref[:, sl] = (q * q_scale).astype(q_ref.dtype)
        k_ref[:, sl] = (_dot(ckv, wk_ref[:, sl]) + k_rope2).astype(k_ref.dtype)
    vt_ref[0] = _dot_nt(wv_ref[...], ckv).astype(vt_ref.dtype)


def _rope_rotation(w):
    half = MLA_ROPE // 2
    return jnp.concatenate([-w[..., half:], w[..., :half]], axis=-1)


def _pad_head_slabs(nope, rope):
    k, h = nope.shape[0], nope.shape[1]
    pad = jnp.zeros((k, h, LANES - MLA_NOPE - MLA_ROPE), nope.dtype)
    return jnp.concatenate([nope, rope, pad], axis=-1).reshape(k, h * LANES)


def _mla_pre(x, g, positions, w_in, q_norm, w_qb, kv_norm, w_kvb):
    t, d = x.shape
    h = MLA_HEADS
    w_cq = w_in[:, :MLA_Q_RANK].astype(BF16)
    w_ckv = w_in[:, MLA_Q_RANK:MLA_Q_RANK + MLA_KV_RANK].astype(BF16)
    w_kr = w_in[:, MLA_Q_RANK + MLA_KV_RANK:]
    zeros_nope = jnp.zeros((d, 1, MLA_NOPE), F32)
    w_kr_pad = _pad_head_slabs(zeros_nope, w_kr[:, None, :]).astype(BF16)
    w_kr_rot = _pad_head_slabs(zeros_nope, _rope_rotation(w_kr)[:, None, :]).astype(BF16)

    wq = w_qb.reshape(MLA_Q_RANK, h, MLA_NOPE + MLA_ROPE)
    wq_pad = _pad_head_slabs(wq[..., :MLA_NOPE], wq[..., MLA_NOPE:]).astype(BF16)
    wq_rot = _pad_head_slabs(jnp.zeros_like(wq[..., :MLA_NOPE]),
                             _rope_rotation(wq[..., MLA_NOPE:])).astype(BF16)
    wkv = w_kvb.reshape(MLA_KV_RANK, h, MLA_NOPE + MLA_V)
    wk_pad = _pad_head_slabs(wkv[..., :MLA_NOPE],
                             jnp.zeros((MLA_KV_RANK, h, MLA_ROPE), F32)).astype(BF16)
    wv_t = wkv[..., MLA_NOPE:].reshape(MLA_KV_RANK, h * MLA_V).T.astype(BF16)

    half = MLA_ROPE // 2
    freqs = ROPE_THETA ** (-jnp.arange(half, dtype=F32) / half)
    ang = positions.astype(F32).reshape(t, 1) * freqs
    cos, sin = jnp.cos(ang), jnp.sin(ang)
    tail = jnp.zeros((t, LANES - MLA_NOPE - MLA_ROPE), F32)
    cos_tab = jnp.concatenate([jnp.ones((t, MLA_NOPE), F32), cos, cos, tail], axis=-1)
    sin_tab = jnp.concatenate([jnp.zeros((t, MLA_NOPE), F32), sin, sin, tail], axis=-1)

    tm = ROW_TILE // 2
    full = lambda a: pl.BlockSpec(a.shape, lambda i: (0,) * a.ndim)
    row = lambda n: pl.BlockSpec((tm, n), lambda i: (i, 0))
    weights = (w_cq, w_ckv, w_kr_pad, w_kr_rot, q_norm.reshape(1, -1), kv_norm.reshape(1, -1),
               wq_pad, wq_rot, wk_pad, wv_t)
    assert tm == ATT_TILE
    limit = _vmem_limit(
        _nbytes((tm, d), F32) + sum(_nbytes(a.shape, a.dtype) for a in weights)
        + 2 * _nbytes((tm, LANES), F32) + 2 * _nbytes((tm, h * LANES), BF16)
        + _nbytes((tm, h * MLA_V), BF16),
        4 * _nbytes((tm, d), F32))
    return pl.pallas_call(
        functools.partial(_mla_pre_kernel, q_scale=(MLA_NOPE + MLA_ROPE) ** -0.5),
        out_shape=(jax.ShapeDtypeStruct((t, h * LANES), BF16),
                   jax.ShapeDtypeStruct((t, h * LANES), BF16),
                   jax.ShapeDtypeStruct((t // tm, h * MLA_V, tm), BF16)),
        grid=(t // tm,),
        in_specs=[row(d), full(g)] + [full(a) for a in weights] + [row(LANES), row(LANES)],
        out_specs=(row(h * LANES), row(h * LANES),
                   pl.BlockSpec((1, h * MLA_V, tm), lambda i: (i, 0, 0))),
        compiler_params=pltpu.CompilerParams(dimension_semantics=("parallel",),
                                             vmem_limit_bytes=limit),
        name="mla_pre",
    )(x, g, *weights, cos_tab, sin_tab)


def _mla_kernel(q_ref, k_ref, vt_ref, o_ref):
    tq = q_ref.shape[1]
    qi = pl.program_id(2)
    key = lax.broadcasted_iota(jnp.int32, (tq, tq), 0)
    query = lax.broadcasted_iota(jnp.int32, (tq, tq), 1)
    causal = key <= query
    n_heads = q_ref.shape[2] // LANES
    heads = range(n_heads)
    q_heads = [q_ref[0, :, h * LANES:(h + 1) * LANES] for h in heads]

    def all_heads(kb, st, mask):
        kblk = k_ref[0, pl.ds(pl.multiple_of(kb * tq, tq), tq), :]
        vt = vt_ref[kb]
        ss = [_dot_nt(kblk[:, h * LANES:(h + 1) * LANES], q_heads[h]) for h in heads]
        probs, stats = [], []
        for h in heads:
            s, (m, den) = ss[h], st[3 * h:3 * h + 2]
            if mask is not None:
                s = jnp.where(mask, s, NEG)
            m_new = jnp.maximum(m, jnp.max(s, axis=0, keepdims=True))
            alpha = jnp.exp(m - m_new)
            e = jnp.exp(s - m_new)
            stats.append((m_new, alpha * den + jnp.sum(e, axis=0, keepdims=True), alpha))
            probs.append(e.astype(BF16))
        out = []
        for h in heads:
            pv = _dot(vt[h * MLA_V:(h + 1) * MLA_V, :], probs[h])
            out.extend((stats[h][0], stats[h][1], stats[h][2] * st[3 * h + 2] + pv))
        return tuple(out)

    init = (jnp.full((1, tq), NEG, F32), jnp.zeros((1, tq), F32),
            jnp.zeros((MLA_V, tq), F32)) * n_heads
    st = all_heads(qi, init, causal)
    st = lax.fori_loop(0, qi, lambda n, st: all_heads(qi - 1 - n, st, None), st)
    for p in range(n_heads // 2):
        a, b = 2 * p, 2 * p + 1
        pair_t = jnp.concatenate([st[3 * a + 2] / st[3 * a + 1], st[3 * b + 2] / st[3 * b + 1]],
                                 axis=0)
        o_ref[0, :, p * LANES:(p + 1) * LANES] = pair_t.T.astype(o_ref.dtype)


MLA_HEADS_PER_STEP = 16


def _mla_attention(q, k, vt, batch, seq):
    nh = MLA_HEADS_PER_STEP
    steps = MLA_HEADS // nh
    tq = ATT_TILE
    nb = seq // tq
    limit = _vmem_limit(
        _nbytes((tq, nh * LANES), BF16) + _nbytes((seq, nh * LANES), BF16)
        + _nbytes((seq, nh * MLA_V), BF16) + _nbytes((tq, nh * MLA_V), BF16),
        8 * nh * _nbytes((tq, tq), F32))
    return pl.pallas_call(
        _mla_kernel,
        out_shape=jax.ShapeDtypeStruct((batch, seq, MLA_HEADS * MLA_V), BF16),
        grid=(batch, steps, nb),
        in_specs=[pl.BlockSpec((1, tq, nh * LANES), lambda b, p, i: (b, i, p)),
                  pl.BlockSpec((1, seq, nh * LANES), lambda b, p, i: (b, 0, p)),
                  pl.BlockSpec((nb, nh * MLA_V, tq), lambda b, p, i: (b, p, 0))],
        out_specs=pl.BlockSpec((1, tq, nh * MLA_V), lambda b, p, i: (b, i, p)),
        compiler_params=pltpu.CompilerParams(
            dimension_semantics=("parallel", "parallel", "parallel"), vmem_limit_bytes=limit),
        name="mla_attention",
    )(q, k, vt)


def _ffn_kernel(layer_ref, x_ref, g_in_ref, wg_ref, wu_ref, cwg_ref, cwu_ref, cbg_ref, cbu_ref,
                wd_ref, g_out_ref, o_ref, tail_g_ref, tail_u_ref, acc_ref, *, tiles_per_seq):
    del layer_ref
    tm = x_ref.shape[0]

    @pl.when(pl.program_id(0) % tiles_per_seq == 0)
    def _():
        tail_g_ref[...] = jnp.zeros_like(tail_g_ref)
        tail_u_ref[...] = jnp.zeros_like(tail_u_ref)

    x = x_ref[...]
    xn = (_rms_scale(x) * g_in_ref[...]).astype(BF16)
    row = lax.broadcasted_iota(jnp.int32, (tm, FF_CHUNK), 0)

    def conv(h, tail_ref, cw_ref, cb_ref, sl):
        p1 = tail_ref[7:8, sl]
        p2 = tail_ref[6:7, sl]
        h1 = jnp.where(row == 0, p1, pltpu.roll(h, 1, 0))
        h2 = jnp.where(row == 0, p2, jnp.where(row == 1, p1, pltpu.roll(h, 2, 0)))
        tail_ref[:, sl] = h[tm - 8:, :]
        return cw_ref[0:1, sl] * h2 + cw_ref[1:2, sl] * h1 + cw_ref[2:3, sl] * h + cb_ref[:, sl]

    chunk = lambda c: slice(c * FF_CHUNK, (c + 1) * FF_CHUNK)
    n_chunks = wg_ref.shape[1] // FF_CHUNK
    pending = None
    for c in range(n_chunks):
        groups = [slice(r * tm // FFN_ROW_GROUPS, (r + 1) * tm // FFN_ROW_GROUPS)
                  for r in range(FFN_ROW_GROUPS)]
        hg = jnp.concatenate([_dot(xn[rs], wg_ref[:, chunk(c)]) for rs in groups], axis=0)
        hu = jnp.concatenate([_dot(xn[rs], wu_ref[:, chunk(c)]) for rs in groups], axis=0)
        if pending is not None:
            part = _dot(pending[1], wd_ref[chunk(pending[0]), :])
            if pending[0] == 0:
                acc_ref[...] = part
            else:
                acc_ref[...] += part
        gate = conv(hg, tail_g_ref, cwg_ref, cbg_ref, chunk(c))
        val = conv(hu, tail_u_ref, cwu_ref, cbu_ref, chunk(c))
        pending = (c, (gate * (1.0 / (1.0 + jnp.exp(-gate))) * val).astype(BF16))
    m = acc_ref[...] + _dot(pending[1], wd_ref[chunk(pending[0]), :])
    o_ref[...] = x + _rms_scale(m) * g_out_ref[...]


def _ffn(x, layer, gains, w_up, conv_w, conv_b, w_down, seq):
    t, d = x.shape
    tm = FFN_ROW_TILE
    assert seq % tm == 0 and D_FF % FF_CHUNK == 0

    def resident(rows, cols, at):
        return pl.BlockSpec((None, rows, cols), lambda i, layer_ref: at(layer_ref[0]),
                            pipeline_mode=pl.Buffered(1))

    gain = lambda n: resident(1, d, lambda l: (4 * l + n, 0, 0))
    halves = lambda rows: [resident(rows, D_FF, lambda l, h=h: (l, 0, h)) for h in (0, 1)]
    row = pl.BlockSpec((tm, d), lambda i, layer_ref: (i, 0))
    limit = _vmem_limit(
        2 * _nbytes((tm, d), F32),
        3 * _nbytes((d, D_FF), BF16) + 8 * _nbytes((8, D_FF), F32)
        + 3 * _nbytes((tm, d), F32) + 16 * _nbytes((tm, FF_CHUNK), F32))
    return pl.pallas_call(
        functools.partial(_ffn_kernel, tiles_per_seq=seq // tm),
        out_shape=jax.ShapeDtypeStruct((t, d), F32),
        grid_spec=pltpu.PrefetchScalarGridSpec(
            num_scalar_prefetch=1,
            grid=(t // tm,),
            in_specs=[row, gain(2)] + halves(d) + halves(3) + halves(1)
            + [resident(D_FF, d, lambda l: (l, 0, 0)), gain(3)],
            out_specs=row,
            scratch_shapes=[pltpu.VMEM((8, D_FF), F32), pltpu.VMEM((8, D_FF), F32),
                            pltpu.VMEM((tm, d), F32)]),
        compiler_params=pltpu.CompilerParams(dimension_semantics=("arbitrary",),
                                             vmem_limit_bytes=limit),
        name="conv_ffn",
    )(jnp.full((1,), layer, jnp.int32), x, gains, w_up, w_up, conv_w, conv_w, conv_b, conv_b,
      w_down, gains)


def kernel(x, positions, rel_bias, norm_gains, sb_w_qkv, sb_w_o, dil_w_qkv, dil_w_o, mla_w_in,
           mla_q_norm, mla_w_qb, mla_kv_norm, mla_w_kvb, mla_w_o, ffn_w_up, ffn_conv_w,
           ffn_conv_b, ffn_w_down):
    batch, seq, d = x.shape
    t = batch * seq
    x = x.reshape(t, d)
    gains3 = norm_gains.reshape(4 * DEPTH, 1, d)
    w_up_bf16, w_down_bf16 = ffn_w_up.astype(BF16), ffn_w_down.astype(BF16)
    conv_b3 = ffn_conv_b.reshape(DEPTH, 1, 2 * D_FF)
    for i in range(DEPTH):
        kind, j = i % N_MIXERS, i // N_MIXERS
        gain = lambda n: norm_gains[i, n].reshape(1, d)
        if kind == 0:
            qk, vt = _sb_proj(x, gain(0), sb_w_qkv[j])
            o = _sb_attention(qk.reshape(batch, seq, -1), vt, batch, seq).reshape(t, -1)
            x = _proj_post(o, sb_w_o[j].astype(BF16), x, gain(1), "sb_out_proj")
        elif kind == 1:
            width = DIL_HEADS * HEAD_DIM
            w_groups = dil_w_qkv[j].reshape(d, 3, len(DIL_GROUPS), width).astype(BF16)
            outs, lses = [], []
            for grp, (window, dilation) in enumerate(DIL_GROUPS):
                n_back = window // dilation
                assert n_back == DIL_TILE and (seq // dilation) % DIL_TILE == 0
                qkv = _dil_proj(x, gain(0), w_groups[:, :, grp].reshape(d, 3 * width), dilation,
                                batch, seq, f"dil_qkv_proj_g{grp}")
                bias = _dil_bias(rel_bias, grp, dilation, n_back)
                o, lse = _dil_group_attention(qkv, bias, grp, dilation, batch, seq)
                if dilation > 1:
                    o, lse = _dil_token_order(o, lse, dilation, batch, seq,
                                              f"dil_token_order_g{grp}")
                outs.append(o)
                lses.append(lse)
            x = _dil_merge_proj(outs, lses, dil_w_o[j].astype(BF16), x, gain(1))
        else:
            q, k, vt = _mla_pre(x, gain(0), positions, mla_w_in[j], mla_q_norm[j], mla_w_qb[j],
                                mla_kv_norm[j], mla_w_kvb[j])
            o = _mla_attention(q.reshape(batch, seq, -1), k.reshape(batch, seq, -1),
                               vt, batch, seq).reshape(t, -1)
            x = _proj_post(o, mla_w_o[j].astype(BF16), x, gain(1), "mla_out_proj")
        x = _ffn(x, i, gains3, w_up_bf16, ffn_conv_w, conv_b3, w_down_bf16, seq)
    return x.reshape(batch, seq, d)
```

```python
import functools
import math

import jax
import jax.numpy as jnp
from jax import lax
from jax.experimental import pallas as pl
from jax.experimental.pallas import tpu as pltpu

F32 = jnp.float32
BF16 = jnp.bfloat16

D_MODEL = 1024
DEPTH = 4
N_MIXERS = 3
EPS = 1e-6
NEG = -1e30

SB_HEADS = 16
HEAD_DIM = 64

DIL_GROUPS = ((128, 1), (512, 4), (2048, 16))
DIL_HEADS = 8
N_BUCKETS = 32
BUCKET_MAX_DIST = 2048

MLA_HEADS = 16
MLA_Q_RANK = 384
MLA_KV_RANK = 256
MLA_NOPE = 64
MLA_ROPE = 32
MLA_V = 64
ROPE_THETA = 10000.0

D_FF = 2816

LANES = 128
V7X_VMEM_CAP_BYTES = 56 * 1024 * 1024

ROW_TILE = 512
PROJ_ROW_TILE = 1024
ATT_TILE = 256
DIL_TILE = 128
FF_CHUNK = 256
FFN_ROW_TILE = 1024
FFN_ROW_GROUPS = 8


def _vmem_limit(pipelined_bytes, resident_bytes):
    return int(min(V7X_VMEM_CAP_BYTES, 2 * pipelined_bytes + resident_bytes))


def _nbytes(shape, dtype):
    return math.prod(shape) * jnp.dtype(dtype).itemsize


def _rms_scale(x):
    return x * lax.rsqrt(jnp.mean(x * x, axis=-1, keepdims=True) + EPS)


def _dot(a, b):
    return jnp.dot(a, b, preferred_element_type=F32)


def _neg_abs(x):
    bits = lax.bitcast_convert_type(x, jnp.uint32) | jnp.uint32(0x80000000)
    return lax.bitcast_convert_type(bits, F32)


def _dot_nt(a, b):
    return lax.dot_general(a, b, (((1,), (1,)), ((), ())), preferred_element_type=F32)


def _proj_post_kernel(a_ref, w_ref, x_ref, g_ref, o_ref):
    m = _dot(a_ref[...], w_ref[...])
    o_ref[...] = x_ref[...] + _rms_scale(m) * g_ref[...]


def _proj_post(a, w, x, g, name):
    t, k = a.shape
    d = w.shape[1]
    tm = PROJ_ROW_TILE
    limit = _vmem_limit(
        _nbytes((tm, k), BF16) + _nbytes((k, d), BF16) + 2 * _nbytes((tm, d), F32),
        2 * _nbytes((tm, d), F32))
    return pl.pallas_call(
        _proj_post_kernel,
        out_shape=jax.ShapeDtypeStruct((t, d), F32),
        grid=(t // tm,),
        in_specs=[pl.BlockSpec((tm, k), lambda i: (i, 0)),
                  pl.BlockSpec((k, d), lambda i: (0, 0)),
                  pl.BlockSpec((tm, d), lambda i: (i, 0)),
                  pl.BlockSpec((1, d), lambda i: (0, 0))],
        out_specs=pl.BlockSpec((tm, d), lambda i: (i, 0)),
        compiler_params=pltpu.CompilerParams(dimension_semantics=("parallel",),
                                             vmem_limit_bytes=limit),
        name=name,
    )(a, w, x, g)


def _sb_proj_kernel(x_ref, g_ref, wqk_ref, wvt_ref, qk_ref, vt_ref, *, n_chunk):
    xn = (_rms_scale(x_ref[...]) * g_ref[...]).astype(BF16)
    for c in range(qk_ref.shape[1] // n_chunk):
        sl = slice(c * n_chunk, (c + 1) * n_chunk)
        qk_ref[:, sl] = _dot(xn, wqk_ref[:, sl]).astype(qk_ref.dtype)
    tk = vt_ref.shape[2]
    for c in range(wvt_ref.shape[0] // n_chunk):
        sl = slice(c * n_chunk, (c + 1) * n_chunk)
        vt = _dot_nt(wvt_ref[sl, :], xn).astype(vt_ref.dtype)
        for j in range(vt_ref.shape[0]):
            vt_ref[j, sl, :] = vt[:, j * tk:(j + 1) * tk]


def _sb_proj(x, g, w_qkv):
    t, d = x.shape
    width = SB_HEADS * HEAD_DIM
    wqk = w_qkv[:, :2 * width].astype(BF16)
    wvt = w_qkv[:, 2 * width:].T.astype(BF16)
    n_chunk = 512
    tk = ATT_TILE
    assert t % ROW_TILE == 0 and ROW_TILE % tk == 0 and width % n_chunk == 0
    limit = _vmem_limit(
        _nbytes((ROW_TILE, d), F32) + _nbytes((d, 3 * width), BF16)
        + _nbytes((ROW_TILE, 3 * width), BF16),
        _nbytes((ROW_TILE, d), F32) + 2 * _nbytes((ROW_TILE, n_chunk), F32))
    return pl.pallas_call(
        functools.partial(_sb_proj_kernel, n_chunk=n_chunk),
        out_shape=(jax.ShapeDtypeStruct((t, 2 * width), BF16),
                   jax.ShapeDtypeStruct((t // tk, width, tk), BF16)),
        grid=(t // ROW_TILE,),
        in_specs=[pl.BlockSpec((ROW_TILE, d), lambda i: (i, 0)),
                  pl.BlockSpec((1, d), lambda i: (0, 0)),
                  pl.BlockSpec((d, 2 * width), lambda i: (0, 0)),
                  pl.BlockSpec((width, d), lambda i: (0, 0))],
        out_specs=(pl.BlockSpec((ROW_TILE, 2 * width), lambda i: (i, 0)),
                   pl.BlockSpec((ROW_TILE // tk, width, tk), lambda i: (i, 0, 0))),
        compiler_params=pltpu.CompilerParams(dimension_semantics=("parallel",),
                                             vmem_limit_bytes=limit),
        name="sb_qkv_proj",
    )(x, g, wqk, wvt)


SB_UNDERFLOW = 105.0
SB_HEADS_PER_STEP = 8


def _sb_kernel(q_ref, k_ref, vt_ref, o_ref):
    tq = q_ref.shape[1]
    n_pairs = q_ref.shape[2] // LANES
    qi = pl.program_id(2)
    lane = lax.broadcasted_iota(jnp.int32, (tq, LANES), 1)
    first = lane < HEAD_DIM
    q_heads = []
    for p in range(n_pairs):
        q2 = q_ref[0, :, p * LANES:(p + 1) * LANES] * jnp.asarray(HEAD_DIM ** -0.5, BF16)
        zero = jnp.zeros_like(q2)
        q_heads.append(jnp.where(first, q2, zero))
        q_heads.append(jnp.where(first, zero, q2))
    heads = range(len(q_heads))

    key = lax.broadcasted_iota(jnp.int32, (tq, tq), 0)
    other = lax.broadcasted_iota(jnp.int32, (tq, tq), 1)
    later = jnp.where(other > key, 1.0, 0.0).astype(BF16)
    causal = key < other

    def all_heads(kb, st, mask):
        kblk = k_ref[0, pl.ds(pl.multiple_of(kb * tq, tq), tq), :]
        vt = vt_ref[kb]
        pair = lambda h: slice((h // 2) * LANES, (h // 2 + 1) * LANES)
        zs = [_dot_nt(kblk[:, pair(h)], q_heads[h]) for h in heads]
        mids = []
        for h in heads:
            z = zs[h]
            softplus = jnp.maximum(z, 0.0) + jnp.log(1.0 + jnp.exp(_neg_abs(z)))
            base = (z - softplus) - st[2 * h]
            if mask is not None:
                softplus = jnp.where(mask, softplus, 0.0)
            mids.append((softplus.astype(BF16), base, jnp.sum(softplus, axis=0, keepdims=True)))
        betweens = [_dot(later, sp) for sp, _, _ in mids]
        weights = []
        for h in heads:
            a = jnp.exp(mids[h][1] - betweens[h])
            if mask is not None:
                a = jnp.where(mask, a, 0.0)
            weights.append(a.astype(BF16))
        out = []
        for h in heads:
            out.append(st[2 * h] + mids[h][2])
            out.append(st[2 * h + 1] + _dot(vt[h * HEAD_DIM:(h + 1) * HEAD_DIM, :], weights[h]))
        return tuple(out)

    def smallest_carry(st):
        return functools.reduce(jnp.minimum, [jnp.min(c) for c in st[0::2]])

    init = (jnp.zeros((1, tq), F32), jnp.zeros((HEAD_DIM, tq), F32)) * len(q_heads)
    st = all_heads(qi, init, causal)

    def more(carry):
        n, low, _ = carry
        return jnp.logical_and(n < qi, low < SB_UNDERFLOW)

    def body(carry):
        n, _, st = carry
        st = all_heads(qi - 1 - n, st, None)
        return n + 1, smallest_carry(st), st

    _, _, st = lax.while_loop(more, body, (jnp.int32(0), smallest_carry(st), st))
    for p in range(n_pairs):
        pair_t = jnp.concatenate([st[4 * p + 1], st[4 * p + 3]], axis=0)
        o_ref[0, :, p * LANES:(p + 1) * LANES] = pair_t.T.astype(o_ref.dtype)


def _sb_attention(qk, vt, batch, seq):
    width = SB_HEADS_PER_STEP * HEAD_DIM
    groups = SB_HEADS // SB_HEADS_PER_STEP
    tq = ATT_TILE
    nb = seq // tq
    limit = _vmem_limit(
        2 * _nbytes((tq, width), BF16) + 2 * _nbytes((seq, width), BF16),
        12 * SB_HEADS_PER_STEP * _nbytes((tq, tq), F32))
    return pl.pallas_call(
        _sb_kernel,
        out_shape=jax.ShapeDtypeStruct((batch, seq, SB_HEADS * HEAD_DIM), BF16),
        grid=(batch, groups, nb),
        in_specs=[pl.BlockSpec((1, tq, width), lambda b, p, i: (b, i, p)),
                  pl.BlockSpec((1, seq, width), lambda b, p, i: (b, 0, groups + p)),
                  pl.BlockSpec((nb, width, tq), lambda b, p, i: (b, p, 0))],
        out_specs=pl.BlockSpec((1, tq, width), lambda b, p, i: (b, i, p)),
        compiler_params=pltpu.CompilerParams(
            dimension_semantics=("parallel", "parallel", "parallel"), vmem_limit_bytes=limit),
        name="sb_attention",
    )(qk, qk, vt)


PERM_BLOCK = 256


def _dil_proj_kernel(x_ref, g_ref, w_ref, o_ref, xp_ref, *, dilation, n_chunk):
    seq = x_ref.shape[0]
    run = PERM_BLOCK // dilation
    length = seq // dilation
    if dilation > 1:
        row = lax.broadcasted_iota(jnp.int32, (PERM_BLOCK, PERM_BLOCK), 0)
        col = lax.broadcasted_iota(jnp.int32, (PERM_BLOCK, PERM_BLOCK), 1)
        src = (row & (run - 1)) * dilation + lax.shift_right_logical(row, run.bit_length() - 1)
        perm = jnp.where(col == src, 1.0, 0.0).astype(BF16)
    for blk in range(seq // PERM_BLOCK):
        rows = slice(blk * PERM_BLOCK, (blk + 1) * PERM_BLOCK)
        xn = (_rms_scale(x_ref[rows, :]) * g_ref[...]).astype(BF16)
        if dilation == 1:
            xp_ref[rows, :] = xn
        else:
            moved = _dot(perm, xn).astype(BF16)
            for c in range(dilation):
                dst = c * length + blk * run
                xp_ref[dst:dst + run, :] = moved[c * run:(c + 1) * run, :]
    for c in range(w_ref.shape[1] // n_chunk):
        sl = slice(c * n_chunk, (c + 1) * n_chunk)
        for m in range(seq // ROW_TILE):
            rows = slice(m * ROW_TILE, (m + 1) * ROW_TILE)
            o_ref[rows, sl] = _dot(xp_ref[rows, :], w_ref[:, sl]).astype(o_ref.dtype)


def _dil_proj(x, g, w, dilation, batch, seq, name):
    d = x.shape[1]
    n = w.shape[1]
    n_chunk = 512
    assert seq % PERM_BLOCK == 0 and PERM_BLOCK % (16 * dilation) == 0 and n % n_chunk == 0
    limit = _vmem_limit(
        _nbytes((seq, d), F32) + _nbytes((d, n), BF16) + _nbytes((seq, n), BF16),
        _nbytes((seq, d), BF16) + 4 * _nbytes((ROW_TILE, d), F32))
    return pl.pallas_call(
        functools.partial(_dil_proj_kernel, dilation=dilation, n_chunk=n_chunk),
        out_shape=jax.ShapeDtypeStruct((batch * seq, n), BF16),
        grid=(batch,),
        in_specs=[pl.BlockSpec((seq, d), lambda b: (b, 0)),
                  pl.BlockSpec((1, d), lambda b: (0, 0)),
                  pl.BlockSpec((d, n), lambda b: (0, 0))],
        out_specs=pl.BlockSpec((seq, n), lambda b: (b, 0)),
        scratch_shapes=[pltpu.VMEM((seq, d), BF16)],
        compiler_params=pltpu.CompilerParams(dimension_semantics=("parallel",),
                                             vmem_limit_bytes=limit),
        name=name,
    )(x, g, w)


def _dil_kernel(q_ref, kp_ref, kc_ref, vp_ref, vc_ref, bias_ref, o_ref, lse_ref):
    w = q_ref.shape[1]
    ut = pl.program_id(2)
    lane = lax.broadcasted_iota(jnp.int32, (w, LANES), 1)
    first = lane < HEAD_DIM
    col = lax.broadcasted_iota(jnp.int32, (w, 2 * w), 1)
    key_ok = jnp.logical_or(col >= w, ut > 0)
    n_pairs = q_ref.shape[2] // LANES
    slab = lambda p: slice(p * LANES, (p + 1) * LANES)
    q_heads, k2, v2 = [], [], []
    for p in range(n_pairs):
        q2 = q_ref[0, :, slab(p)] * jnp.asarray(HEAD_DIM ** -0.5, BF16)
        zero = jnp.zeros_like(q2)
        q_heads += [jnp.where(first, q2, zero), jnp.where(first, zero, q2)]
        k2.append(jnp.concatenate([kp_ref[0, :, slab(p)], kc_ref[0, :, slab(p)]], axis=0))
        v2.append(jnp.concatenate([vp_ref[0, :, slab(p)], vc_ref[0, :, slab(p)]], axis=0))
    heads = range(2 * n_pairs)
    scores = [_dot_nt(q_heads[h], k2[h // 2]) for h in heads]
    probs, dens, lses = [], [], []
    for h in heads:
        s = jnp.where(key_ok, scores[h] + bias_ref[h], NEG)
        m = jnp.max(s, axis=-1, keepdims=True)
        e = jnp.exp(s - m)
        den = jnp.sum(e, axis=-1, keepdims=True)
        probs.append(e.astype(BF16))
        dens.append(den)
        lses.append(m + jnp.log(den))
    outs = [_dot(probs[h], v2[h // 2]) / dens[h] for h in heads]
    for p in range(n_pairs):
        o_ref[0, :, slab(p)] = jnp.where(first, outs[2 * p], outs[2 * p + 1]).astype(o_ref.dtype)
    lse = jnp.zeros((w, LANES), F32)
    for h in heads:
        lse = jnp.where(lane == h, lses[h], lse)
    lse_ref[0] = lse


def _dil_bias(rel_bias, group, dilation, n_back):
    w = n_back
    dist = jnp.arange(w + 1) * dilation
    max_exact = N_BUCKETS // 2
    d = jnp.maximum(dist.astype(F32), 1.0)
    large = max_exact + (jnp.log(d / max_exact) / math.log(BUCKET_MAX_DIST / max_exact)
                         * (N_BUCKETS - max_exact)).astype(jnp.int32)
    bucket = jnp.where(dist < max_exact, dist, jnp.minimum(large, N_BUCKETS - 1))
    per_m = rel_bias[:, group * DIL_HEADS:(group + 1) * DIL_HEADS][bucket].astype(F32).T
    period = jnp.concatenate(
        [per_m[:, ::-1], jnp.full((DIL_HEADS, w), NEG, F32)], axis=1)
    tiled = jnp.tile(period, (1, w))[:, :w * 2 * w]
    return tiled.reshape(DIL_HEADS, w, 2 * w)


def _dil_group_attention(qkv, bias, group, dilation, batch, seq):
    width = DIL_HEADS * HEAD_DIM
    length = seq // dilation
    w = DIL_TILE
    qv = qkv.reshape(batch * dilation, length, 3 * width)

    def cur(which):
        return pl.BlockSpec((1, w, width), lambda b, c, u: (b * dilation + c, u, which))

    def prev(which):
        return pl.BlockSpec((1, w, width),
                            lambda b, c, u: (b * dilation + c, jnp.maximum(u - 1, 0), which))

    at = lambda b, c, u: (b * dilation + c, u, 0)
    limit = _vmem_limit(
        5 * _nbytes((w, width), BF16) + _nbytes((DIL_HEADS, w, 2 * w), F32)
        + _nbytes((w, width), BF16) + _nbytes((w, LANES), F32),
        32 * _nbytes((w, 2 * w), F32))
    o, lse = pl.pallas_call(
        _dil_kernel,
        out_shape=(jax.ShapeDtypeStruct((batch * dilation, length, width), BF16),
                   jax.ShapeDtypeStruct((batch * dilation, length, LANES), F32)),
        grid=(batch, dilation, length // w),
        in_specs=[cur(0), prev(1), cur(1), prev(2), cur(2),
                  pl.BlockSpec((DIL_HEADS, w, 2 * w), lambda b, c, u: (0, 0, 0))],
        out_specs=(pl.BlockSpec((1, w, width), at), pl.BlockSpec((1, w, LANES), at)),
        compiler_params=pltpu.CompilerParams(
            dimension_semantics=("parallel", "parallel", "parallel"), vmem_limit_bytes=limit),
        name=f"dil_attention_g{group}",
    )(qv, qv, qv, qv, qv, bias)
    return o.reshape(batch * seq, width), lse.reshape(batch * seq, LANES)


def _dil_token_order_kernel(o_ref, l_ref, on_ref, ln_ref, *, dilation):
    seq = o_ref.shape[0]
    run = PERM_BLOCK // dilation
    length = seq // dilation
    row = lax.broadcasted_iota(jnp.int32, (PERM_BLOCK, PERM_BLOCK), 0)
    col = lax.broadcasted_iota(jnp.int32, (PERM_BLOCK, PERM_BLOCK), 1)
    src = (row & (dilation - 1)) * run + lax.shift_right_logical(row, dilation.bit_length() - 1)
    perm = jnp.where(col == src, 1.0, 0.0).astype(BF16)
    for blk in range(seq // PERM_BLOCK):
        gather = lambda ref: jnp.concatenate(
            [ref[c * length + blk * run:c * length + (blk + 1) * run, :] for c in range(dilation)],
            axis=0)
        rows = slice(blk * PERM_BLOCK, (blk + 1) * PERM_BLOCK)
        on_ref[rows, :] = _dot(perm, gather(o_ref)).astype(on_ref.dtype)
        l = gather(l_ref)
        hi = l.astype(BF16)
        rest = l - hi.astype(F32)
        mid = rest.astype(BF16)
        lo = (rest - mid.astype(F32)).astype(BF16)
        ln_ref[rows, :] = (_dot(perm, hi) + _dot(perm, mid)) + _dot(perm, lo)


def _dil_token_order(o, lse, dilation, batch, seq, name):
    width = o.shape[1]
    assert seq % PERM_BLOCK == 0 and PERM_BLOCK % (16 * dilation) == 0
    spec = lambda n: pl.BlockSpec((seq, n), lambda b: (b, 0))
    limit = _vmem_limit(2 * _nbytes((seq, width), BF16) + 2 * _nbytes((seq, LANES), F32),
                        8 * _nbytes((PERM_BLOCK, width), F32))
    return pl.pallas_call(
        functools.partial(_dil_token_order_kernel, dilation=dilation),
        out_shape=(jax.ShapeDtypeStruct(o.shape, o.dtype), jax.ShapeDtypeStruct(lse.shape, lse.dtype)),
        grid=(batch,),
        in_specs=[spec(width), spec(LANES)],
        out_specs=(spec(width), spec(LANES)),
        compiler_params=pltpu.CompilerParams(dimension_semantics=("parallel",),
                                             vmem_limit_bytes=limit),
        name=name,
    )(o, lse)


def _dil_merge_kernel(o0_ref, o1_ref, o2_ref, l0_ref, l1_ref, l2_ref, w_ref, x_ref, g_ref, out_ref):
    l0, l1, l2 = l0_ref[...], l1_ref[...], l2_ref[...]
    m = jnp.maximum(jnp.maximum(l0, l1), l2)
    e0, e1, e2 = jnp.exp(l0 - m), jnp.exp(l1 - m), jnp.exp(l2 - m)
    inv = 1.0 / (e0 + e1 + e2)
    width = o0_ref.shape[1]
    head = lax.broadcasted_iota(jnp.int32, (LANES, width), 0)
    lane = lax.broadcasted_iota(jnp.int32, (LANES, width), 1)
    owner = lax.shift_right_logical(lane, HEAD_DIM.bit_length() - 1)
    spread = jnp.where(owner == head, 1.0, 0.0).astype(BF16)

    def per_lane(wt):
        hi = wt.astype(BF16)
        lo = (wt - hi.astype(F32)).astype(BF16)
        return _dot(hi, spread) + _dot(lo, spread)

    o = (o0_ref[...].astype(F32) * per_lane(e0 * inv) + o1_ref[...].astype(F32) * per_lane(e1 * inv)
         + o2_ref[...].astype(F32) * per_lane(e2 * inv))
    mo = _dot(o.astype(BF16), w_ref[...])
    out_ref[...] = x_ref[...] + _rms_scale(mo) * g_ref[...]


def _dil_merge_proj(outs, lses, w, x, g):
    t, k = outs[0].shape
    d = w.shape[1]
    row = lambda i: (i, 0)
    limit = _vmem_limit(
        3 * _nbytes((ROW_TILE, k), BF16) + 3 * _nbytes((ROW_TILE, LANES), F32)
        + _nbytes((k, d), BF16) + 2 * _nbytes((ROW_TILE, d), F32),
        8 * _nbytes((ROW_TILE, k), F32) + 2 * _nbytes((ROW_TILE, d), F32))
    return pl.pallas_call(
        _dil_merge_kernel,
        out_shape=jax.ShapeDtypeStruct((t, d), F32),
        grid=(t // ROW_TILE,),
        in_specs=[pl.BlockSpec((ROW_TILE, k), row)] * 3 + [pl.BlockSpec((ROW_TILE, LANES), row)] * 3
        + [pl.BlockSpec((k, d), lambda i: (0, 0)),
           pl.BlockSpec((ROW_TILE, d), row),
           pl.BlockSpec((1, d), lambda i: (0, 0))],
        out_specs=pl.BlockSpec((ROW_TILE, d), row),
        compiler_params=pltpu.CompilerParams(dimension_semantics=("parallel",),
                                             vmem_limit_bytes=limit),
        name="dil_merge_proj",
    )(*outs, *lses, w, x, g)


def _mla_pre_kernel(x_ref, g_ref, wcq_ref, wckv_ref, wkr_ref, wkr_rot_ref, qn_ref, kvn_ref,
                    wq_ref, wq_rot_ref, wk_ref, wv_ref, cos_ref, sin_ref,
                    q_ref, k_ref, vt_ref, *, q_scale):
    xn = (_rms_scale(x_ref[...]) * g_ref[...]).astype(BF16)
    cq = (_rms_scale(_dot(xn, wcq_ref[...])) * qn_ref[...]).astype(BF16)
    ckv = (_rms_scale(_dot(xn, wckv_ref[...])) * kvn_ref[...]).astype(BF16)
    cos, sin = cos_ref[...], sin_ref[...]
    k_rope = _dot(xn, wkr_ref[...]) * cos + _dot(xn, wkr_rot_ref[...]) * sin
    cos2, sin2, k_rope2 = (jnp.concatenate([a, a], axis=1) for a in (cos, sin, k_rope))
    for p in range(MLA_HEADS // 2):
        sl = slice(2 * p * LANES, 2 * (p + 1) * LANES)
        q = _dot(cq, wq_ref[:, sl]) * cos2 + _dot(cq, wq_rot_ref[:, sl]) * sin2
        q_ref[:, sl] = (q * q_scale).astype(q_ref.dtype)
        k_ref[:, sl] = (_dot(ckv, wk_ref[:, sl]) + k_rope2).astype(k_ref.dtype)
    vt_ref[0] = _dot_nt(wv_ref[...], ckv).astype(vt_ref.dtype)


def _rope_rotation(w):
    half = MLA_ROPE // 2
    return jnp.concatenate([-w[..., half:], w[..., :half]], axis=-1)


def _pad_head_slabs(nope, rope):
    k, h = nope.shape[0], nope.shape[1]
    pad = jnp.zeros((k, h, LANES - MLA_NOPE - MLA_ROPE), nope.dtype)
    return jnp.concatenate([nope, rope, pad], axis=-1).reshape(k, h * LANES)


def _mla_pre(x, g, positions, w_in, q_norm, w_qb, kv_norm, w_kvb):
    t, d = x.shape
    h = MLA_HEADS
    w_cq = w_in[:, :MLA_Q_RANK].astype(BF16)
    w_ckv = w_in[:, MLA_Q_RANK:MLA_Q_RANK + MLA_KV_RANK].astype(BF16)
    w_kr = w_in[:, MLA_Q_RANK + MLA_KV_RANK:]
    zeros_nope = jnp.zeros((d, 1, MLA_NOPE), F32)
    w_kr_pad = _pad_head_slabs(zeros_nope, w_kr[:, None, :]).astype(BF16)
    w_kr_rot = _pad_head_slabs(zeros_nope, _rope_rotation(w_kr)[:, None, :]).astype(BF16)

    wq = w_qb.reshape(MLA_Q_RANK, h, MLA_NOPE + MLA_ROPE)
    wq_pad = _pad_head_slabs(wq[..., :MLA_NOPE], wq[..., MLA_NOPE:]).astype(BF16)
    wq_rot = _pad_head_slabs(jnp.zeros_like(wq[..., :MLA_NOPE]),
                             _rope_rotation(wq[..., MLA_NOPE:])).astype(BF16)
    wkv = w_kvb.reshape(MLA_KV_RANK, h, MLA_NOPE + MLA_V)
    wk_pad = _pad_head_slabs(wkv[..., :MLA_NOPE],
                             jnp.zeros((MLA_KV_RANK, h, MLA_ROPE), F32)).astype(BF16)
    wv_t = wkv[..., MLA_NOPE:].reshape(MLA_KV_RANK, h * MLA_V).T.astype(BF16)

    half = MLA_ROPE // 2
    freqs = ROPE_THETA ** (-jnp.arange(half, dtype=F32) / half)
    ang = positions.astype(F32).reshape(t, 1) * freqs
    cos, sin = jnp.cos(ang), jnp.sin(ang)
    tail = jnp.zeros((t, LANES - MLA_NOPE - MLA_ROPE), F32)
    cos_tab = jnp.concatenate([jnp.ones((t, MLA_NOPE), F32), cos, cos, tail], axis=-1)
    sin_tab = jnp.concatenate([jnp.zeros((t, MLA_NOPE), F32), sin, sin, tail], axis=-1)

    tm = ROW_TILE // 2
    full = lambda a: pl.BlockSpec(a.shape, lambda i: (0,) * a.ndim)
    row = lambda n: pl.BlockSpec((tm, n), lambda i: (i, 0))
    weights = (w_cq, w_ckv, w_kr_pad, w_kr_rot, q_norm.reshape(1, -1), kv_norm.reshape(1, -1),
               wq_pad, wq_rot, wk_pad, wv_t)
    assert tm == ATT_TILE
    limit = _vmem_limit(
        _nbytes((tm, d), F32) + sum(_nbytes(a.shape, a.dtype) for a in weights)
        + 2 * _nbytes((tm, LANES), F32) + 2 * _nbytes((tm, h * LANES), BF16)
        + _nbytes((tm, h * MLA_V), BF16),
        4 * _nbytes((tm, d), F32))
    return pl.pallas_call(
        functools.partial(_mla_pre_kernel, q_scale=(MLA_NOPE + MLA_ROPE) ** -0.5),
        out_shape=(jax.ShapeDtypeStruct((t, h * LANES), BF16),
                   jax.ShapeDtypeStruct((t, h * LANES), BF16),
                   jax.ShapeDtypeStruct((t // tm, h * MLA_V, tm), BF16)),
        grid=(t // tm,),
        in_specs=[row(d), full(g)] + [full(a) for a in weights] + [row(LANES), row(LANES)],
        out_specs=(row(h * LANES), row(h * LANES),
                   pl.BlockSpec((1, h * MLA_V, tm), lambda i: (i, 0, 0))),
        compiler_params=pltpu.CompilerParams(dimension_semantics=("parallel",),
                                             vmem_limit_bytes=limit),
        name="mla_pre",
    )(x, g, *weights, cos_tab, sin_tab)


def _mla_kernel(q_ref, k_ref, vt_ref, o_ref):
    tq = q_ref.shape[1]
    qi = pl.program_id(2)
    key = lax.broadcasted_iota(jnp.int32, (tq, tq), 0)
    query = lax.broadcasted_iota(jnp.int32, (tq, tq), 1)
    causal = key <= query
    n_heads = q_ref.shape[2] // LANES
    heads = range(n_heads)
    q_heads = [q_ref[0, :, h * LANES:(h + 1) * LANES] for h in heads]

    def all_heads(kb, st, mask):
        kblk = k_ref[0, pl.ds(pl.multiple_of(kb * tq, tq), tq), :]
        vt = vt_ref[kb]
        ss = [_dot_nt(kblk[:, h * LANES:(h + 1) * LANES], q_heads[h]) for h in heads]
        probs, stats = [], []
        for h in heads:
            s, (m, den) = ss[h], st[3 * h:3 * h + 2]
            if mask is not None:
                s = jnp.where(mask, s, NEG)
            m_new = jnp.maximum(m, jnp.max(s, axis=0, keepdims=True))
            alpha = jnp.exp(m - m_new)
            e = jnp.exp(s - m_new)
            stats.append((m_new, alpha * den + jnp.sum(e, axis=0, keepdims=True), alpha))
            probs.append(e.astype(BF16))
        out = []
        for h in heads:
            pv = _dot(vt[h * MLA_V:(h + 1) * MLA_V, :], probs[h])
            out.extend((stats[h][0], stats[h][1], stats[h][2] * st[3 * h + 2] + pv))
        return tuple(out)

    init = (jnp.full((1, tq), NEG, F32), jnp.zeros((1, tq), F32),
            jnp.zeros((MLA_V, tq), F32)) * n_heads
    st = all_heads(qi, init, causal)
    st = lax.fori_loop(0, qi, lambda n, st: all_heads(qi - 1 - n, st, None), st)
    for p in range(n_heads // 2):
        a, b = 2 * p, 2 * p + 1
        pair_t = jnp.concatenate([st[3 * a + 2] / st[3 * a + 1], st[3 * b + 2] / st[3 * b + 1]],
                                 axis=0)
        o_ref[0, :, p * LANES:(p + 1) * LANES] = pair_t.T.astype(o_ref.dtype)


MLA_HEADS_PER_STEP = 16


def _mla_attention(q, k, vt, batch, seq):
    nh = MLA_HEADS_PER_STEP
    steps = MLA_HEADS // nh
    tq = ATT_TILE
    nb = seq // tq
    limit = _vmem_limit(
        _nbytes((tq, nh * LANES), BF16) + _nbytes((seq, nh * LANES), BF16)
        + _nbytes((seq, nh * MLA_V), BF16) + _nbytes((tq, nh * MLA_V), BF16),
        8 * nh * _nbytes((tq, tq), F32))
    return pl.pallas_call(
        _mla_kernel,
        out_shape=jax.ShapeDtypeStruct((batch, seq, MLA_HEADS * MLA_V), BF16),
        grid=(batch, steps, nb),
        in_specs=[pl.BlockSpec((1, tq, nh * LANES), lambda b, p, i: (b, i, p)),
                  pl.BlockSpec((1, seq, nh * LANES), lambda b, p, i: (b, 0, p)),
                  pl.BlockSpec((nb, nh * MLA_V, tq), lambda b, p, i: (b, p, 0))],
        out_specs=pl.BlockSpec((1, tq, nh * MLA_V), lambda b, p, i: (b, i, p)),
        compiler_params=pltpu.CompilerParams(
            dimension_semantics=("parallel", "parallel", "parallel"), vmem_limit_bytes=limit),
        name="mla_attention",
    )(q, k, vt)


def _ffn_kernel(layer_ref, x_ref, g_in_ref, wg_ref, wu_ref, cwg_ref, cwu_ref, cbg_ref, cbu_ref,
                wd_ref, g_out_ref, o_ref, tail_g_ref, tail_u_ref, acc_ref, *, tiles_per_seq):
    del layer_ref
    tm = x_ref.shape[0]

    @pl.when(pl.program_id(0) % tiles_per_seq == 0)
    def _():
        tail_g_ref[...] = jnp.zeros_like(tail_g_ref)
        tail_u_ref[...] = jnp.zeros_like(tail_u_ref)

    x = x_ref[...]
    xn = (_rms_scale(x) * g_in_ref[...]).astype(BF16)
    row = lax.broadcasted_iota(jnp.int32, (tm, FF_CHUNK), 0)

    def conv(h, tail_ref, cw_ref, cb_ref, sl):
        p1 = tail_ref[7:8, sl]
        p2 = tail_ref[6:7, sl]
        h1 = jnp.where(row == 0, p1, pltpu.roll(h, 1, 0))
        h2 = jnp.where(row == 0, p2, jnp.where(row == 1, p1, pltpu.roll(h, 2, 0)))
        tail_ref[:, sl] = h[tm - 8:, :]
        return cw_ref[0:1, sl] * h2 + cw_ref[1:2, sl] * h1 + cw_ref[2:3, sl] * h + cb_ref[:, sl]

    chunk = lambda c: slice(c * FF_CHUNK, (c + 1) * FF_CHUNK)
    n_chunks = wg_ref.shape[1] // FF_CHUNK
    pending = None
    for c in range(n_chunks):
        groups = [slice(r * tm // FFN_ROW_GROUPS, (r + 1) * tm // FFN_ROW_GROUPS)
                  for r in range(FFN_ROW_GROUPS)]
        hg = jnp.concatenate([_dot(xn[rs], wg_ref[:, chunk(c)]) for rs in groups], axis=0)
        hu = jnp.concatenate([_dot(xn[rs], wu_ref[:, chunk(c)]) for rs in groups], axis=0)
        if pending is not None:
            part = _dot(pending[1], wd_ref[chunk(pending[0]), :])
            if pending[0] == 0:
                acc_ref[...] = part
            else:
                acc_ref[...] += part
        gate = conv(hg, tail_g_ref, cwg_ref, cbg_ref, chunk(c))
        val = conv(hu, tail_u_ref, cwu_ref, cbu_ref, chunk(c))
        pending = (c, (gate * (1.0 / (1.0 + jnp.exp(-gate))) * val).astype(BF16))
    m = acc_ref[...] + _dot(pending[1], wd_ref[chunk(pending[0]), :])
    o_ref[...] = x + _rms_scale(m) * g_out_ref[...]


def _ffn(x, layer, gains, w_up, conv_w, conv_b, w_down, seq):
    t, d = x.shape
    tm = FFN_ROW_TILE
    assert seq % tm == 0 and D_FF % FF_CHUNK == 0

    def resident(rows, cols, at):
        return pl.BlockSpec((None, rows, cols), lambda i, layer_ref: at(layer_ref[0]),
                            pipeline_mode=pl.Buffered(1))

    gain = lambda n: resident(1, d, lambda l: (4 * l + n, 0, 0))
    halves = lambda rows: [resident(rows, D_FF, lambda l, h=h: (l, 0, h)) for h in (0, 1)]
    row = pl.BlockSpec((tm, d), lambda i, layer_ref: (i, 0))
    limit = _vmem_limit(
        2 * _nbytes((tm, d), F32),
        3 * _nbytes((d, D_FF), BF16) + 8 * _nbytes((8, D_FF), F32)
        + 3 * _nbytes((tm, d), F32) + 16 * _nbytes((tm, FF_CHUNK), F32))
    return pl.pallas_call(
        functools.partial(_ffn_kernel, tiles_per_seq=seq // tm),
        out_shape=jax.ShapeDtypeStruct((t, d), F32),
        grid_spec=pltpu.PrefetchScalarGridSpec(
            num_scalar_prefetch=1,
            grid=(t // tm,),
            in_specs=[row, gain(2)] + halves(d) + halves(3) + halves(1)
            + [resident(D_FF, d, lambda l: (l, 0, 0)), gain(3)],
            out_specs=row,
            scratch_shapes=[pltpu.VMEM((8, D_FF), F32), pltpu.VMEM((8, D_FF), F32),
                            pltpu.VMEM((tm, d), F32)]),
        compiler_params=pltpu.CompilerParams(dimension_semantics=("arbitrary",),
                                             vmem_limit_bytes=limit),
        name="conv_ffn",
    )(jnp.full((1,), layer, jnp.int32), x, gains, w_up, w_up, conv_w, conv_w, conv_b, conv_b,
      w_down, gains)


def kernel(x, positions, rel_bias, norm_gains, sb_w_qkv, sb_w_o, dil_w_qkv, dil_w_o, mla_w_in,
           mla_q_norm, mla_w_qb, mla_kv_norm, mla_w_kvb, mla_w_o, ffn_w_up, ffn_conv_w,
           ffn_conv_b, ffn_w_down):
    batch, seq, d = x.shape
    t = batch * seq
    x = x.reshape(t, d)
    gains3 = norm_gains.reshape(4 * DEPTH, 1, d)
    w_up_bf16, w_down_bf16 = ffn_w_up.astype(BF16), ffn_w_down.astype(BF16)
    conv_b3 = ffn_conv_b.reshape(DEPTH, 1, 2 * D_FF)
    for i in range(DEPTH):
        kind, j = i % N_MIXERS, i // N_MIXERS
        gain = lambda n: norm_gains[i, n].reshape(1, d)
        if kind == 0:
            qk, vt = _sb_proj(x, gain(0), sb_w_qkv[j])
            o = _sb_attention(qk.reshape(batch, seq, -1), vt, batch, seq).reshape(t, -1)
            x = _proj_post(o, sb_w_o[j].astype(BF16), x, gain(1), "sb_out_proj")
        elif kind == 1:
            width = DIL_HEADS * HEAD_DIM
            w_groups = dil_w_qkv[j].reshape(d, 3, len(DIL_GROUPS), width).astype(BF16)
            outs, lses = [], []
            for grp, (window, dilation) in enumerate(DIL_GROUPS):
                n_back = window // dilation
                assert n_back == DIL_TILE and (seq // dilation) % DIL_TILE == 0
                qkv = _dil_proj(x, gain(0), w_groups[:, :, grp].reshape(d, 3 * width), dilation,
                                batch, seq, f"dil_qkv_proj_g{grp}")
                bias = _dil_bias(rel_bias, grp, dilation, n_back)
                o, lse = _dil_group_attention(qkv, bias, grp, dilation, batch, seq)
                if dilation > 1:
                    o, lse = _dil_token_order(o, lse, dilation, batch, seq,
                                              f"dil_token_order_g{grp}")
                outs.append(o)
                lses.append(lse)
            x = _dil_merge_proj(outs, lses, dil_w_o[j].astype(BF16), x, gain(1))
        else:
            q, k, vt = _mla_pre(x, gain(0), positions, mla_w_in[j], mla_q_norm[j], mla_w_qb[j],
                                mla_kv_norm[j], mla_w_kvb[j])
            o = _mla_attention(q.reshape(batch, seq, -1), k.reshape(batch, seq, -1),
                               vt, batch, seq).reshape(t, -1)
            x = _proj_post(o, mla_w_o[j].astype(BF16), x, gain(1), "mla_out_proj")
        x = _ffn(x, i, gains3, w_up_bf16, ffn_conv_w, conv_b3, w_down_bf16, seq)
    return x.reshape(batch, seq, d)
```
